```python
import math
import jax, jax.numpy as jnp
from jax import lax
import numpy as np

D_MODEL = 2048
BATCH = 4
SEQ = 8192
DEPTH = 2

GRID_W = 64
CTX_LEN = 256
N_EVEN = (DEPTH + 1) // 2
N_ODD = DEPTH // 2

CONV_DIM = D_MODEL // 2
CONV_WIDTH = 3
DA_HEADS = 8
DA_HEAD_DIM = D_MODEL // (4 * DA_HEADS)
DA_V_DIM = 2 * DA_HEAD_DIM
Q_DIM = DA_HEADS * 2 * DA_HEAD_DIM
DA_WIDTH = DA_HEADS * DA_V_DIM
CONV_COLS = 3 * CONV_DIM
Q_START = CONV_COLS
K_START = Q_START + Q_DIM
V_START = K_START + Q_DIM
IN_E_COLS = V_START + DA_WIDTH
MIX_E_WIDTH = CONV_DIM + DA_WIDTH
ROPE_AXIS_DIM = DA_HEAD_DIM // 2
ROPE_BASE = 10000.0
Q_BLOCK = 128

GMLP_DIM = D_MODEL
CHUNK = 128
GMLP_GROUPS = 16
GMLP_GROUP_DIM = GMLP_DIM // GMLP_GROUPS

N_EXPERTS = 16
N_GROUPS = 4
EXPERTS_PER_GROUP = N_EXPERTS // N_GROUPS
TOPK_GROUPS = 1
TOP_K = 2
EXPERT_DIM = D_MODEL // 4

DEEPNORM_ALPHA = (2 * DEPTH) ** 0.25
DEEPNORM_BETA = (8 * DEPTH) ** -0.25
LN_EPS = 1e-5

kernel_name = "hybrid_diffusion_conv_diffattn_gmlp_moe"


def layer_norm(x, g, b):
    xf = x.astype(jnp.float32)
    mu = jnp.mean(xf, axis=-1, keepdims=True)
    var = jnp.mean(jnp.square(xf - mu), axis=-1, keepdims=True)
    y = (xf - mu) * lax.rsqrt(var + LN_EPS)
    return (y * g.astype(jnp.float32) + b.astype(jnp.float32)).astype(x.dtype)


def rms_norm(x, g):
    xf = x.astype(jnp.float32)
    y = xf * lax.rsqrt(jnp.mean(jnp.square(xf), axis=-1, keepdims=True) + LN_EPS)
    return (y * g.astype(jnp.float32)).astype(x.dtype)


def modulate(x, shift, scale):
    return x * (1 + scale) + shift


def axial_rope_tables(n, dtype):
    rows = n // GRID_W
    half = ROPE_AXIS_DIM // 2
    inv = jnp.power(ROPE_BASE, -jnp.arange(half, dtype=jnp.float32) / half)
    ang_r = jnp.arange(rows, dtype=jnp.float32)[:, None, None] * inv
    ang_c = jnp.arange(GRID_W, dtype=jnp.float32)[None, :, None] * inv
    ang = jnp.stack(jnp.broadcast_arrays(ang_r, ang_c), axis=2).reshape(n, 2, half)
    return jnp.cos(ang).astype(dtype), jnp.sin(ang).astype(dtype)


def apply_axial_rope(t, cos, sin):
    tt = t.reshape(t.shape[:-1] + (2, 2, ROPE_AXIS_DIM // 2))
    t1, t2 = tt[..., 0, :], tt[..., 1, :]
    cs, sn = cos[:, None, None], sin[:, None, None]
    r = jnp.stack([t1 * cs - t2 * sn, t2 * cs + t1 * sn], axis=-2)
    return r.reshape(t.shape)


def short_gated_conv(p3, conv_w):
    xa, bg, cg = jnp.split(p3, 3, axis=-1)
    z = jnp.pad(cg * xa, ((0, 0), (1, 1), (0, 0)))
    conv = conv_w[0] * z[:, :-2] + conv_w[1] * z[:, 1:-1] + conv_w[2] * z[:, 2:]
    return bg * conv


def qk_heads(t):
    return t.reshape(t.shape[:2] + (DA_HEADS, 2, DA_HEAD_DIM))


def v_heads(t):
    return t.reshape(t.shape[:2] + (DA_HEADS, DA_V_DIM))


def diff_softmax_mix(q, k, v, lam):
    s = jnp.einsum("bqhmd,bkhmd->bhmqk", q, k).astype(jnp.float32)
    p = jax.nn.softmax(s, axis=-1)
    a = p[:, :, 0] - lam * p[:, :, 1]
    return jnp.einsum("bhqk,bkhe->bqhe", a.astype(v.dtype), v)


def blocked_diff_attention(q, k, v, lam):
    b, n = q.shape[:2]
    qb = q.reshape((b, n // Q_BLOCK, Q_BLOCK) + q.shape[2:]).swapaxes(0, 1)
    out = lax.map(lambda qblk: diff_softmax_mix(qblk, k, v, lam), qb)
    return out.swapaxes(0, 1).reshape((b, n) + out.shape[3:])


def diff_attn_output(o, subln_g, lam_init):
    o = rms_norm(o, subln_g) * (1.0 - lam_init)
    return o.reshape(o.shape[:2] + (DA_WIDTH,))


def even_mixer(h, hc, w_in, conv_w, lam, lam_init, subln_g, w_out, cos, sin, need_ctx):
    scale = DA_HEAD_DIM ** -0.5
    proj = h @ w_in
    a = short_gated_conv(proj[..., :CONV_COLS], conv_w)
    q = apply_axial_rope(qk_heads(proj[..., Q_START:K_START]), cos, sin) * scale
    k = apply_axial_rope(qk_heads(proj[..., K_START:V_START]), cos, sin)
    v = v_heads(proj[..., V_START:])
    proj_c = hc @ w_in if need_ctx else None
    kv_c = proj_c[..., K_START:] if need_ctx else hc @ w_in[:, K_START:]
    kc = qk_heads(kv_c[..., :Q_DIM])
    vc = v_heads(kv_c[..., Q_DIM:])
    o = blocked_diff_attention(q, jnp.concatenate([kc, k], axis=1), jnp.concatenate([vc, v], axis=1), lam)
    y = jnp.concatenate([a, diff_attn_output(o, subln_g, lam_init)], axis=-1) @ w_out
    if not need_ctx:
        return y, None
    ac = short_gated_conv(proj_c[..., :CONV_COLS], conv_w)
    qc = qk_heads(proj_c[..., Q_START:K_START]) * scale
    oc = diff_softmax_mix(qc, kc, vc, lam)
    yc = jnp.concatenate([ac, diff_attn_output(oc, subln_g, lam_init)], axis=-1) @ w_out
    return y, yc


def chunk_gmlp(h, w_in, v_g, v_b, w_s, b_s, w_out):
    b, n, _ = h.shape
    u, v = jnp.split(jax.nn.gelu(h @ w_in), 2, axis=-1)
    v = layer_norm(v, v_g, v_b).reshape(b, n // CHUNK, CHUNK, GMLP_GROUPS, GMLP_GROUP_DIM)
    s = jnp.einsum("gpq,bnqgc->bnpgc", w_s, v) + b_s.T[None, None, :, :, None]
    return (u * s.reshape(b, n, GMLP_DIM)) @ w_out


def moe(h, w_router, router_bias, w_gate, w_up, w_down):
    t = h.shape[0]
    scores = jax.nn.sigmoid((h @ w_router).astype(jnp.float32))
    sel = scores + router_bias.astype(jnp.float32)
    grp_score = lax.top_k(sel.reshape(t, N_GROUPS, EXPERTS_PER_GROUP), 2)[0].sum(-1)
    _, gidx = lax.top_k(grp_score, TOPK_GROUPS)
    gmask = jnp.any(gidx[:, :, None] == jnp.arange(N_GROUPS)[None, None, :], axis=1)
    emask = jnp.repeat(gmask, EXPERTS_PER_GROUP, axis=1)
    _, eidx = lax.top_k(jnp.where(emask, sel, -jnp.inf), TOP_K)
    wsel = jnp.take_along_axis(scores, eidx, axis=1)
    wsel = wsel / jnp.sum(wsel, axis=-1, keepdims=True)
    combine = jnp.sum(jax.nn.one_hot(eidx, N_EXPERTS, dtype=jnp.float32) * wsel[..., None], axis=1).astype(h.dtype)
    out = jnp.zeros_like(h)
    for e in range(N_EXPERTS):
        hid = jax.nn.silu(h @ w_gate[e]) * (h @ w_up[e])
        out = out + combine[:, e:e + 1] * (hid @ w_down[e])
    return out


def setup_inputs(seed: int = 0) -> dict:
    key = jax.random.key(seed)
    ks = jax.random.split(key, 32)
    f32 = jnp.float32

    def nrm(k, shape, scale):
        return jax.random.normal(k, shape, f32) * scale

    d = D_MODEL
    return {
        "x": nrm(ks[0], (BATCH, SEQ, d), 1.0),
        "c": nrm(ks[1], (BATCH, d), 1.0),
        "ctx": nrm(ks[2], (BATCH, CTX_LEN, d), 1.0),
        "c_ctx": nrm(ks[3], (d,), 1.0),
        "w_mod": nrm(ks[4], (DEPTH, d, 6 * d), 0.5 * d ** -0.5),
        "b_mod": nrm(ks[5], (DEPTH, 6 * d), 0.02),
        "ln_g": 1.0 + nrm(ks[6], (DEPTH, 2, d), 0.02),
        "ln_b": nrm(ks[7], (DEPTH, 2, d), 0.02),
        "w_in_e": nrm(ks[8], (N_EVEN, d, IN_E_COLS), d ** -0.5),
        "conv_w": nrm(ks[9], (N_EVEN, CONV_WIDTH, CONV_DIM), CONV_WIDTH ** -0.5),
        "lambda_q1": nrm(ks[10], (N_EVEN, DA_HEAD_DIM), 0.1),
        "lambda_k1": nrm(ks[11], (N_EVEN, DA_HEAD_DIM), 0.1),
        "lambda_q2": nrm(ks[12], (N_EVEN, DA_HEAD_DIM), 0.1),
        "lambda_k2": nrm(ks[13], (N_EVEN, DA_HEAD_DIM), 0.1),
        "subln_g": 1.0 + nrm(ks[14], (N_EVEN, DA_V_DIM), 0.02),
        "w_out_e": nrm(ks[15], (N_EVEN, MIX_E_WIDTH, d), DEEPNORM_BETA * MIX_E_WIDTH ** -0.5),
        "w_in_o": nrm(ks[16], (N_ODD, d, 2 * GMLP_DIM), d ** -0.5),
        "v_ln_g": 1.0 + nrm(ks[17], (N_ODD, GMLP_DIM), 0.02),
        "v_ln_b": nrm(ks[18], (N_ODD, GMLP_DIM), 0.02),
        "w_spatial": nrm(ks[19], (N_ODD, GMLP_GROUPS, CHUNK, CHUNK), CHUNK ** -0.5),
        "b_spatial": 1.0 + nrm(ks[20], (N_ODD, GMLP_GROUPS, CHUNK), 0.02),
        "w_out_o": nrm(ks[21], (N_ODD, GMLP_DIM, d), DEEPNORM_BETA * GMLP_DIM ** -0.5),
        "w_router": nrm(ks[22], (d, N_EXPERTS), d ** -0.5),
        "router_bias": nrm(ks[23], (N_EXPERTS,), 0.01),
        "w_gate": nrm(ks[24], (DEPTH, N_EXPERTS, d, EXPERT_DIM), d ** -0.5),
        "w_up": nrm(ks[25], (DEPTH, N_EXPERTS, d, EXPERT_DIM), d ** -0.5),
        "w_down": nrm(ks[26], (DEPTH, N_EXPERTS, EXPERT_DIM, d), DEEPNORM_BETA * EXPERT_DIM ** -0.5),
    }


def reference(x, c, ctx, c_ctx, w_mod, b_mod, ln_g, ln_b, w_in_e, conv_w, lambda_q1, lambda_k1,
              lambda_q2, lambda_k2, subln_g, w_out_e, w_in_o, v_ln_g, v_ln_b, w_spatial, b_spatial,
              w_out_o, w_router, router_bias, w_gate, w_up, w_down):
    bsz, n, d = x.shape
    cos, sin = axial_rope_tables(n, x.dtype)
    xc = ctx
    for l in range(DEPTH):
        need_ctx = any(j % 2 == 0 for j in range(l + 1, DEPTH))
        mod = (jax.nn.silu(c) @ w_mod[l] + b_mod[l]).reshape(bsz, 6, 1, d)
        sh1, sc1, g1, sh2, sc2, g2 = (mod[:, i] for i in range(6))
        modc = (jax.nn.silu(c_ctx) @ w_mod[l] + b_mod[l]).reshape(6, d)

        h = modulate(x, sh1, sc1)
        hc = modulate(xc, modc[0], modc[1])
        if l % 2 == 0:
            e = l // 2
            lam_init = 0.8 - 0.6 * math.exp(-0.3 * l)
            lam = (jnp.exp(jnp.sum(lambda_q1[e].astype(jnp.float32) * lambda_k1[e].astype(jnp.float32)))
                   - jnp.exp(jnp.sum(lambda_q2[e].astype(jnp.float32) * lambda_k2[e].astype(jnp.float32)))
                   + lam_init)
            y, yc = even_mixer(h, hc, w_in_e[e], conv_w[e], lam, lam_init, subln_g[e], w_out_e[e],
                               cos, sin, need_ctx)
        else:
            o = l // 2
            y = chunk_gmlp(h, w_in_o[o], v_ln_g[o], v_ln_b[o], w_spatial[o], b_spatial[o], w_out_o[o])
            yc = (chunk_gmlp(hc, w_in_o[o], v_ln_g[o], v_ln_b[o], w_spatial[o], b_spatial[o], w_out_o[o])
                  if need_ctx else None)
        x = layer_norm(DEEPNORM_ALPHA * x + g1 * y, ln_g[l, 0], ln_b[l, 0])
        if need_ctx:
            xc = layer_norm(DEEPNORM_ALPHA * xc + modc[2] * yc, ln_g[l, 0], ln_b[l, 0])

        h = modulate(x, sh2, sc2)
        y = moe(h.reshape(-1, d), w_router, router_bias, w_gate[l], w_up[l], w_down[l]).reshape(x.shape)
        x = layer_norm(DEEPNORM_ALPHA * x + g2 * y, ln_g[l, 1], ln_b[l, 1])
        if need_ctx:
            hc = modulate(xc, modc[3], modc[4])
            yc = moe(hc.reshape(-1, d), w_router, router_bias, w_gate[l], w_up[l], w_down[l]).reshape(xc.shape)
            xc = layer_norm(DEEPNORM_ALPHA * xc + modc[5] * yc, ln_g[l, 1], ln_b[l, 1])
    return x
```

```python
import functools
import math

import numpy as np
import jax
import jax.numpy as jnp
from jax import lax
from jax.experimental import pallas as pl
from jax.experimental.pallas import tpu as pltpu

F32 = jnp.float32
BF16 = jnp.bfloat16

GRID_W = 64
DA_HEADS = 8
N_EXPERTS = 16
N_GROUPS = 4
EXPERTS_PER_GROUP = N_EXPERTS // N_GROUPS
ROPE_BASE = 10000.0
LN_EPS = 1e-5
CHUNK = 128
GMLP_GROUPS = 16
LANES = 128

VMEM_LIMIT = 56 * 1024 * 1024


def _cparams(sem):
    return pltpu.CompilerParams(dimension_semantics=sem, vmem_limit_bytes=VMEM_LIMIT)


def _layer_norm(r, g, b):
    mu = jnp.mean(r, axis=-1, keepdims=True)
    d = r - mu
    var = jnp.mean(d * d, axis=-1, keepdims=True)
    return d * lax.rsqrt(var + LN_EPS) * g + b


def _mod_kernel(c_ref, w_ref, b_ref, o_ref):
    c = c_ref[...]
    s = (c * jax.nn.sigmoid(c)).astype(BF16)
    o_ref[0] = jnp.dot(s, w_ref[0].astype(BF16), preferred_element_type=F32) + b_ref[0]


def _mod_call(cond, w_mod, b_mod):
    depth, d, n = w_mod.shape
    rows = cond.shape[0]
    tn = 1024
    return pl.pallas_call(
        _mod_kernel,
        grid=(depth, n // tn),
        in_specs=[
            pl.BlockSpec((rows, d), lambda l, j: (0, 0)),
            pl.BlockSpec((1, d, tn), lambda l, j: (l, 0, j)),
            pl.BlockSpec((1, 1, tn), lambda l, j: (l, 0, j)),
        ],
        out_specs=pl.BlockSpec((1, rows, tn), lambda l, j: (l, 0, j)),
        out_shape=jax.ShapeDtypeStruct((depth, rows, n), F32),
        compiler_params=_cparams(("arbitrary", "arbitrary")),
        name="adaln_mod",
    )(cond, w_mod, b_mod.reshape(depth, 1, n))


def _modulate_to_scratch(x_ref, sh_ref, sc_ref, h_ref):
    @pl.when(pl.program_id(2) == 0)
    def _():
        h_ref[...] = (x_ref[0] * (1.0 + sc_ref[0]) + sh_ref[0]).astype(BF16)


def _proj_rope_kernel(x_ref, sh_ref, sc_ref, w_ref, cos_ref, sin_ref, o_ref, h_ref, *, q_tile, k_tile, q_scale):
    _modulate_to_scratch(x_ref, sh_ref, sc_ref, h_ref)
    j = pl.program_id(2)
    acc = jnp.dot(h_ref[...], w_ref[...], preferred_element_type=F32)
    is_rope = jnp.logical_or(j == q_tile, j == k_tile)

    @pl.when(is_rope)
    def _():
        scale = jnp.where(j == q_tile, q_scale, 1.0).astype(F32)
        cs = cos_ref[...] * scale
        sn = sin_ref[...] * scale
        for h in range(acc.shape[1] // LANES):
            t = acc[:, h * LANES:(h + 1) * LANES]
            r = t * cs + pltpu.roll(t, LANES // 2, axis=1) * sn
            o_ref[0, :, h * LANES:(h + 1) * LANES] = r.astype(o_ref.dtype)

    @pl.when(jnp.logical_not(is_rope))
    def _():
        o_ref[0] = acc.astype(o_ref.dtype)


def _proj_plain_kernel(x_ref, sh_ref, sc_ref, w_ref, o_ref, h_ref, *, gelu):
    _modulate_to_scratch(x_ref, sh_ref, sc_ref, h_ref)
    acc = jnp.dot(h_ref[...], w_ref[...], preferred_element_type=F32)
    if gelu:
        acc = jax.nn.gelu(acc, approximate=True)
    o_ref[0] = acc.astype(o_ref.dtype)


def _proj_call(x, shift, scale, w, *, tm, tn, rope=None, gelu=False):
    bsz, s, d = x.shape
    n = w.shape[1]
    tm = min(tm, s)
    per_batch = shift.shape[0] > 1
    mod_map = (lambda b, i, j: (b, 0, 0)) if per_batch else (lambda b, i, j: (0, 0, 0))
    in_specs = [
        pl.BlockSpec((1, tm, d), lambda b, i, j: (b, i, 0)),
        pl.BlockSpec((1, 1, d), mod_map),
        pl.BlockSpec((1, 1, d), mod_map),
        pl.BlockSpec((d, tn), lambda b, i, j: (0, j)),
    ]
    args = [x, shift, scale, w]
    if rope is not None:
        cos_t, sin_t, q_tile, k_tile, q_scale = rope
        in_specs += [pl.BlockSpec((tm, LANES), lambda b, i, j: (i, 0))] * 2
        args += [cos_t, sin_t]
        body = functools.partial(_proj_rope_kernel, q_tile=q_tile, k_tile=k_tile, q_scale=q_scale)
    else:
        body = functools.partial(_proj_plain_kernel, gelu=gelu)
    return pl.pallas_call(
        body,
        grid=(bsz, s // tm, n // tn),
        in_specs=in_specs,
        out_specs=pl.BlockSpec((1, tm, tn), lambda b, i, j: (b, i, j)),
        out_shape=jax.ShapeDtypeStruct((bsz, s, n), BF16),
        scratch_shapes=[pltpu.VMEM((tm, d), BF16)],
        compiler_params=_cparams(("arbitrary", "arbitrary", "arbitrary")),
        name="mod_proj",
    )(*args)


def _attn_kernel(lam_ref, g_ref, q_ref, kc_ref, vc_ref, k_ref, v_ref, o_ref, m_ref, l_ref, acc_ref,
                 *, tk, lam_init):
    tq = q_ref.shape[1]
    q = q_ref[0]
    lane = lax.broadcasted_iota(jnp.int32, q.shape, 1)
    first_map = (lane % (LANES // 2)) < (LANES // 4)
    zero = jnp.zeros_like(q)
    qs = jnp.concatenate([jnp.where(first_map, q, zero), jnp.where(first_map, zero, q)], axis=0)

    m_ref[...] = jnp.full(m_ref.shape, -jnp.inf, F32)
    l_ref[...] = jnp.zeros(l_ref.shape, F32)
    acc_ref[...] = jnp.zeros(acc_ref.shape, F32)

    def step(kb, vb):
        s = lax.dot_general(qs, kb, (((1,), (1,)), ((), ())), preferred_element_type=F32)
        m_old = m_ref[...]
        m_new = jnp.maximum(m_old, jnp.max(s, axis=-1, keepdims=True))
        alpha = jnp.exp(m_old - m_new)
        p = jnp.exp(s - m_new)
        l_ref[...] = alpha * l_ref[...] + jnp.sum(p, axis=-1, keepdims=True)
        acc_ref[...] = alpha * acc_ref[...] + jnp.dot(p.astype(BF16), vb, preferred_element_type=F32)
        m_ref[...] = m_new

    step(kc_ref[0], vc_ref[0])

    def body(i, carry):
        off = pl.multiple_of(i * tk, tk)
        step(k_ref[0, pl.ds(off, tk), :], v_ref[0, pl.ds(off, tk), :])
        return carry

    lax.fori_loop(0, k_ref.shape[1] // tk, body, 0)

    lp = lam_ref[...]
    lam = (jnp.exp(jnp.sum(lp[0:1] * lp[1:2], axis=-1, keepdims=True))
           - jnp.exp(jnp.sum(lp[2:3] * lp[3:4], axis=-1, keepdims=True)) + lam_init)
    o = acc_ref[...] / l_ref[...]
    o = o[:tq] - lam * o[tq:]
    o = o * lax.rsqrt(jnp.mean(o * o, axis=-1, keepdims=True) + LN_EPS)
    o_ref[0] = (o * g_ref[...] * (1.0 - lam_init)).astype(o_ref.dtype)


def _attn_call(proj, ctx_kv, lam_params, subln_g, *, q_col, k_col, v_col, lam_init, tq, tk):
    bsz, s, _ = proj.shape
    c_len = ctx_kv.shape[1]
    h = DA_HEADS
    qb, kb, vb = q_col // LANES, k_col // LANES, v_col // LANES
    return pl.pallas_call(
        functools.partial(_attn_kernel, tk=tk, lam_init=lam_init),
        grid=(bsz, h, s // tq),
        in_specs=[
            pl.BlockSpec((8, LANES), lambda b, hh, i: (0, 0)),
            pl.BlockSpec((1, LANES), lambda b, hh, i: (0, 0)),
            pl.BlockSpec((1, tq, LANES), lambda b, hh, i: (b, i, qb + hh)),
            pl.BlockSpec((1, c_len, LANES), lambda b, hh, i: (b, 0, hh)),
            pl.BlockSpec((1, c_len, LANES), lambda b, hh, i: (b, 0, h + hh)),
            pl.BlockSpec((1, s, LANES), lambda b, hh, i: (b, 0, kb + hh)),
            pl.BlockSpec((1, s, LANES), lambda b, hh, i: (b, 0, vb + hh)),
        ],
        out_specs=pl.BlockSpec((1, tq, LANES), lambda b, hh, i: (b, i, hh)),
        out_shape=jax.ShapeDtypeStruct((bsz, s, h * LANES), BF16),
        scratch_shapes=[
            pltpu.VMEM((2 * tq, 1), F32),
            pltpu.VMEM((2 * tq, 1), F32),
            pltpu.VMEM((2 * tq, LANES), F32),
        ],
        compiler_params=_cparams(("arbitrary", "arbitrary", "arbitrary")),
        name="diff_attn",
    )(lam_params, subln_g, proj, ctx_kv, ctx_kv, proj, proj)


def _conv_kernel(xa_ref, bg_ref, cg_ref, xap_ref, cgp_ref, xan_ref, cgn_ref, w_ref, o_ref):
    i = pl.program_id(1)
    tm = xa_ref.shape[1]
    halo = xap_ref.shape[1]
    z = xa_ref[0].astype(F32) * cg_ref[0].astype(F32)
    z_before = (xap_ref[0].astype(F32) * cgp_ref[0].astype(F32))[halo - 1:halo]
    z_after = (xan_ref[0].astype(F32) * cgn_ref[0].astype(F32))[0:1]
    z_before = jnp.where(i == 0, 0.0, z_before)
    z_after = jnp.where(i == pl.num_programs(1) - 1, 0.0, z_after)
    row = lax.broadcasted_iota(jnp.int32, z.shape, 0)
    z_prev = jnp.where(row == 0, z_before, pltpu.roll(z, 1, axis=0))
    z_next = jnp.where(row == tm - 1, z_after, pltpu.roll(z, tm - 1, axis=0))
    w = w_ref[...]
    conv = w[0:1] * z_prev + w[1:2] * z + w[2:3] * z_next
    o_ref[0] = (bg_ref[0].astype(F32) * conv).astype(o_ref.dtype)


def _conv_call(proj, conv_w, *, width, tm):
    bsz, s, _ = proj.shape
    halo = 16
    nh = tm // halo
    last = s // halo - 1
    main = lambda c: pl.BlockSpec((1, tm, width), lambda b, i: (b, i, c))
    prev = lambda c: pl.BlockSpec((1, halo, width), lambda b, i: (b, jnp.maximum(i * nh - 1, 0), c))
    nxt = lambda c: pl.BlockSpec((1, halo, width), lambda b, i: (b, jnp.minimum((i + 1) * nh, last), c))
    w_pad = jnp.zeros((8, width), F32).at[:conv_w.shape[0]].set(conv_w)
    return pl.pallas_call(
        _conv_kernel,
        grid=(bsz, s // tm),
        in_specs=[main(0), main(1), main(2), prev(0), prev(2), nxt(0), nxt(2),
                  pl.BlockSpec((8, width), lambda b, i: (0, 0))],
        out_specs=pl.BlockSpec((1, tm, width), lambda b, i: (b, i, 0)),
        out_shape=jax.ShapeDtypeStruct((bsz, s, width), BF16),
        compiler_params=_cparams(("arbitrary", "arbitrary")),
        name="gated_conv",
    )(proj, proj, proj, proj, proj, proj, proj, w_pad)


def _spatial_kernel(u_ref, v_ref, g_ref, b_ref, ws_ref, bs_ref, o_ref):
    tm = u_ref.shape[1]
    v = _layer_norm(v_ref[0].astype(F32), g_ref[...], b_ref[...]).astype(BF16)
    bs = bs_ref[...]
    for n in range(tm // CHUNK):
        rows = slice(n * CHUNK, (n + 1) * CHUNK)
        for g in range(GMLP_GROUPS):
            cols = slice(g * LANES, (g + 1) * LANES)
            sg = jnp.dot(ws_ref[g], v[rows, cols], preferred_element_type=F32) + bs[:, g:g + 1]
            o_ref[0, rows, cols] = (u_ref[0, rows, cols].astype(F32) * sg).astype(o_ref.dtype)


def _spatial_call(uv, v_g, v_b, w_s, b_s_t, *, tm):
    bsz, s, two_d = uv.shape
    d = two_d // 2
    return pl.pallas_call(
        _spatial_kernel,
        grid=(bsz, s // tm),
        in_specs=[
            pl.BlockSpec((1, tm, d), lambda b, i: (b, i, 0)),
            pl.BlockSpec((1, tm, d), lambda b, i: (b, i, 1)),
            pl.BlockSpec((1, d), lambda b, i: (0, 0)),
            pl.BlockSpec((1, d), lambda b, i: (0, 0)),
            pl.BlockSpec(w_s.shape, lambda b, i: (0, 0, 0)),
            pl.BlockSpec(b_s_t.shape, lambda b, i: (0, 0)),
        ],
        out_specs=pl.BlockSpec((1, tm, d), lambda b, i: (b, i, 0)),
        out_shape=jax.ShapeDtypeStruct((bsz, s, d), BF16),
        compiler_params=_cparams(("arbitrary", "arbitrary")),
        name="gmlp_spatial",
    )(uv, uv, v_g, v_b, w_s, b_s_t)


def _out_ln_kernel(a1_ref, a2_ref, w_ref, x_ref, gate_ref, g_ref, b_ref, o_ref, *, alpha):
    a = jnp.concatenate([a1_ref[0], a2_ref[0]], axis=1)
    y = jnp.dot(a, w_ref[...], preferred_element_type=F32)
    r = alpha * x_ref[0] + gate_ref[0] * y
    o_ref[0] = _layer_norm(r, g_ref[...], b_ref[...])


def _out_ln_call(a1, a2, col1, col2, w, x, gate, ln_g, ln_b, *, alpha, tm):
    bsz, s, d = x.shape
    half = w.shape[0] // 2
    return pl.pallas_call(
        functools.partial(_out_ln_kernel, alpha=alpha),
        grid=(bsz, s // tm),
        in_specs=[
            pl.BlockSpec((1, tm, half), lambda b, i: (b, i, col1)),
            pl.BlockSpec((1, tm, half), lambda b, i: (b, i, col2)),
            pl.BlockSpec(w.shape, lambda b, i: (0, 0)),
            pl.BlockSpec((1, tm, d), lambda b, i: (b, i, 0)),
            pl.BlockSpec((1, 1, d), lambda b, i: (b, 0, 0)),
            pl.BlockSpec((1, d), lambda b, i: (0, 0)),
            pl.BlockSpec((1, d), lambda b, i: (0, 0)),
        ],
        out_specs=pl.BlockSpec((1, tm, d), lambda b, i: (b, i, 0)),
        out_shape=jax.ShapeDtypeStruct((bsz, s, d), F32),
        compiler_params=_cparams(("arbitrary", "arbitrary")),
        name="out_proj_ln",
    )(a1, a2, w, x, gate, ln_g, ln_b)


def _route(scores, bias):
    sel = scores + bias
    rows = [sel[e:e + 1, :] for e in range(N_EXPERTS)]
    group_score = []
    for g in range(N_GROUPS):
        r = rows[g * EXPERTS_PER_GROUP:(g + 1) * EXPERTS_PER_GROUP]
        best = None
        for a in range(EXPERTS_PER_GROUP):
            for b in range(a + 1, EXPERTS_PER_GROUP):
                pair = r[a] + r[b]
                best = pair if best is None else jnp.maximum(best, pair)
        group_score.append(best)
    one = jnp.ones_like(rows[0])
    zero = jnp.zeros_like(rows[0])
    picked = []
    for g in range(N_GROUPS):
        beaten = zero
        for o in range(N_GROUPS):
            if o < g:
                beaten = jnp.where(group_score[o] >= group_score[g], one, beaten)
            elif o > g:
                beaten = jnp.where(group_score[o] > group_score[g], one, beaten)
        for a in range(EXPERTS_PER_GROUP):
            e = g * EXPERTS_PER_GROUP + a
            rank = zero
            for b in range(EXPERTS_PER_GROUP):
                o = g * EXPERTS_PER_GROUP + b
                if b < a:
                    rank = rank + jnp.where(rows[o] >= rows[e], one, zero)
                elif b > a:
                    rank = rank + jnp.where(rows[o] > rows[e], one, zero)
            picked.append(jnp.where(rank < 2.0, one, zero) * (1.0 - beaten))
    w = [picked[e] * scores[e:e + 1, :] for e in range(N_EXPERTS)]
    total = w[0]
    for e in range(1, N_EXPERTS):
        total = total + w[e]
    return jnp.concatenate(w, axis=0) / total


def _moe_dense_kernel(x_ref, sh_ref, sc_ref, gate_ref, wr_ref, rb_ref, wg_ref, wu_ref, wd_ref, g_ref, b_ref,
                      o_ref, h_ref, comb_ref, acc_ref, *, alpha):
    e = pl.program_id(2)

    @pl.when(e == 0)
    def _():
        h = (x_ref[0] * (1.0 + sc_ref[0]) + sh_ref[0]).astype(BF16)
        h_ref[...] = h
        logits = lax.dot_general(wr_ref[...], h, (((1,), (1,)), ((), ())), preferred_element_type=F32)
        comb = _route(jax.nn.sigmoid(logits), rb_ref[...])
        pad = jnp.zeros((LANES - N_EXPERTS, comb.shape[1]), F32)
        comb_ref[...] = jnp.transpose(jnp.concatenate([comb, pad], axis=0))
        acc_ref[...] = jnp.zeros(acc_ref.shape, F32)

    h = h_ref[...]
    hid = jax.nn.silu(jnp.dot(h, wg_ref[0], preferred_element_type=F32)) * jnp.dot(
        h, wu_ref[0], preferred_element_type=F32)
    lane = lax.broadcasted_iota(jnp.int32, comb_ref.shape, 1)
    w_e = jnp.sum(jnp.where(lane == e, comb_ref[...], 0.0), axis=-1, keepdims=True)
    acc_ref[...] += w_e * jnp.dot(hid.astype(BF16), wd_ref[0], preferred_element_type=F32)

    @pl.when(e == pl.num_programs(2) - 1)
    def _():
        r = alpha * x_ref[0] + gate_ref[0] * acc_ref[...]
        o_ref[0] = _layer_norm(r, g_ref[...], b_ref[...])


def _moe_dense_call(x, shift, scale, gate, w_router_t, router_bias, w_gate, w_up, w_down, ln_g, ln_b, *, alpha, tm):
    bsz, s, d = x.shape
    n_e, _, f = w_gate.shape
    vec = lambda: pl.BlockSpec((1, 1, d), lambda b, i, e: (b, 0, 0))
    return pl.pallas_call(
        functools.partial(_moe_dense_kernel, alpha=alpha),
        grid=(bsz, s // tm, n_e),
        in_specs=[
            pl.BlockSpec((1, tm, d), lambda b, i, e: (b, i, 0)),
            vec(), vec(), vec(),
            pl.BlockSpec((n_e, d), lambda b, i, e: (0, 0)),
            pl.BlockSpec((n_e, 1), lambda b, i, e: (0, 0)),
            pl.BlockSpec((1, d, f), lambda b, i, e: (e, 0, 0)),
            pl.BlockSpec((1, d, f), lambda b, i, e: (e, 0, 0)),
            pl.BlockSpec((1, f, d), lambda b, i, e: (e, 0, 0)),
            pl.BlockSpec((1, d), lambda b, i, e: (0, 0)),
            pl.BlockSpec((1, d), lambda b, i, e: (0, 0)),
        ],
        out_specs=pl.BlockSpec((1, tm, d), lambda b, i, e: (b, i, 0)),
        out_shape=jax.ShapeDtypeStruct((bsz, s, d), F32),
        scratch_shapes=[
            pltpu.VMEM((tm, d), BF16),
            pltpu.VMEM((tm, LANES), F32),
            pltpu.VMEM((tm, d), F32),
        ],
        compiler_params=_cparams(("arbitrary", "arbitrary", "arbitrary")),
        name="moe_dense",
    )(x, shift, scale, gate, w_router_t, router_bias, w_gate, w_up, w_down, ln_g, ln_b)


def _rope_tables(n):
    freqs = LANES // 8
    inv = jnp.power(ROPE_BASE, -jnp.arange(freqs, dtype=F32) / freqs)
    tok = jnp.arange(n)
    ang_r = (tok // GRID_W).astype(F32)[:, None] * inv
    ang_c = (tok % GRID_W).astype(F32)[:, None] * inv
    ang = jnp.concatenate([ang_r, ang_c], axis=1)
    cos = jnp.tile(jnp.cos(ang), (1, 4))
    sin = jnp.tile(jnp.sin(ang), (1, 4))
    sign = jnp.where(jnp.arange(LANES) < LANES // 2, -1.0, 1.0).astype(F32)
    return cos, sin * sign


def _rope_column_perm():
    perm = np.zeros((DA_HEADS, 2, 2, 2, 16), np.int32)
    for h in range(DA_HEADS):
        for p in range(2):
            for m in range(2):
                for ax in range(2):
                    for f in range(16):
                        perm[h, p, m, ax, f] = h * LANES + m * 64 + ax * 32 + p * 16 + f
    return perm.reshape(-1)


def kernel(x, c, ctx, c_ctx, w_mod, b_mod, ln_g, ln_b, w_in_e, conv_w, lambda_q1, lambda_k1, lambda_q2, lambda_k2,
           subln_g, w_out_e, w_in_o, v_ln_g, v_ln_b, w_spatial, b_spatial, w_out_o, w_router, router_bias,
           w_gate, w_up, w_down):
    bsz, s, d = x.shape
    depth = w_mod.shape[0]
    assert depth == 2 and w_in_e.shape[0] == 1 and w_in_o.shape[0] == 1, "two-layer (even, odd) stack only"
    conv_dim = conv_w.shape[-1]
    q_dim = DA_HEADS * LANES
    q_col, k_col, v_col = 3 * conv_dim, 3 * conv_dim + q_dim, 3 * conv_dim + 2 * q_dim
    assert w_in_e.shape[2] == v_col + q_dim and s % GRID_W == 0
    alpha = float((2 * depth) ** 0.25)

    rows = 8 * ((bsz + 1 + 7) // 8)
    cond = jnp.zeros((rows, d), F32).at[:bsz].set(c).at[bsz].set(c_ctx)
    mods = _mod_call(cond, w_mod, b_mod)

    def mod_vec(l, k, ctx_row=False):
        v = mods[l, bsz:bsz + 1, k * d:(k + 1) * d] if ctx_row else mods[l, :bsz, k * d:(k + 1) * d]
        return v.reshape(-1, 1, d)

    w_router_t = w_router.T.astype(BF16)
    rbias = router_bias.reshape(-1, 1).astype(F32)

    perm = _rope_column_perm()
    w_in = w_in_e[0]
    w_in = jnp.concatenate(
        [w_in[:, :q_col], w_in[:, q_col:k_col][:, perm], w_in[:, k_col:v_col][:, perm], w_in[:, v_col:]],
        axis=1).astype(BF16)
    cos_t, sin_t = _rope_tables(s)
    tn = 1024
    assert q_col % tn == 0 and q_dim == tn
    proj = _proj_call(x, mod_vec(0, 0), mod_vec(0, 1), w_in, tm=512, tn=tn,
                      rope=(cos_t, sin_t, q_col // tn, k_col // tn, float(64 ** -0.5)))
    ctx_kv = _proj_call(ctx, mod_vec(0, 0, True), mod_vec(0, 1, True), w_in[:, k_col:], tm=256, tn=tn)

    lam_init = 0.8 - 0.6 * math.exp(-0.3 * 0)
    lam_params = jnp.zeros((8, LANES), F32)
    for r, p in enumerate((lambda_q1, lambda_k1, lambda_q2, lambda_k2)):
        lam_params = lam_params.at[r, :p.shape[-1]].set(p[0].astype(F32))
    attn = _attn_call(proj, ctx_kv, lam_params, subln_g[0].reshape(1, LANES), q_col=q_col, k_col=k_col,
                      v_col=v_col, lam_init=lam_init, tq=256, tk=512)
    conv = _conv_call(proj, conv_w[0], width=conv_dim, tm=512)
    x = _out_ln_call(conv, attn, 0, 0, w_out_e[0].astype(BF16), x, mod_vec(0, 2), ln_g[0, 0:1], ln_b[0, 0:1],
                     alpha=alpha, tm=512)
    x = _moe_dense_call(x, mod_vec(0, 3), mod_vec(0, 4), mod_vec(0, 5), w_router_t, rbias,
                        w_gate[0].astype(BF16), w_up[0].astype(BF16), w_down[0].astype(BF16),
                        ln_g[0, 1:2], ln_b[0, 1:2], alpha=alpha, tm=512)

    uv = _proj_call(x, mod_vec(1, 0), mod_vec(1, 1), w_in_o[0].astype(BF16), tm=512, tn=tn, gelu=True)
    gated = _spatial_call(uv, v_ln_g[0:1], v_ln_b[0:1], w_spatial[0].astype(BF16), b_spatial[0].T, tm=256)
    x = _out_ln_call(gated, gated, 0, 1, w_out_o[0].astype(BF16), x, mod_vec(1, 2), ln_g[1, 0:1], ln_b[1, 0:1],
                     alpha=alpha, tm=512)
    x = _moe_dense_call(x, mod_vec(1, 3), mod_vec(1, 4), mod_vec(1, 5), w_router_t, rbias,
                        w_gate[1].astype(BF16), w_up[1].astype(BF16), w_down[1].astype(BF16),
                        ln_g[1, 1:2], ln_b[1, 1:2], alpha=alpha, tm=512)
    return x
```

```python
import functools
import math

import numpy as np
import jax
import jax.numpy as jnp
from jax import lax
from jax.experimental import pallas as pl
from jax.experimental.pallas import tpu as pltpu

F32 = jnp.float32
BF16 = jnp.bfloat16

GRID_W = 64
DA_HEADS = 8
N_EXPERTS = 16
N_GROUPS = 4
EXPERTS_PER_GROUP = N_EXPERTS // N_GROUPS
ROPE_BASE = 10000.0
LN_EPS = 1e-5
CHUNK = 128
GMLP_GROUPS = 16
LANES = 128
MXU_DEPTH = 256

VMEM_LIMIT = 56 * 1024 * 1024


def _cparams(sem):
    return pltpu.CompilerParams(dimension_semantics=sem, vmem_limit_bytes=VMEM_LIMIT)


def _layer_norm(r, g, b):
    mu = jnp.mean(r, axis=-1, keepdims=True)
    d = r - mu
    var = jnp.mean(d * d, axis=-1, keepdims=True)
    return d * lax.rsqrt(var + LN_EPS) * g + b


def _mod_kernel(c_ref, w_ref, b_ref, o_ref):
    c = c_ref[...]
    s = (c * jax.nn.sigmoid(c)).astype(BF16)
    o_ref[0] = jnp.dot(s, w_ref[0].astype(BF16), preferred_element_type=F32) + b_ref[0]


def _mod_call(cond, w_mod, b_mod):
    depth, d, n = w_mod.shape
    rows = cond.shape[0]
    tn = 1024
    return pl.pallas_call(
        _mod_kernel,
        grid=(depth, n // tn),
        in_specs=[
            pl.BlockSpec((rows, d), lambda l, j: (0, 0)),
            pl.BlockSpec((1, d, tn), lambda l, j: (l, 0, j)),
            pl.BlockSpec((1, 1, tn), lambda l, j: (l, 0, j)),
        ],
        out_specs=pl.BlockSpec((1, rows, tn), lambda l, j: (l, 0, j)),
        out_shape=jax.ShapeDtypeStruct((depth, rows, n), F32),
        compiler_params=_cparams(("arbitrary", "arbitrary")),
        name="adaln_mod",
    )(cond, w_mod, b_mod.reshape(depth, 1, n))


def _modulate_to_scratch(x_ref, sh_ref, sc_ref, h_ref):
    @pl.when(pl.program_id(2) == 0)
    def _():
        h_ref[...] = (x_ref[0] * (1.0 + sc_ref[0]) + sh_ref[0]).astype(BF16)


def _proj_rope_kernel(x_ref, sh_ref, sc_ref, w_ref, cos_ref, sin_ref, o_ref, h_ref, *, q_tile, k_tile, q_scale):
    _modulate_to_scratch(x_ref, sh_ref, sc_ref, h_ref)
    j = pl.program_id(2)
    acc = jnp.dot(h_ref[...], w_ref[...], preferred_element_type=F32)
    is_rope = jnp.logical_or(j == q_tile, j == k_tile)

    @pl.when(is_rope)
    def _():
        scale = jnp.where(j == q_tile, q_scale, 1.0).astype(F32)
        cs = cos_ref[...] * scale
        sn = sin_ref[...] * scale
        for h in range(acc.shape[1] // LANES):
            t = acc[:, h * LANES:(h + 1) * LANES]
            r = t * cs + pltpu.roll(t, LANES // 2, axis=1) * sn
            o_ref[0, :, h * LANES:(h + 1) * LANES] = r.astype(o_ref.dtype)

    @pl.when(jnp.logical_not(is_rope))
    def _():
        o_ref[0] = acc.astype(o_ref.dtype)


def _proj_plain_kernel(x_ref, sh_ref, sc_ref, w_ref, o_ref, h_ref, *, gelu):
    _modulate_to_scratch(x_ref, sh_ref, sc_ref, h_ref)
    acc = jnp.dot(h_ref[...], w_ref[...], preferred_element_type=F32)
    if gelu:
        acc = jax.nn.gelu(acc, approximate=True)
    o_ref[0] = acc.astype(o_ref.dtype)


def _proj_call(x, shift, scale, w, *, tm, tn, rope=None, gelu=False):
    bsz, s, d = x.shape
    n = w.shape[1]
    tm = min(tm, s)
    per_batch = shift.shape[0] > 1
    mod_map = (lambda b, i, j: (b, 0, 0)) if per_batch else (lambda b, i, j: (0, 0, 0))
    in_specs = [
        pl.BlockSpec((1, tm, d), lambda b, i, j: (b, i, 0)),
        pl.BlockSpec((1, 1, d), mod_map),
        pl.BlockSpec((1, 1, d), mod_map),
        pl.BlockSpec((d, tn), lambda b, i, j: (0, j)),
    ]
    args = [x, shift, scale, w]
    if rope is not None:
        cos_t, sin_t, q_tile, k_tile, q_scale = rope
        in_specs += [pl.BlockSpec((tm, LANES), lambda b, i, j: (i, 0))] * 2
        args += [cos_t, sin_t]
        body = functools.partial(_proj_rope_kernel, q_tile=q_tile, k_tile=k_tile, q_scale=q_scale)
    else:
        body = functools.partial(_proj_plain_kernel, gelu=gelu)
    return pl.pallas_call(
        body,
        grid=(bsz, s // tm, n // tn),
        in_specs=in_specs,
        out_specs=pl.BlockSpec((1, tm, tn), lambda b, i, j: (b, i, j)),
        out_shape=jax.ShapeDtypeStruct((bsz, s, n), BF16),
        scratch_shapes=[pltpu.VMEM((tm, d), BF16)],
        compiler_params=_cparams(("arbitrary", "arbitrary", "arbitrary")),
        name="mod_proj",
    )(*args)


ONES_ROWS = 16


def _attn_kernel(lam_ref, g_ref, q_ref, kc_ref, vc_ref, k_ref, v_ref, o_ref, kf_ref, vt_ref, s_ref, p_ref, acc_ref,
                 *, tk, lam_init):
    tq = q_ref.shape[1]
    c_len = kc_ref.shape[1]
    s_len = k_ref.shape[1]
    n_chunks = (c_len + s_len) // tk
    vt_chunk = 512

    @pl.when(pl.program_id(2) == 0)
    def _():
        kf_ref[:c_len, :] = kc_ref[0]
        kf_ref[c_len:, :] = k_ref[0]
        vt_ref[LANES:, :] = jnp.ones((ONES_ROWS, vt_ref.shape[1]), BF16)
        vt_ref[:LANES, :c_len] = jnp.transpose(vc_ref[0].astype(F32)).astype(BF16)
        for n in range(s_len // vt_chunk):
            rows = slice(n * vt_chunk, (n + 1) * vt_chunk)
            cols = slice(c_len + n * vt_chunk, c_len + (n + 1) * vt_chunk)
            vt_ref[:LANES, cols] = jnp.transpose(v_ref[0, rows, :].astype(F32)).astype(BF16)

    q_t = jnp.transpose(q_ref[0].astype(F32))
    dim = lax.broadcasted_iota(jnp.int32, q_t.shape, 0)
    first_map = (dim % (LANES // 2)) < (LANES // 4)
    qs_t = jnp.concatenate([jnp.where(first_map, q_t, 0.0), jnp.where(first_map, 0.0, q_t)], axis=1).astype(BF16)

    acc_ref[...] = jnp.zeros(acc_ref.shape, F32)

    def scores(t):
        off = t * tk if isinstance(t, int) else pl.multiple_of(t * tk, tk)
        return jnp.dot(kf_ref[pl.ds(off, tk), :], qs_t, preferred_element_type=F32)

    def softmax(slot, m_old):
        s_t = s_ref[slot]
        m_new = jnp.maximum(m_old, jnp.max(s_t, axis=0, keepdims=True))
        p_ref[slot] = jnp.exp2(s_t - m_new).astype(BF16)
        return m_new, jnp.exp2(m_old - m_new)

    def values(t, slot, alpha):
        off = t * tk if isinstance(t, int) else pl.multiple_of(t * tk, tk)
        pv = jnp.dot(vt_ref[:, pl.ds(off, tk)], p_ref[slot], preferred_element_type=F32)
        acc_ref[...] = alpha * acc_ref[...] + pv

    def tick(t, parity, m, alpha_prev):
        s_ref[parity] = scores(t)
        m, alpha = softmax(1 - parity, m)
        values(t - 2, parity, alpha_prev)
        return m, alpha

    s_ref[0] = scores(0)
    s_ref[1] = scores(1)
    m, alpha = softmax(0, jnp.full((1, 2 * tq), -jnp.inf, F32))

    def pair(jj, carry):
        m, alpha = tick(2 + 2 * jj, 0, *carry)
        return tick(3 + 2 * jj, 1, m, alpha)

    n_full = n_chunks - 2
    m, alpha = lax.fori_loop(0, n_full // 2, pair, (m, alpha))
    if n_full % 2:
        m, alpha = tick(n_chunks - 1, (n_chunks - 1) % 2, m, alpha)
    last = (n_chunks - 1) % 2
    m, alpha_last = softmax(last, m)
    values(n_chunks - 2, 1 - last, alpha)
    values(n_chunks - 1, last, alpha_last)

    lp = lam_ref[...]
    lam = (jnp.exp(jnp.sum(lp[0:1] * lp[1:2], axis=-1, keepdims=True))
           - jnp.exp(jnp.sum(lp[2:3] * lp[3:4], axis=-1, keepdims=True)) + lam_init)
    acc = acc_ref[...]
    o_t = acc[:LANES] / acc[LANES:LANES + 1]
    o_t = o_t[:, :tq] - lam * o_t[:, tq:]
    o_t = o_t * lax.rsqrt(jnp.mean(o_t * o_t, axis=0, keepdims=True) + LN_EPS)
    o_ref[0] = (jnp.transpose(o_t) * g_ref[...] * (1.0 - lam_init)).astype(o_ref.dtype)


def _attn_call(proj, ctx_kv, lam_params, subln_g, *, q_col, k_col, v_col, lam_init, tq):
    bsz, s, _ = proj.shape
    c_len = ctx_kv.shape[1]
    h = DA_HEADS
    tq = min(tq, s)
    tk = max(t for t in (MXU_DEPTH, 2 * MXU_DEPTH, 3 * MXU_DEPTH) if (c_len + s) % t == 0)
    assert (c_len + s) // tk >= 3 and s % 512 == 0
    qb, kb, vb = q_col // LANES, k_col // LANES, v_col // LANES
    return pl.pallas_call(
        functools.partial(_attn_kernel, tk=tk, lam_init=lam_init),
        grid=(bsz, h, s // tq),
        in_specs=[
            pl.BlockSpec((8, LANES), lambda b, hh, i: (0, 0)),
            pl.BlockSpec((1, LANES), lambda b, hh, i: (0, 0)),
            pl.BlockSpec((1, tq, LANES), lambda b, hh, i: (b, i, qb + hh)),
            pl.BlockSpec((1, c_len, LANES), lambda b, hh, i: (b, 0, hh)),
            pl.BlockSpec((1, c_len, LANES), lambda b, hh, i: (b, 0, h + hh)),
            pl.BlockSpec((1, s, LANES), lambda b, hh, i: (b, 0, kb + hh)),
            pl.BlockSpec((1, s, LANES), lambda b, hh, i: (b, 0, vb + hh)),
        ],
        out_specs=pl.BlockSpec((1, tq, LANES), lambda b, hh, i: (b, i, hh)),
        out_shape=jax.ShapeDtypeStruct((bsz, s, h * LANES), BF16),
        scratch_shapes=[
            pltpu.VMEM((c_len + s, LANES), BF16),
            pltpu.VMEM((LANES + ONES_ROWS, c_len + s), BF16),
            pltpu.VMEM((2, tk, 2 * tq), F32),
            pltpu.VMEM((2, tk, 2 * tq), BF16),
            pltpu.VMEM((LANES + ONES_ROWS, 2 * tq), F32),
        ],
        compiler_params=_cparams(("arbitrary", "arbitrary", "arbitrary")),
        name="diff_attn",
    )(lam_params, subln_g, proj, ctx_kv, ctx_kv, proj, proj)


def _conv_kernel(xa_ref, bg_ref, cg_ref, xap_ref, cgp_ref, xan_ref, cgn_ref, w_ref, o_ref):
    i = pl.program_id(1)
    tm = xa_ref.shape[1]
    halo = xap_ref.shape[1]
    z = xa_ref[0].astype(F32) * cg_ref[0].astype(F32)
    z_before = (xap_ref[0].astype(F32) * cgp_ref[0].astype(F32))[halo - 1:halo]
    z_after = (xan_ref[0].astype(F32) * cgn_ref[0].astype(F32))[0:1]
    z_before = jnp.where(i == 0, 0.0, z_before)
    z_after = jnp.where(i == pl.num_programs(1) - 1, 0.0, z_after)
    row = lax.broadcasted_iota(jnp.int32, z.shape, 0)
    z_prev = jnp.where(row == 0, z_before, pltpu.roll(z, 1, axis=0))
    z_next = jnp.where(row == tm - 1, z_after, pltpu.roll(z, tm - 1, axis=0))
    w = w_ref[...]
    conv = w[0:1] * z_prev + w[1:2] * z + w[2:3] * z_next
    o_ref[0] = (bg_ref[0].astype(F32) * conv).astype(o_ref.dtype)


def _conv_call(proj, conv_w, *, width, tm):
    bsz, s, _ = proj.shape
    halo = 16
    nh = tm // halo
    last = s // halo - 1
    main = lambda c: pl.BlockSpec((1, tm, width), lambda b, i: (b, i, c))
    prev = lambda c: pl.BlockSpec((1, halo, width), lambda b, i: (b, jnp.maximum(i * nh - 1, 0), c))
    nxt = lambda c: pl.BlockSpec((1, halo, width), lambda b, i: (b, jnp.minimum((i + 1) * nh, last), c))
    w_pad = jnp.zeros((8, width), F32).at[:conv_w.shape[0]].set(conv_w)
    return pl.pallas_call(
        _conv_kernel,
        grid=(bsz, s // tm),
        in_specs=[main(0), main(1), main(2), prev(0), prev(2), nxt(0), nxt(2),
                  pl.BlockSpec((8, width), lambda b, i: (0, 0))],
        out_specs=pl.BlockSpec((1, tm, width), lambda b, i: (b, i, 0)),
        out_shape=jax.ShapeDtypeStruct((bsz, s, width), BF16),
        compiler_params=_cparams(("arbitrary", "arbitrary")),
        name="gated_conv",
    )(proj, proj, proj, proj, proj, proj, proj, w_pad)


def _spatial_kernel(u_ref, v_ref, g_ref, b_ref, ws_ref, bs_ref, o_ref):
    tm = u_ref.shape[1]
    v = _layer_norm(v_ref[0].astype(F32), g_ref[...], b_ref[...]).astype(BF16)
    bs = bs_ref[...]
    for n in range(tm // CHUNK):
        rows = slice(n * CHUNK, (n + 1) * CHUNK)
        for g in range(GMLP_GROUPS):
            cols = slice(g * LANES, (g + 1) * LANES)
            sg = jnp.dot(ws_ref[g], v[rows, cols], preferred_element_type=F32) + bs[:, g:g + 1]
            o_ref[0, rows, cols] = (u_ref[0, rows, cols].astype(F32) * sg).astype(o_ref.dtype)


def _spatial_call(uv, v_g, v_b, w_s, b_s_t, *, tm):
    bsz, s, two_d = uv.shape
    d = two_d // 2
    return pl.pallas_call(
        _spatial_kernel,
        grid=(bsz, s // tm),
        in_specs=[
            pl.BlockSpec((1, tm, d), lambda b, i: (b, i, 0)),
            pl.BlockSpec((1, tm, d), lambda b, i: (b, i, 1)),
            pl.BlockSpec((1, d), lambda b, i: (0, 0)),
            pl.BlockSpec((1, d), lambda b, i: (0, 0)),
            pl.BlockSpec(w_s.shape, lambda b, i: (0, 0, 0)),
            pl.BlockSpec(b_s_t.shape, lambda b, i: (0, 0)),
        ],
        out_specs=pl.BlockSpec((1, tm, d), lambda b, i: (b, i, 0)),
        out_shape=jax.ShapeDtypeStruct((bsz, s, d), BF16),
        compiler_params=_cparams(("arbitrary", "arbitrary")),
        name="gmlp_spatial",
    )(uv, uv, v_g, v_b, w_s, b_s_t)


def _out_ln_kernel(a1_ref, a2_ref, w_ref, x_ref, gate_ref, g_ref, b_ref, o_ref, *, alpha):
    a = jnp.concatenate([a1_ref[0], a2_ref[0]], axis=1)
    y = jnp.dot(a, w_ref[...], preferred_element_type=F32)
    r = alpha * x_ref[0] + gate_ref[0] * y
    o_ref[0] = _layer_norm(r, g_ref[...], b_ref[...])


def _out_ln_call(a1, a2, col1, col2, w, x, gate, ln_g, ln_b, *, alpha, tm):
    bsz, s, d = x.shape
    half = w.shape[0] // 2
    return pl.pallas_call(
        functools.partial(_out_ln_kernel, alpha=alpha),
        grid=(bsz, s // tm),
        in_specs=[
            pl.BlockSpec((1, tm, half), lambda b, i: (b, i, col1)),
            pl.BlockSpec((1, tm, half), lambda b, i: (b, i, col2)),
            pl.BlockSpec(w.shape, lambda b, i: (0, 0)),
            pl.BlockSpec((1, tm, d), lambda b, i: (b, i, 0)),
            pl.BlockSpec((1, 1, d), lambda b, i: (b, 0, 0)),
            pl.BlockSpec((1, d), lambda b, i: (0, 0)),
            pl.BlockSpec((1, d), lambda b, i: (0, 0)),
        ],
        out_specs=pl.BlockSpec((1, tm, d), lambda b, i: (b, i, 0)),
        out_shape=jax.ShapeDtypeStruct((bsz, s, d), F32),
        compiler_params=_cparams(("arbitrary", "arbitrary")),
        name="out_proj_ln",
    )(a1, a2, w, x, gate, ln_g, ln_b)


def _route(scores, bias):
    sel = scores + bias
    rows = [sel[e:e + 1, :] for e in range(N_EXPERTS)]
    group_score = []
    for g in range(N_GROUPS):
        r = rows[g * EXPERTS_PER_GROUP:(g + 1) * EXPERTS_PER_GROUP]
        best = None
        for a in range(EXPERTS_PER_GROUP):
            for b in range(a + 1, EXPERTS_PER_GROUP):
                pair = r[a] + r[b]
                best = pair if best is None else jnp.maximum(best, pair)
        group_score.append(best)
    one = jnp.ones_like(rows[0])
    zero = jnp.zeros_like(rows[0])
    picked = []
    for g in range(N_GROUPS):
        beaten = zero
        for o in range(N_GROUPS):
            if o < g:
                beaten = jnp.where(group_score[o] >= group_score[g], one, beaten)
            elif o > g:
                beaten = jnp.where(group_score[o] > group_score[g], one, beaten)
        for a in range(EXPERTS_PER_GROUP):
            e = g * EXPERTS_PER_GROUP + a
            rank = zero
            for b in range(EXPERTS_PER_GROUP):
                o = g * EXPERTS_PER_GROUP + b
                if b < a:
                    rank = rank + jnp.where(rows[o] >= rows[e], one, zero)
                elif b > a:
                    rank = rank + jnp.where(rows[o] > rows[e], one, zero)
            picked.append(jnp.where(rank < 2.0, one, zero) * (1.0 - beaten))
    w = [picked[e] * scores[e:e + 1, :] for e in range(N_EXPERTS)]
    total = w[0]
    for e in range(1, N_EXPERTS):
        total = total + w[e]
    return jnp.concatenate(w, axis=0) / total


def _moe_dense_kernel(x_ref, sh_ref, sc_ref, gate_ref, wr_ref, rb_ref, wg_ref, wu_ref, wd_ref, g_ref, b_ref,
                      o_ref, h_ref, comb_ref, acc_ref, *, alpha):
    e = pl.program_id(2)

    @pl.when(e == 0)
    def _():
        h = (x_ref[0] * (1.0 + sc_ref[0]) + sh_ref[0]).astype(BF16)
        h_ref[...] = h
        logits = lax.dot_general(wr_ref[...], h, (((1,), (1,)), ((), ())), preferred_element_type=F32)
        comb = _route(jax.nn.sigmoid(logits), rb_ref[...])
        pad = jnp.zeros((LANES - N_EXPERTS, comb.shape[1]), F32)
        comb_ref[...] = jnp.transpose(jnp.concatenate([comb, pad], axis=0))
        acc_ref[...] = jnp.zeros(acc_ref.shape, F32)

    h = h_ref[...]
    hid = jax.nn.silu(jnp.dot(h, wg_ref[0], preferred_element_type=F32)) * jnp.dot(
        h, wu_ref[0], preferred_element_type=F32)
    lane = lax.broadcasted_iota(jnp.int32, comb_ref.shape, 1)
    w_e = jnp.sum(jnp.where(lane == e, comb_ref[...], 0.0), axis=-1, keepdims=True)
    acc_ref[...] += w_e * jnp.dot(hid.astype(BF16), wd_ref[0], preferred_element_type=F32)

    @pl.when(e == pl.num_programs(2) - 1)
    def _():
        r = alpha * x_ref[0] + gate_ref[0] * acc_ref[...]
        o_ref[0] = _layer_norm(r, g_ref[...], b_ref[...])


def _moe_dense_call(x, shift, scale, gate, w_router_t, router_bias, w_gate, w_up, w_down, ln_g, ln_b, *, alpha, tm):
    bsz, s, d = x.shape
    n_e, _, f = w_gate.shape
    vec = lambda: pl.BlockSpec((1, 1, d), lambda b, i, e: (b, 0, 0))
    return pl.pallas_call(
        functools.partial(_moe_dense_kernel, alpha=alpha),
        grid=(bsz, s // tm, n_e),
        in_specs=[
            pl.BlockSpec((1, tm, d), lambda b, i, e: (b, i, 0)),
            vec(), vec(), vec(),
            pl.BlockSpec((n_e, d), lambda b, i, e: (0, 0)),
            pl.BlockSpec((n_e, 1), lambda b, i, e: (0, 0)),
            pl.BlockSpec((1, d, f), lambda b, i, e: (e, 0, 0)),
            pl.BlockSpec((1, d, f), lambda b, i, e: (e, 0, 0)),
            pl.BlockSpec((1, f, d), lambda b, i, e: (e, 0, 0)),
            pl.BlockSpec((1, d), lambda b, i, e: (0, 0)),
            pl.BlockSpec((1, d), lambda b, i, e: (0, 0)),
        ],
        out_specs=pl.BlockSpec((1, tm, d), lambda b, i, e: (b, i, 0)),
        out_shape=jax.ShapeDtypeStruct((bsz, s, d), F32),
        scratch_shapes=[
            pltpu.VMEM((tm, d), BF16),
            pltpu.VMEM((tm, LANES), F32),
            pltpu.VMEM((tm, d), F32),
        ],
        compiler_params=_cparams(("arbitrary", "arbitrary", "arbitrary")),
        name="moe_dense",
    )(x, shift, scale, gate, w_router_t, router_bias, w_gate, w_up, w_down, ln_g, ln_b)


def _rope_tables(n):
    freqs = LANES // 8
    inv = jnp.power(ROPE_BASE, -jnp.arange(freqs, dtype=F32) / freqs)
    tok = jnp.arange(n)
    ang_r = (tok // GRID_W).astype(F32)[:, None] * inv
    ang_c = (tok % GRID_W).astype(F32)[:, None] * inv
    ang = jnp.concatenate([ang_r, ang_c], axis=1)
    cos = jnp.tile(jnp.cos(ang), (1, 4))
    sin = jnp.tile(jnp.sin(ang), (1, 4))
    sign = jnp.where(jnp.arange(LANES) < LANES // 2, -1.0, 1.0).astype(F32)
    return cos, sin * sign


def _rope_column_perm():
    perm = np.zeros((DA_HEADS, 2, 2, 2, 16), np.int32)
    for h in range(DA_HEADS):
        for p in range(2):
            for m in range(2):
                for ax in range(2):
                    for f in range(16):
                        perm[h, p, m, ax, f] = h * LANES + m * 64 + ax * 32 + p * 16 + f
    return perm.reshape(-1)


def kernel(x, c, ctx, c_ctx, w_mod, b_mod, ln_g, ln_b, w_in_e, conv_w, lambda_q1, lambda_k1, lambda_q2, lambda_k2,
           subln_g, w_out_e, w_in_o, v_ln_g, v_ln_b, w_spatial, b_spatial, w_out_o, w_router, router_bias,
           w_gate, w_up, w_down):
    bsz, s, d = x.shape
    depth = w_mod.shape[0]
    assert depth == 2 and w_in_e.shape[0] == 1 and w_in_o.shape[0] == 1, "two-layer (even, odd) stack only"
    conv_dim = conv_w.shape[-1]
    q_dim = DA_HEADS * LANES
    q_col, k_col, v_col = 3 * conv_dim, 3 * conv_dim + q_dim, 3 * conv_dim + 2 * q_dim
    assert w_in_e.shape[2] == v_col + q_dim and s % GRID_W == 0
    alpha = float((2 * depth) ** 0.25)

    rows = 8 * ((bsz + 1 + 7) // 8)
    cond = jnp.zeros((rows, d), F32).at[:bsz].set(c).at[bsz].set(c_ctx)
    mods = _mod_call(cond, w_mod, b_mod)

    def mod_vec(l, k, ctx_row=False):
        v = mods[l, bsz:bsz + 1, k * d:(k + 1) * d] if ctx_row else mods[l, :bsz, k * d:(k + 1) * d]
        return v.reshape(-1, 1, d)

    w_router_t = w_router.T.astype(BF16)
    rbias = router_bias.reshape(-1, 1).astype(F32)

    perm = _rope_column_perm()
    w_in = w_in_e[0]
    w_in = jnp.concatenate(
        [w_in[:, :q_col], w_in[:, q_col:k_col][:, perm], w_in[:, k_col:v_col][:, perm], w_in[:, v_col:]],
        axis=1).astype(BF16)
    cos_t, sin_t = _rope_tables(s)
    tn = 1024
    assert q_col % tn == 0 and q_dim == tn
    proj = _proj_call(x, mod_vec(0, 0), mod_vec(0, 1), w_in, tm=512, tn=tn,
                      rope=(cos_t, sin_t, q_col // tn, k_col // tn, float(64 ** -0.5 * math.log2(math.e))))
    ctx_kv = _proj_call(ctx, mod_vec(0, 0, True), mod_vec(0, 1, True), w_in[:, k_col:], tm=256, tn=tn)

    lam_init = 0.8 - 0.6 * math.exp(-0.3 * 0)
    lam_params = jnp.zeros((8, LANES), F32)
    for r, p in enumerate((lambda_q1, lambda_k1, lambda_q2, lambda_k2)):
        lam_params = lam_params.at[r, :p.shape[-1]].set(p[0].astype(F32))
    attn = _attn_call(proj, ctx_kv, lam_params, subln_g[0].reshape(1, LANES), q_col=q_col, k_col=k_col,
                      v_col=v_col, lam_init=lam_init, tq=512)
    conv = _conv_call(proj, conv_w[0], width=conv_dim, tm=512)
    x = _out_ln_call(conv, attn, 0, 0, w_out_e[0].astype(BF16), x, mod_vec(0, 2), ln_g[0, 0:1], ln_b[0, 0:1],
                     alpha=alpha, tm=512)
    x = _moe_dense_call(x, mod_vec(0, 3), mod_vec(0, 4), mod_vec(0, 5), w_router_t, rbias,
                        w_gate[0].astype(BF16), w_up[0].astype(BF16), w_down[0].astype(BF16),
                        ln_g[0, 1:2], ln_b[0, 1:2], alpha=alpha, tm=512)

    uv = _proj_call(x, mod_vec(1, 0), mod_vec(1, 1), w_in_o[0].astype(BF16), tm=512, tn=tn, gelu=True)
    gated = _spatial_call(uv, v_ln_g[0:1], v_ln_b[0:1], w_spatial[0].astype(BF16), b_spatial[0].T, tm=256)
    x = _out_ln_call(gated, gated, 0, 1, w_out_o[0].astype(BF16), x, mod_vec(1, 2), ln_g[1, 0:1], ln_b[1, 0:1],
                     alpha=alpha, tm=512)
    x = _moe_dense_call(x, mod_vec(1, 3), mod_vec(1, 4), mod_vec(1, 5), w_router_t, rbias,
                        w_gate[1].astype(BF16), w_up[1].astype(BF16), w_down[1].astype(BF16),
                        ln_g[1, 1:2], ln_b[1, 1:2], alpha=alpha, tm=512)
    return x
```

```python
import functools
import math

import numpy as np
import jax
import jax.numpy as jnp
from jax import lax
from jax.experimental import pallas as pl
from jax.experimental.pallas import tpu as pltpu

F32 = jnp.float32
BF16 = jnp.bfloat16

GRID_W = 64
DA_HEADS = 8
N_EXPERTS = 16
N_GROUPS = 4
EXPERTS_PER_GROUP = N_EXPERTS // N_GROUPS
ROPE_BASE = 10000.0
LN_EPS = 1e-5
CHUNK = 128
GMLP_GROUPS = 16
LANES = 128
MXU_DEPTH = 256

VMEM_LIMIT = 56 * 1024 * 1024
PROJ_TM = 1024
ROW_TM = 512


def _cparams(sem):
    return pltpu.CompilerParams(dimension_semantics=sem, vmem_limit_bytes=VMEM_LIMIT)


def _layer_norm(r, g, b):
    mu = jnp.mean(r, axis=-1, keepdims=True)
    d = r - mu
    var = jnp.mean(d * d, axis=-1, keepdims=True)
    return d * lax.rsqrt(var + LN_EPS) * g + b


def _mod_kernel(c_ref, w_ref, b_ref, o_ref):
    c = c_ref[...]
    s = (c * jax.nn.sigmoid(c)).astype(BF16)
    o_ref[0] = jnp.dot(s, w_ref[0].astype(BF16), preferred_element_type=F32) + b_ref[0]


def _mod_call(cond, w_mod, b_mod):
    depth, d, n = w_mod.shape
    rows = cond.shape[0]
    tn = 1024
    return pl.pallas_call(
        _mod_kernel,
        grid=(depth, n // tn),
        in_specs=[
            pl.BlockSpec((rows, d), lambda l, j: (0, 0)),
            pl.BlockSpec((1, d, tn), lambda l, j: (l, 0, j)),
            pl.BlockSpec((1, 1, tn), lambda l, j: (l, 0, j)),
        ],
        out_specs=pl.BlockSpec((1, rows, tn), lambda l, j: (l, 0, j)),
        out_shape=jax.ShapeDtypeStruct((depth, rows, n), F32),
        compiler_params=_cparams(("arbitrary", "arbitrary")),
        name="adaln_mod",
    )(cond, w_mod, b_mod.reshape(depth, 1, n))


def _proj_kernel(*refs, n_flat, n_head_tiles, rope, gelu):
    x_ref, sh_ref, sc_ref, w_ref = refs[:4]
    pos = 4
    if rope is not None:
        cos_ref, sin_ref = refs[pos:pos + 2]
        pos += 2
    flat_ref = refs[pos] if n_flat else None
    pos += 1 if n_flat else 0
    head_refs = refs[pos:pos + n_head_tiles]
    h_ref = refs[pos + n_head_tiles]
    j = pl.program_id(2)

    @pl.when(j == 0)
    def _():
        h_ref[...] = (x_ref[0] * (1.0 + sc_ref[0]) + sh_ref[0]).astype(BF16)

    acc = jnp.dot(h_ref[...], w_ref[...], preferred_element_type=F32)

    if n_flat:
        @pl.when(j < n_flat)
        def _():
            flat_ref[0] = (jax.nn.gelu(acc, approximate=True) if gelu else acc).astype(flat_ref.dtype)

    for t in range(n_head_tiles):
        tile = n_flat + t
        o_ref = head_refs[t]

        @pl.when(j == tile)
        def _(tile=tile, o_ref=o_ref):
            rotary = rope is not None and tile in rope[0]
            if rotary:
                scale = rope[2] if tile == rope[1] else 1.0
                cs = cos_ref[...] * scale
                sn = sin_ref[...] * scale
            for h in range(acc.shape[1] // LANES):
                piece = acc[:, h * LANES:(h + 1) * LANES]
                if rotary:
                    piece = piece * cs + pltpu.roll(piece, LANES // 2, axis=1) * sn
                o_ref[0, h] = piece.astype(o_ref.dtype)


def _proj_call(x, shift, scale, w, *, tm, tn, n_flat, rope=None, gelu=False):
    bsz, s, d = x.shape
    n_tiles = w.shape[1] // tn
    n_head_tiles = n_tiles - n_flat
    heads = tn // LANES
    tm = min(tm, s)
    per_batch = shift.shape[0] > 1
    mod_map = (lambda b, i, j: (b, 0, 0)) if per_batch else (lambda b, i, j: (0, 0, 0))
    in_specs = [
        pl.BlockSpec((1, tm, d), lambda b, i, j: (b, i, 0)),
        pl.BlockSpec((1, 1, d), mod_map),
        pl.BlockSpec((1, 1, d), mod_map),
        pl.BlockSpec((d, tn), lambda b, i, j: (0, j)),
    ]
    args = [x, shift, scale, w]
    rope_static = None
    if rope is not None:
        cos_t, sin_t, rope_tiles, q_tile, q_scale = rope
        in_specs += [pl.BlockSpec((tm, LANES), lambda b, i, j: (i, 0))] * 2
        args += [cos_t, sin_t]
        rope_static = (tuple(rope_tiles), q_tile, q_scale)
    out_specs, out_shape = [], []
    if n_flat:
        out_specs.append(pl.BlockSpec((1, tm, tn), lambda b, i, j: (b, i, jnp.minimum(j, n_flat - 1))))
        out_shape.append(jax.ShapeDtypeStruct((bsz, s, n_flat * tn), BF16))
    for _ in range(n_head_tiles):
        out_specs.append(pl.BlockSpec((1, heads, tm, LANES), lambda b, i, j: (b, 0, i, 0)))
        out_shape.append(jax.ShapeDtypeStruct((bsz, heads, s, LANES), BF16))
    return pl.pallas_call(
        functools.partial(_proj_kernel, n_flat=n_flat, n_head_tiles=n_head_tiles, rope=rope_static, gelu=gelu),
        grid=(bsz, s // tm, n_tiles),
        in_specs=in_specs,
        out_specs=out_specs,
        out_shape=out_shape,
        scratch_shapes=[pltpu.VMEM((tm, d), BF16)],
        compiler_params=_cparams(("arbitrary", "arbitrary", "arbitrary")),
        name="mod_proj",
    )(*args)


ONES_ROWS = 16


def _attn_kernel(lam_ref, g_ref, q_ref, kc_ref, vc_ref, k_ref, v_ref, o_ref, kf_ref, vt_ref, s_ref, p_ref, acc_ref,
                 *, tk, lam_init):
    tq = q_ref.shape[2]
    c_len = kc_ref.shape[2]
    s_len = k_ref.shape[2]
    n_chunks = (c_len + s_len) // tk
    vt_chunk = 512

    @pl.when(pl.program_id(2) == 0)
    def _():
        kf_ref[:c_len, :] = kc_ref[0, 0]
        kf_ref[c_len:, :] = k_ref[0, 0]
        vt_ref[LANES:, :] = jnp.ones((ONES_ROWS, vt_ref.shape[1]), BF16)
        vt_ref[:LANES, :c_len] = jnp.transpose(vc_ref[0, 0].astype(F32)).astype(BF16)
        for n in range(s_len // vt_chunk):
            rows = slice(n * vt_chunk, (n + 1) * vt_chunk)
            cols = slice(c_len + n * vt_chunk, c_len + (n + 1) * vt_chunk)
            vt_ref[:LANES, cols] = jnp.transpose(v_ref[0, 0, rows, :].astype(F32)).astype(BF16)

    q_t = jnp.transpose(q_ref[0, 0].astype(F32))
    dim = lax.broadcasted_iota(jnp.int32, q_t.shape, 0)
    first_map = (dim % (LANES // 2)) < (LANES // 4)
    qs_t = jnp.concatenate([jnp.where(first_map, q_t, 0.0), jnp.where(first_map, 0.0, q_t)], axis=1).astype(BF16)

    acc_ref[...] = jnp.zeros(acc_ref.shape, F32)

    def scores(t):
        off = t * tk if isinstance(t, int) else pl.multiple_of(t * tk, tk)
        return jnp.dot(kf_ref[pl.ds(off, tk), :], qs_t, preferred_element_type=F32)

    def softmax(slot, m_old):
        s_t = s_ref[slot]
        m_new = jnp.maximum(m_old, jnp.max(s_t, axis=0, keepdims=True))
        p_ref[slot] = jnp.exp2(s_t - m_new).astype(BF16)
        return m_new, jnp.exp2(m_old - m_new)

    def values(t, slot, alpha):
        off = t * tk if isinstance(t, int) else pl.multiple_of(t * tk, tk)
        pv = jnp.dot(vt_ref[:, pl.ds(off, tk)], p_ref[slot], preferred_element_type=F32)
        acc_ref[...] = alpha * acc_ref[...] + pv

    def tick(t, parity, m, alpha_prev):
        s_ref[parity] = scores(t)
        m, alpha = softmax(1 - parity, m)
        values(t - 2, parity, alpha_prev)
        return m, alpha

    s_ref[0] = scores(0)
    s_ref[1] = scores(1)
    m, alpha = softmax(0, jnp.full((1, 2 * tq), -jnp.inf, F32))

    def pair(jj, carry):
        m, alpha = tick(2 + 2 * jj, 0, *carry)
        return tick(3 + 2 * jj, 1, m, alpha)

    n_full = n_chunks - 2
    m, alpha = lax.fori_loop(0, n_full // 2, pair, (m, alpha))
    if n_full % 2:
        m, alpha = tick(n_chunks - 1, (n_chunks - 1) % 2, m, alpha)
    last = (n_chunks - 1) % 2
    m, alpha_last = softmax(last, m)
    values(n_chunks - 2, 1 - last, alpha)
    values(n_chunks - 1, last, alpha_last)

    lp = lam_ref[...]
    lam = (jnp.exp(jnp.sum(lp[0:1] * lp[1:2], axis=-1, keepdims=True))
           - jnp.exp(jnp.sum(lp[2:3] * lp[3:4], axis=-1, keepdims=True)) + lam_init)
    acc = acc_ref[...]
    o_t = acc[:LANES] / acc[LANES:LANES + 1]
    o_t = o_t[:, :tq] - lam * o_t[:, tq:]
    o_t = o_t * lax.rsqrt(jnp.mean(o_t * o_t, axis=0, keepdims=True) + LN_EPS)
    o_ref[0, 0] = (jnp.transpose(o_t) * g_ref[...] * (1.0 - lam_init)).astype(o_ref.dtype)


def _attn_call(q, k, v, kc, vc, lam_params, subln_g, *, lam_init, tq):
    bsz, h, s, _ = q.shape
    c_len = kc.shape[2]
    tq = min(tq, s)
    tk = max(t for t in (MXU_DEPTH, 2 * MXU_DEPTH, 3 * MXU_DEPTH) if (c_len + s) % t == 0)
    assert (c_len + s) // tk >= 3 and s % 512 == 0
    whole = lambda n: pl.BlockSpec((1, 1, n, LANES), lambda b, hh, i: (b, hh, 0, 0))
    return pl.pallas_call(
        functools.partial(_attn_kernel, tk=tk, lam_init=lam_init),
        grid=(bsz, h, s // tq),
        in_specs=[
            pl.BlockSpec((8, LANES), lambda b, hh, i: (0, 0)),
            pl.BlockSpec((1, LANES), lambda b, hh, i: (0, 0)),
            pl.BlockSpec((1, 1, tq, LANES), lambda b, hh, i: (b, hh, i, 0)),
            whole(c_len), whole(c_len), whole(s), whole(s),
        ],
        out_specs=pl.BlockSpec((1, 1, tq, LANES), lambda b, hh, i: (b, hh, i, 0)),
        out_shape=jax.ShapeDtypeStruct((bsz, h, s, LANES), BF16),
        scratch_shapes=[
            pltpu.VMEM((c_len + s, LANES), BF16),
            pltpu.VMEM((LANES + ONES_ROWS, c_len + s), BF16),
            pltpu.VMEM((2, tk, 2 * tq), F32),
            pltpu.VMEM((2, tk, 2 * tq), BF16),
            pltpu.VMEM((LANES + ONES_ROWS, 2 * tq), F32),
        ],
        compiler_params=_cparams(("arbitrary", "arbitrary", "arbitrary")),
        name="diff_attn",
    )(lam_params, subln_g, q, kc, vc, k, v)


def _conv_kernel(xa_ref, bg_ref, cg_ref, xap_ref, cgp_ref, xan_ref, cgn_ref, w_ref, o_ref):
    i = pl.program_id(1)
    tm = xa_ref.shape[1]
    halo = xap_ref.shape[1]
    z = xa_ref[0].astype(F32) * cg_ref[0].astype(F32)
    z_before = (xap_ref[0].astype(F32) * cgp_ref[0].astype(F32))[halo - 1:halo]
    z_after = (xan_ref[0].astype(F32) * cgn_ref[0].astype(F32))[0:1]
    z_before = jnp.where(i == 0, 0.0, z_before)
    z_after = jnp.where(i == pl.num_programs(1) - 1, 0.0, z_after)
    row = lax.broadcasted_iota(jnp.int32, z.shape, 0)
    z_prev = jnp.where(row == 0, z_before, pltpu.roll(z, 1, axis=0))
    z_next = jnp.where(row == tm - 1, z_after, pltpu.roll(z, tm - 1, axis=0))
    w = w_ref[...]
    conv = w[0:1] * z_prev + w[1:2] * z + w[2:3] * z_next
    o_ref[0] = (bg_ref[0].astype(F32) * conv).astype(o_ref.dtype)


def _conv_call(proj, conv_w, *, width, tm):
    bsz, s, _ = proj.shape
    halo = 16
    nh = tm // halo
    last = s // halo - 1
    main = lambda c: pl.BlockSpec((1, tm, width), lambda b, i: (b, i, c))
    prev = lambda c: pl.BlockSpec((1, halo, width), lambda b, i: (b, jnp.maximum(i * nh - 1, 0), c))
    nxt = lambda c: pl.BlockSpec((1, halo, width), lambda b, i: (b, jnp.minimum((i + 1) * nh, last), c))
    w_pad = jnp.zeros((8, width), F32).at[:conv_w.shape[0]].set(conv_w)
    return pl.pallas_call(
        _conv_kernel,
        grid=(bsz, s // tm),
        in_specs=[main(0), main(1), main(2), prev(0), prev(2), nxt(0), nxt(2),
                  pl.BlockSpec((8, width), lambda b, i: (0, 0))],
        out_specs=pl.BlockSpec((1, tm, width), lambda b, i: (b, i, 0)),
        out_shape=jax.ShapeDtypeStruct((bsz, s, width), BF16),
        compiler_params=_cparams(("arbitrary", "arbitrary")),
        name="gated_conv",
    )(proj, proj, proj, proj, proj, proj, proj, w_pad)


def _spatial_kernel(u_ref, v_ref, g_ref, b_ref, ws_ref, bs_ref, o_ref):
    tm = u_ref.shape[1]
    v = _layer_norm(v_ref[0].astype(F32), g_ref[...], b_ref[...]).astype(BF16)
    bs = bs_ref[...]
    for n in range(tm // CHUNK):
        rows = slice(n * CHUNK, (n + 1) * CHUNK)
        for g in range(GMLP_GROUPS):
            cols = slice(g * LANES, (g + 1) * LANES)
            sg = jnp.dot(ws_ref[g], v[rows, cols], preferred_element_type=F32) + bs[:, g:g + 1]
            o_ref[0, rows, cols] = (u_ref[0, rows, cols].astype(F32) * sg).astype(o_ref.dtype)


def _spatial_call(uv, v_g, v_b, w_s, b_s_t, *, tm):
    bsz, s, two_d = uv.shape
    d = two_d // 2
    return pl.pallas_call(
        _spatial_kernel,
        grid=(bsz, s // tm),
        in_specs=[
            pl.BlockSpec((1, tm, d), lambda b, i: (b, i, 0)),
            pl.BlockSpec((1, tm, d), lambda b, i: (b, i, 1)),
            pl.BlockSpec((1, d), lambda b, i: (0, 0)),
            pl.BlockSpec((1, d), lambda b, i: (0, 0)),
            pl.BlockSpec(w_s.shape, lambda b, i: (0, 0, 0)),
            pl.BlockSpec(b_s_t.shape, lambda b, i: (0, 0)),
        ],
        out_specs=pl.BlockSpec((1, tm, d), lambda b, i: (b, i, 0)),
        out_shape=jax.ShapeDtypeStruct((bsz, s, d), BF16),
        compiler_params=_cparams(("arbitrary", "arbitrary")),
        name="gmlp_spatial",
    )(uv, uv, v_g, v_b, w_s, b_s_t)


def _out_ln_kernel(a1_ref, a2_ref, w_ref, x_ref, gate_ref, g_ref, b_ref, o_ref, *, alpha):
    if len(a2_ref.shape) == 4:
        second = [a2_ref[0, h] for h in range(a2_ref.shape[1])]
    else:
        second = [a2_ref[0]]
    a = jnp.concatenate([a1_ref[0]] + second, axis=1)
    y = jnp.dot(a, w_ref[...], preferred_element_type=F32)
    r = alpha * x_ref[0] + gate_ref[0] * y
    o_ref[0] = _layer_norm(r, g_ref[...], b_ref[...])


def _out_ln_call(a1, a2, col1, col2, w, x, gate, ln_g, ln_b, *, alpha, tm):
    bsz, s, d = x.shape
    half = w.shape[0] // 2
    if a2.ndim == 4:
        a2_spec = pl.BlockSpec((1, a2.shape[1], tm, LANES), lambda b, i: (b, 0, i, 0))
    else:
        a2_spec = pl.BlockSpec((1, tm, half), lambda b, i: (b, i, col2))
    return pl.pallas_call(
        functools.partial(_out_ln_kernel, alpha=alpha),
        grid=(bsz, s // tm),
        in_specs=[
            pl.BlockSpec((1, tm, half), lambda b, i: (b, i, col1)),
            a2_spec,
            pl.BlockSpec(w.shape, lambda b, i: (0, 0)),
            pl.BlockSpec((1, tm, d), lambda b, i: (b, i, 0)),
            pl.BlockSpec((1, 1, d), lambda b, i: (b, 0, 0)),
            pl.BlockSpec((1, d), lambda b, i: (0, 0)),
            pl.BlockSpec((1, d), lambda b, i: (0, 0)),
        ],
        out_specs=pl.BlockSpec((1, tm, d), lambda b, i: (b, i, 0)),
        out_shape=jax.ShapeDtypeStruct((bsz, s, d), F32),
        compiler_params=_cparams(("arbitrary", "arbitrary")),
        name="out_proj_ln",
    )(a1, a2, w, x, gate, ln_g, ln_b)


def _route(scores, bias):
    sel = scores + bias
    rows = [sel[e:e + 1, :] for e in range(N_EXPERTS)]
    group_score = []
    for g in range(N_GROUPS):
        r = rows[g * EXPERTS_PER_GROUP:(g + 1) * EXPERTS_PER_GROUP]
        best = None
        for a in range(EXPERTS_PER_GROUP):
            for b in range(a + 1, EXPERTS_PER_GROUP):
                pair = r[a] + r[b]
                best = pair if best is None else jnp.maximum(best, pair)
        group_score.append(best)
    one = jnp.ones_like(rows[0])
    zero = jnp.zeros_like(rows[0])
    picked = []
    chosen = []
    for g in range(N_GROUPS):
        beaten = zero
        for o in range(N_GROUPS):
            if o < g:
                beaten = jnp.where(group_score[o] >= group_score[g], one, beaten)
            elif o > g:
                beaten = jnp.where(group_score[o] > group_score[g], one, beaten)
        chosen.append(1.0 - beaten)
        for a in range(EXPERTS_PER_GROUP):
            e = g * EXPERTS_PER_GROUP + a
            rank = zero
            for b in range(EXPERTS_PER_GROUP):
                o = g * EXPERTS_PER_GROUP + b
                if b < a:
                    rank = rank + jnp.where(rows[o] >= rows[e], one, zero)
                elif b > a:
                    rank = rank + jnp.where(rows[o] > rows[e], one, zero)
            picked.append(jnp.where(rank < 2.0, one, zero) * chosen[g])
    w = [picked[e] * scores[e:e + 1, :] for e in range(N_EXPERTS)]
    total = w[0]
    for e in range(1, N_EXPERTS):
        total = total + w[e]
    return [we / total for we in w], chosen


ROW_BLOCK = 128


def _moe_kernel(x_ref, sh_ref, sc_ref, gate_ref, wr_ref, rb_ref, wg_ref, wu_ref, wd_ref, g_ref, b_ref,
                o_ref, p_ref, xs_ref, cs_ref, acc_ref, seg_ref, *, alpha):
    e = pl.program_id(2)
    tm, d = x_ref.shape[1], x_ref.shape[2]
    rows = p_ref.shape[0]
    col_chunk = 512

    @pl.when(e == 0)
    def _():
        h = (x_ref[0] * (1.0 + sc_ref[0]) + sh_ref[0]).astype(BF16)
        logits = lax.dot_general(wr_ref[...], h, (((1,), (1,)), ((), ())), preferred_element_type=F32)
        comb, chosen = _route(jax.nn.sigmoid(logits), rb_ref[...])
        src = lax.broadcasted_iota(jnp.int32, (tm, tm), 0)
        dst = lax.broadcasted_iota(jnp.int32, (tm, tm), 1)
        before = jnp.where(src < dst, 1.0, 0.0).astype(BF16)
        pad_rows = [jnp.zeros_like(chosen[0])] * (8 - N_GROUPS)
        rank = jnp.dot(jnp.concatenate(chosen + pad_rows, axis=0).astype(BF16), before, preferred_element_type=F32)
        dest = jnp.zeros_like(chosen[0])
        first = jnp.int32(0)
        for g in range(N_GROUPS):
            count = jnp.sum(chosen[g]).astype(jnp.int32)
            n_blocks = (count + (ROW_BLOCK - 1)) // ROW_BLOCK
            seg_ref[g] = first
            seg_ref[N_GROUPS + g] = n_blocks
            dest = dest + chosen[g] * ((first * ROW_BLOCK).astype(F32) + rank[g:g + 1])
            first = first + n_blocks
        row_id = lax.broadcasted_iota(jnp.int32, (rows, tm), 0)
        p = jnp.where(row_id == dest.astype(jnp.int32), 1.0, 0.0).astype(BF16)
        p_ref[...] = p
        for c in range(d // col_chunk):
            cols = slice(c * col_chunk, (c + 1) * col_chunk)
            xs_ref[:, cols] = jnp.dot(p, h[:, cols], preferred_element_type=F32).astype(BF16)
        in_group = []
        for j in range(EXPERTS_PER_GROUP):
            cj = comb[j]
            for g in range(1, N_GROUPS):
                cj = cj + comb[g * EXPERTS_PER_GROUP + j]
            in_group.append(cj)
        c_rows = jnp.concatenate(in_group + [jnp.zeros((LANES - EXPERTS_PER_GROUP, tm), F32)], axis=0)
        c_cols = jnp.transpose(c_rows)
        hi = c_cols.astype(BF16)
        lo = (c_cols - hi.astype(F32)).astype(BF16)
        cs = jnp.dot(p, jnp.concatenate([hi, lo], axis=1), preferred_element_type=F32)
        cs_ref[...] = cs[:, :LANES] + cs[:, LANES:]
        acc_ref[...] = jnp.zeros(acc_ref.shape, F32)

    g = e // EXPERTS_PER_GROUP
    j = e % EXPERTS_PER_GROUP
    first = seg_ref[g]
    lane = lax.broadcasted_iota(jnp.int32, (ROW_BLOCK, LANES), 1)

    def block(b, carry):
        r0 = pl.multiple_of((first + b) * ROW_BLOCK, ROW_BLOCK)
        xb = xs_ref[pl.ds(r0, ROW_BLOCK), :]
        hid = jax.nn.silu(jnp.dot(xb, wg_ref[0], preferred_element_type=F32)) * jnp.dot(
            xb, wu_ref[0], preferred_element_type=F32)
        w_e = jnp.sum(jnp.where(lane == j, cs_ref[pl.ds(r0, ROW_BLOCK), :], 0.0), axis=-1, keepdims=True)
        acc_ref[pl.ds(r0, ROW_BLOCK), :] += jnp.dot((hid * w_e).astype(BF16), wd_ref[0],
                                                    preferred_element_type=F32)
        return carry

    lax.fori_loop(0, seg_ref[N_GROUPS + g], block, 0)

    @pl.when(e == pl.num_programs(2) - 1)
    def _():
        y = lax.dot_general(p_ref[...], acc_ref[...].astype(BF16), (((0,), (0,)), ((), ())),
                            preferred_element_type=F32)
        r = alpha * x_ref[0] + gate_ref[0] * y
        o_ref[0] = _layer_norm(r, g_ref[...], b_ref[...])


def _moe_call(x, shift, scale, gate, w_router_t, router_bias, w_gate, w_up, w_down, ln_g, ln_b, *, alpha, tm):
    bsz, s, d = x.shape
    n_e, _, f = w_gate.shape
    assert n_e == N_EXPERTS
    rows = tm + N_GROUPS * ROW_BLOCK
    vec = lambda: pl.BlockSpec((1, 1, d), lambda b, i, e: (b, 0, 0))
    return pl.pallas_call(
        functools.partial(_moe_kernel, alpha=alpha),
        grid=(bsz, s // tm, n_e),
        in_specs=[
            pl.BlockSpec((1, tm, d), lambda b, i, e: (b, i, 0)),
            vec(), vec(), vec(),
            pl.BlockSpec((n_e, d), lambda b, i, e: (0, 0)),
            pl.BlockSpec((n_e, 1), lambda b, i, e: (0, 0)),
            pl.BlockSpec((1, d, f), lambda b, i, e: (e, 0, 0)),
            pl.BlockSpec((1, d, f), lambda b, i, e: (e, 0, 0)),
            pl.BlockSpec((1, f, d), lambda b, i, e: (e, 0, 0)),
            pl.BlockSpec((1, d), lambda b, i, e: (0, 0)),
            pl.BlockSpec((1, d), lambda b, i, e: (0, 0)),
        ],
        out_specs=pl.BlockSpec((1, tm, d), lambda b, i, e: (b, i, 0)),
        out_shape=jax.ShapeDtypeStruct((bsz, s, d), F32),
        scratch_shapes=[
            pltpu.VMEM((rows, tm), BF16),
            pltpu.VMEM((rows, d), BF16),
            pltpu.VMEM((rows, LANES), F32),
            pltpu.VMEM((rows, d), F32),
            pltpu.SMEM((2 * N_GROUPS,), jnp.int32),
        ],
        compiler_params=_cparams(("arbitrary", "arbitrary", "arbitrary")),
        name="moe_grouped",
    )(x, shift, scale, gate, w_router_t, router_bias, w_gate, w_up, w_down, ln_g, ln_b)


def _rope_tables(n):
    freqs = LANES // 8
    inv = jnp.power(ROPE_BASE, -jnp.arange(freqs, dtype=F32) / freqs)
    tok = jnp.arange(n)
    ang_r = (tok // GRID_W).astype(F32)[:, None] * inv
    ang_c = (tok % GRID_W).astype(F32)[:, None] * inv
    ang = jnp.concatenate([ang_r, ang_c], axis=1)
    cos = jnp.tile(jnp.cos(ang), (1, 4))
    sin = jnp.tile(jnp.sin(ang), (1, 4))
    sign = jnp.where(jnp.arange(LANES) < LANES // 2, -1.0, 1.0).astype(F32)
    return cos, sin * sign


def _rope_column_perm():
    perm = np.zeros((DA_HEADS, 2, 2, 2, 16), np.int32)
    for h in range(DA_HEADS):
        for p in range(2):
            for m in range(2):
                for ax in range(2):
                    for f in range(16):
                        perm[h, p, m, ax, f] = h * LANES + m * 64 + ax * 32 + p * 16 + f
    return perm.reshape(-1)


def kernel(x, c, ctx, c_ctx, w_mod, b_mod, ln_g, ln_b, w_in_e, conv_w, lambda_q1, lambda_k1, lambda_q2, lambda_k2,
           subln_g, w_out_e, w_in_o, v_ln_g, v_ln_b, w_spatial, b_spatial, w_out_o, w_router, router_bias,
           w_gate, w_up, w_down):
    bsz, s, d = x.shape
    depth = w_mod.shape[0]
    assert depth == 2 and w_in_e.shape[0] == 1 and w_in_o.shape[0] == 1, "two-layer (even, odd) stack only"
    conv_dim = conv_w.shape[-1]
    q_dim = DA_HEADS * LANES
    q_col, k_col, v_col = 3 * conv_dim, 3 * conv_dim + q_dim, 3 * conv_dim + 2 * q_dim
    assert w_in_e.shape[2] == v_col + q_dim and s % GRID_W == 0
    alpha = float((2 * depth) ** 0.25)

    rows = 8 * ((bsz + 1 + 7) // 8)
    cond = jnp.zeros((rows, d), F32).at[:bsz].set(c).at[bsz].set(c_ctx)
    mods = _mod_call(cond, w_mod, b_mod)

    def mod_vec(l, k, ctx_row=False):
        v = mods[l, bsz:bsz + 1, k * d:(k + 1) * d] if ctx_row else mods[l, :bsz, k * d:(k + 1) * d]
        return v.reshape(-1, 1, d)

    w_router_t = w_router.T.astype(BF16)
    rbias = router_bias.reshape(-1, 1).astype(F32)

    perm = _rope_column_perm()
    w_in = w_in_e[0]
    w_in = jnp.concatenate(
        [w_in[:, :q_col], w_in[:, q_col:k_col][:, perm], w_in[:, k_col:v_col][:, perm], w_in[:, v_col:]],
        axis=1).astype(BF16)
    cos_t, sin_t = _rope_tables(s)
    tn = q_dim
    assert q_col % tn == 0
    q_tile, k_tile = q_col // tn, k_col // tn
    conv_proj, q, k, v = _proj_call(
        x, mod_vec(0, 0), mod_vec(0, 1), w_in, tm=PROJ_TM, tn=tn, n_flat=q_tile,
        rope=(cos_t, sin_t, (q_tile, k_tile), q_tile, float(64 ** -0.5 * math.log2(math.e))))
    kc, vc = _proj_call(ctx, mod_vec(0, 0, True), mod_vec(0, 1, True), w_in[:, k_col:], tm=PROJ_TM, tn=tn, n_flat=0)

    lam_init = 0.8 - 0.6 * math.exp(-0.3 * 0)
    lam_params = jnp.zeros((8, LANES), F32)
    for r, p in enumerate((lambda_q1, lambda_k1, lambda_q2, lambda_k2)):
        lam_params = lam_params.at[r, :p.shape[-1]].set(p[0].astype(F32))
    attn = _attn_call(q, k, v, kc, vc, lam_params, subln_g[0].reshape(1, LANES), lam_init=lam_init, tq=512)
    conv = _conv_call(conv_proj, conv_w[0], width=conv_dim, tm=ROW_TM)
    x = _out_ln_call(conv, attn, 0, 0, w_out_e[0].astype(BF16), x, mod_vec(0, 2), ln_g[0, 0:1], ln_b[0, 0:1],
                     alpha=alpha, tm=ROW_TM)
    x = _moe_call(x, mod_vec(0, 3), mod_vec(0, 4), mod_vec(0, 5), w_router_t, rbias,
                  w_gate[0].astype(BF16), w_up[0].astype(BF16), w_down[0].astype(BF16),
                  ln_g[0, 1:2], ln_b[0, 1:2], alpha=alpha, tm=ROW_TM)

    w_gmlp = w_in_o[0].astype(BF16)
    uv, = _proj_call(x, mod_vec(1, 0), mod_vec(1, 1), w_gmlp, tm=PROJ_TM, tn=tn, n_flat=w_gmlp.shape[1] // tn,
                     gelu=True)
    gated = _spatial_call(uv, v_ln_g[0:1], v_ln_b[0:1], w_spatial[0].astype(BF16), b_spatial[0].T, tm=2 * CHUNK)
    x = _out_ln_call(gated, gated, 0, 1, w_out_o[0].astype(BF16), x, mod_vec(1, 2), ln_g[1, 0:1], ln_b[1, 0:1],
                     alpha=alpha, tm=ROW_TM)
    x = _moe_call(x, mod_vec(1, 3), mod_vec(1, 4), mod_vec(1, 5), w_router_t, rbias,
                  w_gate[1].astype(BF16), w_up[1].astype(BF16), w_down[1].astype(BF16),
                  ln_g[1, 1:2], ln_b[1, 1:2], alpha=alpha, tm=ROW_TM)
    return x
```

```python
import functools
import math

import numpy as np
import jax
import jax.numpy as jnp
from jax import lax
from jax.experimental import pallas as pl
from jax.experimental.pallas import tpu as pltpu

F32 = jnp.float32
BF16 = jnp.bfloat16

GRID_W = 64
DA_HEADS = 8
N_EXPERTS = 16
N_GROUPS = 4
EXPERTS_PER_GROUP = N_EXPERTS // N_GROUPS
ROPE_BASE = 10000.0
LN_EPS = 1e-5
CHUNK = 128
GMLP_GROUPS = 16
LANES = 128
MXU_DEPTH = 256

VMEM_LIMIT = 56 * 1024 * 1024
PROJ_TM = 1024
ROW_TM = 512


def _cparams(sem):
    return pltpu.CompilerParams(dimension_semantics=sem, vmem_limit_bytes=VMEM_LIMIT)


def _layer_norm(r, g, b):
    mu = jnp.mean(r, axis=-1, keepdims=True)
    d = r - mu
    var = jnp.mean(d * d, axis=-1, keepdims=True)
    return d * lax.rsqrt(var + LN_EPS) * g + b


def _mod_kernel(c_ref, w_ref, b_ref, o_ref):
    c = c_ref[...]
    s = (c * jax.nn.sigmoid(c)).astype(BF16)
    o_ref[0] = jnp.dot(s, w_ref[0].astype(BF16), preferred_element_type=F32) + b_ref[0]


def _mod_call(cond, w_mod, b_mod):
    depth, d, n = w_mod.shape
    rows = cond.shape[0]
    tn = 1024
    return pl.pallas_call(
        _mod_kernel,
        grid=(depth, n // tn),
        in_specs=[
            pl.BlockSpec((rows, d), lambda l, j: (0, 0)),
            pl.BlockSpec((1, d, tn), lambda l, j: (l, 0, j)),
            pl.BlockSpec((1, 1, tn), lambda l, j: (l, 0, j)),
        ],
        out_specs=pl.BlockSpec((1, rows, tn), lambda l, j: (l, 0, j)),
        out_shape=jax.ShapeDtypeStruct((depth, rows, n), F32),
        compiler_params=_cparams(("arbitrary", "arbitrary")),
        name="adaln_mod",
    )(cond, w_mod, b_mod.reshape(depth, 1, n))


def _proj_kernel(*refs, n_flat, n_head_tiles, rope, gelu):
    x_ref, sh_ref, sc_ref, w_ref = refs[:4]
    pos = 4
    if rope is not None:
        cos_ref, sin_ref = refs[pos:pos + 2]
        pos += 2
    flat_ref = refs[pos] if n_flat else None
    pos += 1 if n_flat else 0
    head_refs = refs[pos:pos + n_head_tiles]
    h_ref = refs[pos + n_head_tiles]
    j = pl.program_id(2)

    @pl.when(j == 0)
    def _():
        h_ref[...] = (x_ref[0] * (1.0 + sc_ref[0]) + sh_ref[0]).astype(BF16)

    acc = jnp.dot(h_ref[...], w_ref[...], preferred_element_type=F32)

    if n_flat:
        @pl.when(j < n_flat)
        def _():
            flat_ref[0] = (jax.nn.gelu(acc, approximate=True) if gelu else acc).astype(flat_ref.dtype)

    for t in range(n_head_tiles):
        tile = n_flat + t
        o_ref = head_refs[t]

        @pl.when(j == tile)
        def _(tile=tile, o_ref=o_ref):
            rotary = rope is not None and tile in rope[0]
            if rotary:
                scale = rope[2] if tile == rope[1] else 1.0
                cs = cos_ref[...] * scale
                sn = sin_ref[...] * scale
            for h in range(acc.shape[1] // LANES):
                piece = acc[:, h * LANES:(h + 1) * LANES]
                if rotary:
                    piece = piece * cs + pltpu.roll(piece, LANES // 2, axis=1) * sn
                o_ref[0, h] = piece.astype(o_ref.dtype)


def _proj_call(x, shift, scale, w, *, tm, tn, n_flat, rope=None, gelu=False):
    bsz, s, d = x.shape
    n_tiles = w.shape[1] // tn
    n_head_tiles = n_tiles - n_flat
    heads = tn // LANES
    tm = min(tm, s)
    per_batch = shift.shape[0] > 1
    mod_map = (lambda b, i, j: (b, 0, 0)) if per_batch else (lambda b, i, j: (0, 0, 0))
    in_specs = [
        pl.BlockSpec((1, tm, d), lambda b, i, j: (b, i, 0)),
        pl.BlockSpec((1, 1, d), mod_map),
        pl.BlockSpec((1, 1, d), mod_map),
        pl.BlockSpec((d, tn), lambda b, i, j: (0, j)),
    ]
    args = [x, shift, scale, w]
    rope_static = None
    if rope is not None:
        cos_t, sin_t, rope_tiles, q_tile, q_scale = rope
        in_specs += [pl.BlockSpec((tm, LANES), lambda b, i, j: (i, 0))] * 2
        args += [cos_t, sin_t]
        rope_static = (tuple(rope_tiles), q_tile, q_scale)
    out_specs, out_shape = [], []
    if n_flat:
        out_specs.append(pl.BlockSpec((1, tm, tn), lambda b, i, j: (b, i, jnp.minimum(j, n_flat - 1))))
        out_shape.append(jax.ShapeDtypeStruct((bsz, s, n_flat * tn), BF16))
    for _ in range(n_head_tiles):
        out_specs.append(pl.BlockSpec((1, heads, tm, LANES), lambda b, i, j: (b, 0, i, 0)))
        out_shape.append(jax.ShapeDtypeStruct((bsz, heads, s, LANES), BF16))
    return pl.pallas_call(
        functools.partial(_proj_kernel, n_flat=n_flat, n_head_tiles=n_head_tiles, rope=rope_static, gelu=gelu),
        grid=(bsz, s // tm, n_tiles),
        in_specs=in_specs,
        out_specs=out_specs,
        out_shape=out_shape,
        scratch_shapes=[pltpu.VMEM((tm, d), BF16)],
        compiler_params=_cparams(("arbitrary", "arbitrary", "arbitrary")),
        name="mod_proj",
    )(*args)


ONES_ROWS = 16


def _attn_kernel(lam_ref, g_ref, q_ref, kc_ref, vc_ref, k_ref, v_ref, o_ref, kf_ref, vt_ref, s_ref, p_ref, acc_ref,
                 *, tk, lam_init):
    tq = q_ref.shape[2]
    c_len = kc_ref.shape[2]
    s_len = k_ref.shape[2]
    n_chunks = (c_len + s_len) // tk
    vt_chunk = 512

    @pl.when(pl.program_id(2) == 0)
    def _():
        kf_ref[:c_len, :] = kc_ref[0, 0]
        kf_ref[c_len:, :] = k_ref[0, 0]
        vt_ref[LANES:, :] = jnp.ones((ONES_ROWS, vt_ref.shape[1]), BF16)
        vt_ref[:LANES, :c_len] = jnp.transpose(vc_ref[0, 0].astype(F32)).astype(BF16)
        for n in range(s_len // vt_chunk):
            rows = slice(n * vt_chunk, (n + 1) * vt_chunk)
            cols = slice(c_len + n * vt_chunk, c_len + (n + 1) * vt_chunk)
            vt_ref[:LANES, cols] = jnp.transpose(v_ref[0, 0, rows, :].astype(F32)).astype(BF16)

    q_t = jnp.transpose(q_ref[0, 0].astype(F32))
    dim = lax.broadcasted_iota(jnp.int32, q_t.shape, 0)
    first_map = (dim % (LANES // 2)) < (LANES // 4)
    qs_t = jnp.concatenate([jnp.where(first_map, q_t, 0.0), jnp.where(first_map, 0.0, q_t)], axis=1).astype(BF16)

    acc_ref[...] = jnp.zeros(acc_ref.shape, F32)

    def scores(t):
        off = t * tk if isinstance(t, int) else pl.multiple_of(t * tk, tk)
        return jnp.dot(kf_ref[pl.ds(off, tk), :], qs_t, preferred_element_type=F32)

    def softmax(slot, m_old):
        s_t = s_ref[slot]
        m_new = jnp.maximum(m_old, jnp.max(s_t, axis=0, keepdims=True))
        p_ref[slot] = jnp.exp2(s_t - m_new).astype(BF16)
        return m_new, jnp.exp2(m_old - m_new)

    def values(t, slot, alpha):
        off = t * tk if isinstance(t, int) else pl.multiple_of(t * tk, tk)
        pv = jnp.dot(vt_ref[:, pl.ds(off, tk)], p_ref[slot], preferred_element_type=F32)
        acc_ref[...] = alpha * acc_ref[...] + pv

    def tick(t, parity, m, alpha_prev):
        s_ref[parity] = scores(t)
        m, alpha = softmax(1 - parity, m)
        values(t - 2, parity, alpha_prev)
        return m, alpha

    s_ref[0] = scores(0)
    s_ref[1] = scores(1)
    m, alpha = softmax(0, jnp.full((1, 2 * tq), -jnp.inf, F32))

    def pair(jj, carry):
        m, alpha = tick(2 + 2 * jj, 0, *carry)
        return tick(3 + 2 * jj, 1, m, alpha)

    n_full = n_chunks - 2
    m, alpha = lax.fori_loop(0, n_full // 2, pair, (m, alpha))
    if n_full % 2:
        m, alpha = tick(n_chunks - 1, (n_chunks - 1) % 2, m, alpha)
    last = (n_chunks - 1) % 2
    m, alpha_last = softmax(last, m)
    values(n_chunks - 2, 1 - last, alpha)
    values(n_chunks - 1, last, alpha_last)

    lp = lam_ref[...]
    lam = (jnp.exp(jnp.sum(lp[0:1] * lp[1:2], axis=-1, keepdims=True))
           - jnp.exp(jnp.sum(lp[2:3] * lp[3:4], axis=-1, keepdims=True)) + lam_init)
    acc = acc_ref[...]
    o_t = acc[:LANES] / acc[LANES:LANES + 1]
    o_t = o_t[:, :tq] - lam * o_t[:, tq:]
    o_t = o_t * lax.rsqrt(jnp.mean(o_t * o_t, axis=0, keepdims=True) + LN_EPS)
    o_ref[0, 0] = (jnp.transpose(o_t) * g_ref[...] * (1.0 - lam_init)).astype(o_ref.dtype)


def _attn_call(q, k, v, kc, vc, lam_params, subln_g, *, lam_init, tq):
    bsz, h, s, _ = q.shape
    c_len = kc.shape[2]
    tq = min(tq, s)
    tk = max(t for t in (MXU_DEPTH, 2 * MXU_DEPTH, 3 * MXU_DEPTH) if (c_len + s) % t == 0)
    assert (c_len + s) // tk >= 3 and s % 512 == 0
    whole = lambda n: pl.BlockSpec((1, 1, n, LANES), lambda b, hh, i: (b, hh, 0, 0))
    return pl.pallas_call(
        functools.partial(_attn_kernel, tk=tk, lam_init=lam_init),
        grid=(bsz, h, s // tq),
        in_specs=[
            pl.BlockSpec((8, LANES), lambda b, hh, i: (0, 0)),
            pl.BlockSpec((1, LANES), lambda b, hh, i: (0, 0)),
            pl.BlockSpec((1, 1, tq, LANES), lambda b, hh, i: (b, hh, i, 0)),
            whole(c_len), whole(c_len), whole(s), whole(s),
        ],
        out_specs=pl.BlockSpec((1, 1, tq, LANES), lambda b, hh, i: (b, hh, i, 0)),
        out_shape=jax.ShapeDtypeStruct((bsz, h, s, LANES), BF16),
        scratch_shapes=[
            pltpu.VMEM((c_len + s, LANES), BF16),
            pltpu.VMEM((LANES + ONES_ROWS, c_len + s), BF16),
            pltpu.VMEM((2, tk, 2 * tq), F32),
            pltpu.VMEM((2, tk, 2 * tq), BF16),
            pltpu.VMEM((LANES + ONES_ROWS, 2 * tq), F32),
        ],
        compiler_params=_cparams(("arbitrary", "arbitrary", "arbitrary")),
        name="diff_attn",
    )(lam_params, subln_g, q, kc, vc, k, v)


def _conv_kernel(xa_ref, bg_ref, cg_ref, xap_ref, cgp_ref, xan_ref, cgn_ref, w_ref, o_ref):
    i = pl.program_id(1)
    tm = xa_ref.shape[1]
    halo = xap_ref.shape[1]
    z = xa_ref[0].astype(F32) * cg_ref[0].astype(F32)
    z_before = (xap_ref[0].astype(F32) * cgp_ref[0].astype(F32))[halo - 1:halo]
    z_after = (xan_ref[0].astype(F32) * cgn_ref[0].astype(F32))[0:1]
    z_before = jnp.where(i == 0, 0.0, z_before)
    z_after = jnp.where(i == pl.num_programs(1) - 1, 0.0, z_after)
    row = lax.broadcasted_iota(jnp.int32, z.shape, 0)
    z_prev = jnp.where(row == 0, z_before, pltpu.roll(z, 1, axis=0))
    z_next = jnp.where(row == tm - 1, z_after, pltpu.roll(z, tm - 1, axis=0))
    w = w_ref[...]
    conv = w[0:1] * z_prev + w[1:2] * z + w[2:3] * z_next
    o_ref[0] = (bg_ref[0].astype(F32) * conv).astype(o_ref.dtype)


def _conv_call(proj, conv_w, *, width, tm):
    bsz, s, _ = proj.shape
    halo = 16
    nh = tm // halo
    last = s // halo - 1
    main = lambda c: pl.BlockSpec((1, tm, width), lambda b, i: (b, i, c))
    prev = lambda c: pl.BlockSpec((1, halo, width), lambda b, i: (b, jnp.maximum(i * nh - 1, 0), c))
    nxt = lambda c: pl.BlockSpec((1, halo, width), lambda b, i: (b, jnp.minimum((i + 1) * nh, last), c))
    w_pad = jnp.zeros((8, width), F32).at[:conv_w.shape[0]].set(conv_w)
    return pl.pallas_call(
        _conv_kernel,
        grid=(bsz, s // tm),
        in_specs=[main(0), main(1), main(2), prev(0), prev(2), nxt(0), nxt(2),
                  pl.BlockSpec((8, width), lambda b, i: (0, 0))],
        out_specs=pl.BlockSpec((1, tm, width), lambda b, i: (b, i, 0)),
        out_shape=jax.ShapeDtypeStruct((bsz, s, width), BF16),
        compiler_params=_cparams(("arbitrary", "arbitrary")),
        name="gated_conv",
    )(proj, proj, proj, proj, proj, proj, proj, w_pad)


def _spatial_kernel(u_ref, v_ref, g_ref, b_ref, ws_ref, bs_ref, o_ref):
    tm = u_ref.shape[1]
    v = _layer_norm(v_ref[0].astype(F32), g_ref[...], b_ref[...]).astype(BF16)
    bs = bs_ref[...]
    for n in range(tm // CHUNK):
        rows = slice(n * CHUNK, (n + 1) * CHUNK)
        for g in range(GMLP_GROUPS):
            cols = slice(g * LANES, (g + 1) * LANES)
            sg = jnp.dot(ws_ref[g], v[rows, cols], preferred_element_type=F32) + bs[:, g:g + 1]
            o_ref[0, rows, cols] = (u_ref[0, rows, cols].astype(F32) * sg).astype(o_ref.dtype)


def _spatial_call(uv, v_g, v_b, w_s, b_s_t, *, tm):
    bsz, s, two_d = uv.shape
    d = two_d // 2
    return pl.pallas_call(
        _spatial_kernel,
        grid=(bsz, s // tm),
        in_specs=[
            pl.BlockSpec((1, tm, d), lambda b, i: (b, i, 0)),
            pl.BlockSpec((1, tm, d), lambda b, i: (b, i, 1)),
            pl.BlockSpec((1, d), lambda b, i: (0, 0)),
            pl.BlockSpec((1, d), lambda b, i: (0, 0)),
            pl.BlockSpec(w_s.shape, lambda b, i: (0, 0, 0)),
            pl.BlockSpec(b_s_t.shape, lambda b, i: (0, 0)),
        ],
        out_specs=pl.BlockSpec((1, tm, d), lambda b, i: (b, i, 0)),
        out_shape=jax.ShapeDtypeStruct((bsz, s, d), BF16),
        compiler_params=_cparams(("arbitrary", "arbitrary")),
        name="gmlp_spatial",
    )(uv, uv, v_g, v_b, w_s, b_s_t)


def _out_ln_kernel(a1_ref, a2_ref, w_ref, x_ref, gate_ref, g_ref, b_ref, o_ref, *, alpha):
    if len(a2_ref.shape) == 4:
        second = [a2_ref[0, h] for h in range(a2_ref.shape[1])]
    else:
        second = [a2_ref[0]]
    a = jnp.concatenate([a1_ref[0]] + second, axis=1)
    y = jnp.dot(a, w_ref[...], preferred_element_type=F32)
    r = alpha * x_ref[0] + gate_ref[0] * y
    o_ref[0] = _layer_norm(r, g_ref[...], b_ref[...])


def _out_ln_call(a1, a2, col1, col2, w, x, gate, ln_g, ln_b, *, alpha, tm):
    bsz, s, d = x.shape
    half = w.shape[0] // 2
    if a2.ndim == 4:
        a2_spec = pl.BlockSpec((1, a2.shape[1], tm, LANES), lambda b, i: (b, 0, i, 0))
    else:
        a2_spec = pl.BlockSpec((1, tm, half), lambda b, i: (b, i, col2))
    return pl.pallas_call(
        functools.partial(_out_ln_kernel, alpha=alpha),
        grid=(bsz, s // tm),
        in_specs=[
            pl.BlockSpec((1, tm, half), lambda b, i: (b, i, col1)),
            a2_spec,
            pl.BlockSpec(w.shape, lambda b, i: (0, 0)),
            pl.BlockSpec((1, tm, d), lambda b, i: (b, i, 0)),
            pl.BlockSpec((1, 1, d), lambda b, i: (b, 0, 0)),
            pl.BlockSpec((1, d), lambda b, i: (0, 0)),
            pl.BlockSpec((1, d), lambda b, i: (0, 0)),
        ],
        out_specs=pl.BlockSpec((1, tm, d), lambda b, i: (b, i, 0)),
        out_shape=jax.ShapeDtypeStruct((bsz, s, d), F32),
        compiler_params=_cparams(("arbitrary", "arbitrary")),
        name="out_proj_ln",
    )(a1, a2, w, x, gate, ln_g, ln_b)


def _route(scores, bias):
    sel = scores + bias
    rows = [sel[e:e + 1, :] for e in range(N_EXPERTS)]
    group_score = []
    for g in range(N_GROUPS):
        r = rows[g * EXPERTS_PER_GROUP:(g + 1) * EXPERTS_PER_GROUP]
        best = None
        for a in range(EXPERTS_PER_GROUP):
            for b in range(a + 1, EXPERTS_PER_GROUP):
                pair = r[a] + r[b]
                best = pair if best is None else jnp.maximum(best, pair)
        group_score.append(best)
    one = jnp.ones_like(rows[0])
    zero = jnp.zeros_like(rows[0])
    picked = []
    chosen = []
    for g in range(N_GROUPS):
        beaten = zero
        for o in range(N_GROUPS):
            if o < g:
                beaten = jnp.where(group_score[o] >= group_score[g], one, beaten)
            elif o > g:
                beaten = jnp.where(group_score[o] > group_score[g], one, beaten)
        chosen.append(1.0 - beaten)
        for a in range(EXPERTS_PER_GROUP):
            e = g * EXPERTS_PER_GROUP + a
            rank = zero
            for b in range(EXPERTS_PER_GROUP):
                o = g * EXPERTS_PER_GROUP + b
                if b < a:
                    rank = rank + jnp.where(rows[o] >= rows[e], one, zero)
                elif b > a:
                    rank = rank + jnp.where(rows[o] > rows[e], one, zero)
            picked.append(jnp.where(rank < 2.0, one, zero) * chosen[g])
    w = [picked[e] * scores[e:e + 1, :] for e in range(N_EXPERTS)]
    total = w[0]
    for e in range(1, N_EXPERTS):
        total = total + w[e]
    return [we / total for we in w], chosen


ROW_BLOCK = 128
SORT_TM = 1024


def _sort_rows(tm):
    return tm + N_GROUPS * ROW_BLOCK


def _moe_sort_kernel(x_ref, sh_ref, sc_ref, wr_ref, rb_ref, xs_ref, cs_ref, dest_ref, seg_ref):
    tm, d = x_ref.shape[1], x_ref.shape[2]
    rows = xs_ref.shape[0]
    col_chunk = 512
    h = (x_ref[0] * (1.0 + sc_ref[0]) + sh_ref[0]).astype(BF16)
    logits = lax.dot_general(wr_ref[...], h, (((1,), (1,)), ((), ())), preferred_element_type=F32)
    comb, chosen = _route(jax.nn.sigmoid(logits), rb_ref[...])
    src = lax.broadcasted_iota(jnp.int32, (tm, tm), 0)
    dst = lax.broadcasted_iota(jnp.int32, (tm, tm), 1)
    before = jnp.where(src < dst, 1.0, 0.0).astype(BF16)
    pad_rows = [jnp.zeros_like(chosen[0])] * (8 - N_GROUPS)
    rank = jnp.dot(jnp.concatenate(chosen + pad_rows, axis=0).astype(BF16), before, preferred_element_type=F32)
    dest = jnp.zeros_like(chosen[0])
    first = jnp.zeros((1, 1), F32)
    firsts, counts = [], []
    for g in range(N_GROUPS):
        count = jnp.sum(chosen[g], axis=1, keepdims=True)
        n_blocks = jnp.floor((count + (ROW_BLOCK - 1)) * (1.0 / ROW_BLOCK))
        firsts.append(first)
        counts.append(n_blocks)
        dest = dest + chosen[g] * (first * ROW_BLOCK + rank[g:g + 1])
        first = first + n_blocks
    seg = jnp.concatenate([jnp.broadcast_to(v, (1, LANES)) for v in firsts + counts], axis=0)
    seg_ref[0] = seg.astype(jnp.int32)
    dest_i = dest.astype(jnp.int32)
    dest_ref[0] = dest_i
    row_id = lax.broadcasted_iota(jnp.int32, (rows, tm), 0)
    p = jnp.where(row_id == dest_i, 1.0, 0.0).astype(BF16)
    for c in range(d // col_chunk):
        cols = slice(c * col_chunk, (c + 1) * col_chunk)
        xs_ref[:, cols] = jnp.dot(p, h[:, cols], preferred_element_type=F32).astype(BF16)
    in_group = []
    for j in range(EXPERTS_PER_GROUP):
        cj = comb[j]
        for g in range(1, N_GROUPS):
            cj = cj + comb[g * EXPERTS_PER_GROUP + j]
        in_group.append(cj)
    c_rows = jnp.concatenate(in_group + [jnp.zeros((LANES - EXPERTS_PER_GROUP, tm), F32)], axis=0)
    c_cols = jnp.transpose(c_rows)
    hi = c_cols.astype(BF16)
    lo = (c_cols - hi.astype(F32)).astype(BF16)
    cs = jnp.dot(p, jnp.concatenate([hi, lo], axis=1), preferred_element_type=F32)
    cs_ref[...] = cs[:, :LANES] + cs[:, LANES:]


def _moe_sort_call(x, shift, scale, w_router_t, router_bias, *, tm):
    bsz, s, d = x.shape
    nt = s // tm
    n_tiles = bsz * nt
    rows = _sort_rows(tm)
    tile = lambda b, i: b * nt + i
    vec = lambda: pl.BlockSpec((1, 1, d), lambda b, i: (b, 0, 0))
    return pl.pallas_call(
        _moe_sort_kernel,
        grid=(bsz, nt),
        in_specs=[
            pl.BlockSpec((1, tm, d), lambda b, i: (b, i, 0)),
            vec(), vec(),
            pl.BlockSpec(w_router_t.shape, lambda b, i: (0, 0)),
            pl.BlockSpec(router_bias.shape, lambda b, i: (0, 0)),
        ],
        out_specs=[
            pl.BlockSpec((rows, d), lambda b, i: (tile(b, i), 0)),
            pl.BlockSpec((rows, LANES), lambda b, i: (tile(b, i), 0)),
            pl.BlockSpec((1, 1, tm), lambda b, i: (tile(b, i), 0, 0)),
            pl.BlockSpec((1, 2 * N_GROUPS, LANES), lambda b, i: (tile(b, i), 0, 0)),
        ],
        out_shape=[
            jax.ShapeDtypeStruct(((n_tiles + 1) * rows, d), BF16),
            jax.ShapeDtypeStruct(((n_tiles + 1) * rows, LANES), F32),
            jax.ShapeDtypeStruct((n_tiles, 1, tm), jnp.int32),
            jax.ShapeDtypeStruct((n_tiles, 2 * N_GROUPS, LANES), jnp.int32),
        ],
        compiler_params=_cparams(("arbitrary", "arbitrary")),
        name="moe_sort",
    )(x, shift, scale, w_router_t, router_bias)


def _expert_work_table(seg, rows):
    n_tiles = seg.shape[0]
    per_tile = rows // ROW_BLOCK
    spare = n_tiles * per_tile
    first = seg[:, :N_GROUPS, 0]
    count = seg[:, N_GROUPS:, 0]
    k_max = n_tiles * (per_tile - N_GROUPS + 1)
    k_max += k_max % 2
    k = jnp.arange(k_max)
    blocks, valid = [], []
    for g in range(N_GROUPS):
        cum = jnp.cumsum(count[:, g])
        t = jnp.minimum(jnp.searchsorted(cum, k, side="right"), n_tiles - 1)
        j = k - (cum[t] - count[t, g])
        blocks.append(t * per_tile + first[t, g] + j)
        valid.append(k < cum[-1])
    blocks = jnp.stack(blocks)
    valid = jnp.stack(valid)
    a, b = blocks[:, 0::2], blocks[:, 1::2]
    va, vb = valid[:, 0::2], valid[:, 1::2]
    grp = jnp.broadcast_to(jnp.arange(N_GROUPS)[:, None], a.shape)
    order = jnp.argsort(jnp.where(va, grp, N_GROUPS).reshape(-1), stable=True)
    max_items = (n_tiles * per_tile) // 2 + N_GROUPS
    order = order[:max_items]
    a, b, va, vb, grp = (v.reshape(-1)[order] for v in (a, b, va, vb, grp))
    n_items = jnp.sum(va.astype(jnp.int32))
    last_grp = grp[jnp.maximum(n_items - 1, 0)]
    blk_a = jnp.where(va, a, spare)
    blk_b = jnp.where(vb, b, blk_a)
    grp = jnp.where(va, grp, last_grp)
    step = jnp.arange(max_items)
    where = jnp.full((spare,), 2 * max_items, jnp.int32)
    where = where.at[jnp.where(va, a, spare)].set(2 * step, mode="drop")
    where = where.at[jnp.where(vb, b, spare)].set(2 * step + 1, mode="drop")
    as_i32 = lambda v: v.astype(jnp.int32)
    pad1 = lambda v, fill: jnp.concatenate([as_i32(v), jnp.full((1,), fill, jnp.int32)])
    table = (pad1(blk_a, spare), pad1(blk_b, spare), pad1(grp, 0).at[-1].set(as_i32(last_grp)),
             as_i32(n_items).reshape(1))
    return table, where


def _moe_expert_kernel(blk_a, blk_b, grp, n_items, xa_ref, xb_ref, ca_ref, cb_ref, wgu_ref, wd_ref, y_ref):
    f = wd_ref.shape[1] // EXPERTS_PER_GROUP

    @pl.when(pl.program_id(0) < n_items[0])
    def _():
        x = jnp.concatenate([xa_ref[...], xb_ref[...]], axis=0)
        c = jnp.concatenate([ca_ref[...], cb_ref[...]], axis=0)
        gu = jnp.dot(x, wgu_ref[0], preferred_element_type=F32)
        pieces = []
        for j in range(EXPERTS_PER_GROUP):
            gate = gu[:, j * f:(j + 1) * f]
            up = gu[:, (EXPERTS_PER_GROUP + j) * f:(EXPERTS_PER_GROUP + j + 1) * f]
            pieces.append((jax.nn.silu(gate) * up * c[:, j:j + 1]).astype(BF16))
        y = jnp.dot(jnp.concatenate(pieces, axis=1), wd_ref[0], preferred_element_type=F32)
        y_ref[...] = y.astype(BF16)

    @pl.when(pl.program_id(0) >= n_items[0])
    def _():
        y_ref[...] = jnp.zeros(y_ref.shape, BF16)


def _moe_expert_call(table, xs, cs, w_gate_up, w_down):
    d = xs.shape[1]
    steps = table[0].shape[0]
    resident = pl.Buffered(1)
    grid_spec = pltpu.PrefetchScalarGridSpec(
        num_scalar_prefetch=4,
        grid=(steps,),
        in_specs=[
            pl.BlockSpec((ROW_BLOCK, d), lambda i, a, b, g, n: (a[i], 0)),
            pl.BlockSpec((ROW_BLOCK, d), lambda i, a, b, g, n: (b[i], 0)),
            pl.BlockSpec((ROW_BLOCK, LANES), lambda i, a, b, g, n: (a[i], 0)),
            pl.BlockSpec((ROW_BLOCK, LANES), lambda i, a, b, g, n: (b[i], 0)),
            pl.BlockSpec((1,) + w_gate_up.shape[1:], lambda i, a, b, g, n: (g[i], 0, 0), pipeline_mode=resident),
            pl.BlockSpec((1,) + w_down.shape[1:], lambda i, a, b, g, n: (g[i], 0, 0), pipeline_mode=resident),
        ],
        out_specs=pl.BlockSpec((2 * ROW_BLOCK, d), lambda i, a, b, g, n: (i, 0)),
    )
    return pl.pallas_call(
        _moe_expert_kernel,
        grid_spec=grid_spec,
        out_shape=jax.ShapeDtypeStruct((steps * 2 * ROW_BLOCK, d), BF16),
        compiler_params=_cparams(("arbitrary",)),
        name="moe_experts",
    )(*table, xs, xs, cs, cs, w_gate_up, w_down)


def _moe_combine_kernel(where_ref, dest_ref, *refs, alpha, n_blocks):
    y_refs = refs[:n_blocks]
    x_ref, gate_ref, g_ref, b_ref, o_ref = refs[n_blocks:]
    tm = x_ref.shape[1]
    ys = jnp.concatenate([r[...] for r in y_refs], axis=0)
    row_id = lax.broadcasted_iota(jnp.int32, (ys.shape[0], tm), 0)
    p = jnp.where(row_id == dest_ref[0], 1.0, 0.0).astype(BF16)
    y = lax.dot_general(p, ys, (((0,), (0,)), ((), ())), preferred_element_type=F32)
    r = alpha * x_ref[0] + gate_ref[0] * y
    o_ref[0] = _layer_norm(r, g_ref[...], b_ref[...])


def _moe_combine_call(where, dest, ys, x, gate, ln_g, ln_b, *, alpha, sort_tm, tm):
    bsz, s, d = x.shape
    nt = s // sort_tm
    parts = sort_tm // tm
    n_blocks = _sort_rows(sort_tm) // ROW_BLOCK
    tile = lambda b, i: b * nt + i
    y_spec = lambda j: pl.BlockSpec((ROW_BLOCK, d), lambda b, i, k, w: (w[tile(b, i) * n_blocks + j], 0))
    grid_spec = pltpu.PrefetchScalarGridSpec(
        num_scalar_prefetch=1,
        grid=(bsz, nt, parts),
        in_specs=[pl.BlockSpec((1, 1, tm), lambda b, i, k, w: (tile(b, i), 0, k))]
        + [y_spec(j) for j in range(n_blocks)]
        + [
            pl.BlockSpec((1, tm, d), lambda b, i, k, w: (b, i * parts + k, 0)),
            pl.BlockSpec((1, 1, d), lambda b, i, k, w: (b, 0, 0)),
            pl.BlockSpec((1, d), lambda b, i, k, w: (0, 0)),
            pl.BlockSpec((1, d), lambda b, i, k, w: (0, 0)),
        ],
        out_specs=pl.BlockSpec((1, tm, d), lambda b, i, k, w: (b, i * parts + k, 0)),
    )
    return pl.pallas_call(
        functools.partial(_moe_combine_kernel, alpha=alpha, n_blocks=n_blocks),
        grid_spec=grid_spec,
        out_shape=jax.ShapeDtypeStruct((bsz, s, d), F32),
        compiler_params=_cparams(("arbitrary", "arbitrary", "arbitrary")),
        name="moe_combine",
    )(where, dest, *([ys] * n_blocks), x, gate, ln_g, ln_b)


def _moe(x, shift, scale, gate, w_router_t, router_bias, w_gate_up, w_down, ln_g, ln_b, *, alpha):
    sort_tm = min(SORT_TM, x.shape[1])
    xs, cs, dest, seg = _moe_sort_call(x, shift, scale, w_router_t, router_bias, tm=sort_tm)
    table, where = _expert_work_table(seg, _sort_rows(sort_tm))
    ys = _moe_expert_call(table, xs, cs, w_gate_up, w_down)
    return _moe_combine_call(where, dest, ys, x, gate, ln_g, ln_b, alpha=alpha, sort_tm=sort_tm,
                             tm=min(ROW_TM, sort_tm))


def _group_expert_weights(w_gate, w_up, w_down):
    e, d, f = w_gate.shape
    side = lambda w: w.reshape(N_GROUPS, EXPERTS_PER_GROUP, d, f).transpose(0, 2, 1, 3).reshape(
        N_GROUPS, d, EXPERTS_PER_GROUP * f)
    w_gate_up = jnp.concatenate([side(w_gate), side(w_up)], axis=2).astype(BF16)
    return w_gate_up, w_down.reshape(N_GROUPS, EXPERTS_PER_GROUP * f, d).astype(BF16)


def _rope_tables(n):
    freqs = LANES // 8
    inv = jnp.power(ROPE_BASE, -jnp.arange(freqs, dtype=F32) / freqs)
    tok = jnp.arange(n)
    ang_r = (tok // GRID_W).astype(F32)[:, None] * inv
    ang_c = (tok % GRID_W).astype(F32)[:, None] * inv
    ang = jnp.concatenate([ang_r, ang_c], axis=1)
    cos = jnp.tile(jnp.cos(ang), (1, 4))
    sin = jnp.tile(jnp.sin(ang), (1, 4))
    sign = jnp.where(jnp.arange(LANES) < LANES // 2, -1.0, 1.0).astype(F32)
    return cos, sin * sign


def _rope_column_perm():
    perm = np.zeros((DA_HEADS, 2, 2, 2, 16), np.int32)
    for h in range(DA_HEADS):
        for p in range(2):
            for m in range(2):
                for ax in range(2):
                    for f in range(16):
                        perm[h, p, m, ax, f] = h * LANES + m * 64 + ax * 32 + p * 16 + f
    return perm.reshape(-1)


def kernel(x, c, ctx, c_ctx, w_mod, b_mod, ln_g, ln_b, w_in_e, conv_w, lambda_q1, lambda_k1, lambda_q2, lambda_k2,
           subln_g, w_out_e, w_in_o, v_ln_g, v_ln_b, w_spatial, b_spatial, w_out_o, w_router, router_bias,
           w_gate, w_up, w_down):
    bsz, s, d = x.shape
    depth = w_mod.shape[0]
    assert depth == 2 and w_in_e.shape[0] == 1 and w_in_o.shape[0] == 1, "two-layer (even, odd) stack only"
    conv_dim = conv_w.shape[-1]
    q_dim = DA_HEADS * LANES
    q_col, k_col, v_col = 3 * conv_dim, 3 * conv_dim + q_dim, 3 * conv_dim + 2 * q_dim
    assert w_in_e.shape[2] == v_col + q_dim and s % GRID_W == 0
    alpha = float((2 * depth) ** 0.25)

    rows = 8 * ((bsz + 1 + 7) // 8)
    cond = jnp.zeros((rows, d), F32).at[:bsz].set(c).at[bsz].set(c_ctx)
    mods = _mod_call(cond, w_mod, b_mod)

    def mod_vec(l, k, ctx_row=False):
        v = mods[l, bsz:bsz + 1, k * d:(k + 1) * d] if ctx_row else mods[l, :bsz, k * d:(k + 1) * d]
        return v.reshape(-1, 1, d)

    w_router_t = w_router.T.astype(BF16)
    rbias = router_bias.reshape(-1, 1).astype(F32)

    perm = _rope_column_perm()
    w_in = w_in_e[0]
    w_in = jnp.concatenate(
        [w_in[:, :q_col], w_in[:, q_col:k_col][:, perm], w_in[:, k_col:v_col][:, perm], w_in[:, v_col:]],
        axis=1).astype(BF16)
    cos_t, sin_t = _rope_tables(s)
    tn = q_dim
    assert q_col % tn == 0
    q_tile, k_tile = q_col // tn, k_col // tn
    conv_proj, q, k, v = _proj_call(
        x, mod_vec(0, 0), mod_vec(0, 1), w_in, tm=PROJ_TM, tn=tn, n_flat=q_tile,
        rope=(cos_t, sin_t, (q_tile, k_tile), q_tile, float(64 ** -0.5 * math.log2(math.e))))
    kc, vc = _proj_call(ctx, mod_vec(0, 0, True), mod_vec(0, 1, True), w_in[:, k_col:], tm=PROJ_TM, tn=tn, n_flat=0)

    lam_init = 0.8 - 0.6 * math.exp(-0.3 * 0)
    lam_params = jnp.zeros((8, LANES), F32)
    for r, p in enumerate((lambda_q1, lambda_k1, lambda_q2, lambda_k2)):
        lam_params = lam_params.at[r, :p.shape[-1]].set(p[0].astype(F32))
    attn = _attn_call(q, k, v, kc, vc, lam_params, subln_g[0].reshape(1, LANES), lam_init=lam_init, tq=512)
    conv = _conv_call(conv_proj, conv_w[0], width=conv_dim, tm=ROW_TM)
    x = _out_ln_call(conv, attn, 0, 0, w_out_e[0].astype(BF16), x, mod_vec(0, 2), ln_g[0, 0:1], ln_b[0, 0:1],
                     alpha=alpha, tm=ROW_TM)
    x = _moe(x, mod_vec(0, 3), mod_vec(0, 4), mod_vec(0, 5), w_router_t, rbias,
             *_group_expert_weights(w_gate[0], w_up[0], w_down[0]), ln_g[0, 1:2], ln_b[0, 1:2], alpha=alpha)

    w_gmlp = w_in_o[0].astype(BF16)
    uv, = _proj_call(x, mod_vec(1, 0), mod_vec(1, 1), w_gmlp, tm=PROJ_TM, tn=tn, n_flat=w_gmlp.shape[1] // tn,
                     gelu=True)
    gated = _spatial_call(uv, v_ln_g[0:1], v_ln_b[0:1], w_spatial[0].astype(BF16), b_spatial[0].T, tm=2 * CHUNK)
    x = _out_ln_call(gated, gated, 0, 1, w_out_o[0].astype(BF16), x, mod_vec(1, 2), ln_g[1, 0:1], ln_b[1, 0:1],
                     alpha=alpha, tm=ROW_TM)
    x = _moe(x, mod_vec(1, 3), mod_vec(1, 4), mod_vec(1, 5), w_router_t, rbias,
             *_group_expert_weights(w_gate[1], w_up[1], w_down[1]), ln_g[1, 1:2], ln_b[1, 1:2], alpha=alpha)
    return x
```

```python
import functools
import math

import numpy as np
import jax
import jax.numpy as jnp
from jax import lax
from jax.experimental import pallas as pl
from jax.experimental.pallas import tpu as pltpu

F32 = jnp.float32
BF16 = jnp.bfloat16

GRID_W = 64
DA_HEADS = 8
N_EXPERTS = 16
N_GROUPS = 4
EXPERTS_PER_GROUP = N_EXPERTS // N_GROUPS
ROPE_BASE = 10000.0
LN_EPS = 1e-5
CHUNK = 128
GMLP_GROUPS = 16
LANES = 128
MXU_DEPTH = 256

VMEM_LIMIT = 56 * 1024 * 1024
PROJ_TM = 1024
ROW_TM = 512
ATTN_TQ = 1024


def _cparams(sem):
    return pltpu.CompilerParams(dimension_semantics=sem, vmem_limit_bytes=VMEM_LIMIT)


def _layer_norm(r, g, b):
    mu = jnp.mean(r, axis=-1, keepdims=True)
    d = r - mu
    var = jnp.mean(d * d, axis=-1, keepdims=True)
    return d * lax.rsqrt(var + LN_EPS) * g + b


def _mod_kernel(c_ref, w_ref, b_ref, o_ref):
    c = c_ref[...]
    s = (c * jax.nn.sigmoid(c)).astype(BF16)
    o_ref[0] = jnp.dot(s, w_ref[0].astype(BF16), preferred_element_type=F32) + b_ref[0]


def _mod_call(cond, w_mod, b_mod):
    depth, d, n = w_mod.shape
    rows = cond.shape[0]
    tn = 1024
    return pl.pallas_call(
        _mod_kernel,
        grid=(depth, n // tn),
        in_specs=[
            pl.BlockSpec((rows, d), lambda l, j: (0, 0)),
            pl.BlockSpec((1, d, tn), lambda l, j: (l, 0, j)),
            pl.BlockSpec((1, 1, tn), lambda l, j: (l, 0, j)),
        ],
        out_specs=pl.BlockSpec((1, rows, tn), lambda l, j: (l, 0, j)),
        out_shape=jax.ShapeDtypeStruct((depth, rows, n), F32),
        compiler_params=_cparams(("arbitrary", "arbitrary")),
        name="adaln_mod",
    )(cond, w_mod, b_mod.reshape(depth, 1, n))


def _proj_kernel(*refs, n_flat, n_head_tiles, rope, gelu):
    x_ref, sh_ref, sc_ref, w_ref = refs[:4]
    pos = 4
    if rope is not None:
        cos_ref, sin_ref = refs[pos:pos + 2]
        pos += 2
    flat_ref = refs[pos] if n_flat else None
    pos += 1 if n_flat else 0
    head_refs = refs[pos:pos + n_head_tiles]
    h_ref = refs[pos + n_head_tiles]
    j = pl.program_id(2)

    @pl.when(j == 0)
    def _():
        h_ref[...] = (x_ref[0] * (1.0 + sc_ref[0]) + sh_ref[0]).astype(BF16)

    acc = jnp.dot(h_ref[...], w_ref[...], preferred_element_type=F32)

    if n_flat:
        @pl.when(j < n_flat)
        def _():
            flat_ref[0] = (jax.nn.gelu(acc, approximate=True) if gelu else acc).astype(flat_ref.dtype)

    for t in range(n_head_tiles):
        tile = n_flat + t
        o_ref = head_refs[t]

        @pl.when(j == tile)
        def _(tile=tile, o_ref=o_ref):
            rotary = rope is not None and tile in rope[0]
            if rotary:
                scale = rope[2] if tile == rope[1] else 1.0
                cs = cos_ref[...] * scale
                sn = sin_ref[...] * scale
            for h in range(acc.shape[1] // LANES):
                piece = acc[:, h * LANES:(h + 1) * LANES]
                if rotary:
                    piece = piece * cs + pltpu.roll(piece, LANES // 2, axis=1) * sn
                o_ref[0, h] = piece.astype(o_ref.dtype)


def _proj_call(x, shift, scale, w, *, tm, tn, n_flat, rope=None, gelu=False):
    bsz, s, d = x.shape
    n_tiles = w.shape[1] // tn
    n_head_tiles = n_tiles - n_flat
    heads = tn // LANES
    tm = min(tm, s)
    per_batch = shift.shape[0] > 1
    mod_map = (lambda b, i, j: (b, 0, 0)) if per_batch else (lambda b, i, j: (0, 0, 0))
    in_specs = [
        pl.BlockSpec((1, tm, d), lambda b, i, j: (b, i, 0)),
        pl.BlockSpec((1, 1, d), mod_map),
        pl.BlockSpec((1, 1, d), mod_map),
        pl.BlockSpec((d, tn), lambda b, i, j: (0, j)),
    ]
    args = [x, shift, scale, w]
    rope_static = None
    if rope is not None:
        cos_t, sin_t, rope_tiles, q_tile, q_scale = rope
        in_specs += [pl.BlockSpec((tm, LANES), lambda b, i, j: (i, 0))] * 2
        args += [cos_t, sin_t]
        rope_static = (tuple(rope_tiles), q_tile, q_scale)
    out_specs, out_shape = [], []
    if n_flat:
        out_specs.append(pl.BlockSpec((1, tm, tn), lambda b, i, j: (b, i, jnp.minimum(j, n_flat - 1))))
        out_shape.append(jax.ShapeDtypeStruct((bsz, s, n_flat * tn), BF16))
    for _ in range(n_head_tiles):
        out_specs.append(pl.BlockSpec((1, heads, tm, LANES), lambda b, i, j: (b, 0, i, 0)))
        out_shape.append(jax.ShapeDtypeStruct((bsz, heads, s, LANES), BF16))
    return pl.pallas_call(
        functools.partial(_proj_kernel, n_flat=n_flat, n_head_tiles=n_head_tiles, rope=rope_static, gelu=gelu),
        grid=(bsz, s // tm, n_tiles),
        in_specs=in_specs,
        out_specs=out_specs,
        out_shape=out_shape,
        scratch_shapes=[pltpu.VMEM((tm, d), BF16)],
        compiler_params=_cparams(("arbitrary", "arbitrary", "arbitrary")),
        name="mod_proj",
    )(*args)


ONES_ROWS = 16


def _attn_kernel(lam_ref, g_ref, q_ref, kc_ref, vc_ref, k_ref, v_ref, o_ref, kf_ref, vt_ref, s_ref, p_ref, acc_ref,
                 *, tk, lam_init):
    tq = q_ref.shape[2]
    c_len = kc_ref.shape[2]
    s_len = k_ref.shape[2]
    n_chunks = (c_len + s_len) // tk
    vt_chunk = 512

    @pl.when(pl.program_id(2) == 0)
    def _():
        kf_ref[:c_len, :] = kc_ref[0, 0]
        kf_ref[c_len:, :] = k_ref[0, 0]
        vt_ref[LANES:, :] = jnp.ones((ONES_ROWS, vt_ref.shape[1]), BF16)
        vt_ref[:LANES, :c_len] = jnp.transpose(vc_ref[0, 0].astype(F32)).astype(BF16)
        for n in range(s_len // vt_chunk):
            rows = slice(n * vt_chunk, (n + 1) * vt_chunk)
            cols = slice(c_len + n * vt_chunk, c_len + (n + 1) * vt_chunk)
            vt_ref[:LANES, cols] = jnp.transpose(v_ref[0, 0, rows, :].astype(F32)).astype(BF16)

    q_t = jnp.transpose(q_ref[0, 0].astype(F32))
    dim = lax.broadcasted_iota(jnp.int32, q_t.shape, 0)
    first_map = (dim % (LANES // 2)) < (LANES // 4)
    qs_t = jnp.concatenate([jnp.where(first_map, q_t, 0.0), jnp.where(first_map, 0.0, q_t)], axis=1).astype(BF16)

    acc_ref[...] = jnp.zeros(acc_ref.shape, F32)

    def scores(t):
        off = t * tk if isinstance(t, int) else pl.multiple_of(t * tk, tk)
        return jnp.dot(kf_ref[pl.ds(off, tk), :], qs_t, preferred_element_type=F32)

    def softmax(slot, m_old):
        s_t = s_ref[slot]
        m_new = jnp.maximum(m_old, jnp.max(s_t, axis=0, keepdims=True))
        p_ref[slot] = jnp.exp2((s_t - m_new).astype(BF16))
        return m_new, jnp.exp2(m_old - m_new)

    def values(t, slot, alpha):
        off = t * tk if isinstance(t, int) else pl.multiple_of(t * tk, tk)
        pv = jnp.dot(vt_ref[:, pl.ds(off, tk)], p_ref[slot], preferred_element_type=F32)
        acc_ref[...] = alpha * acc_ref[...] + pv

    def tick(t, parity, m, alpha_prev):
        s_ref[parity] = scores(t)
        m, alpha = softmax(1 - parity, m)
        values(t - 2, parity, alpha_prev)
        return m, alpha

    s_ref[0] = scores(0)
    s_ref[1] = scores(1)
    m, alpha = softmax(0, jnp.full((1, 2 * tq), -jnp.inf, F32))

    def pair(jj, carry):
        m, alpha = tick(2 + 2 * jj, 0, *carry)
        return tick(3 + 2 * jj, 1, m, alpha)

    n_full = n_chunks - 2
    m, alpha = lax.fori_loop(0, n_full // 2, pair, (m, alpha))
    if n_full % 2:
        m, alpha = tick(n_chunks - 1, (n_chunks - 1) % 2, m, alpha)
    last = (n_chunks - 1) % 2
    m, alpha_last = softmax(last, m)
    values(n_chunks - 2, 1 - last, alpha)
    values(n_chunks - 1, last, alpha_last)

    lp = lam_ref[...]
    lam = (jnp.exp(jnp.sum(lp[0:1] * lp[1:2], axis=-1, keepdims=True))
           - jnp.exp(jnp.sum(lp[2:3] * lp[3:4], axis=-1, keepdims=True)) + lam_init)
    acc = acc_ref[...]
    o_t = acc[:LANES] / acc[LANES:LANES + 1]
    o_t = o_t[:, :tq] - lam * o_t[:, tq:]
    o_t = o_t * lax.rsqrt(jnp.mean(o_t * o_t, axis=0, keepdims=True) + LN_EPS)
    o_ref[0, 0] = (jnp.transpose(o_t) * g_ref[...] * (1.0 - lam_init)).astype(o_ref.dtype)


def _attn_call(q, k, v, kc, vc, lam_params, subln_g, *, lam_init, tq):
    bsz, h, s, _ = q.shape
    c_len = kc.shape[2]
    tq = min(tq, s)
    tk = max(t for t in (MXU_DEPTH, 2 * MXU_DEPTH, 3 * MXU_DEPTH) if (c_len + s) % t == 0)
    assert (c_len + s) // tk >= 3 and s % 512 == 0
    whole = lambda n: pl.BlockSpec((1, 1, n, LANES), lambda b, hh, i: (b, hh, 0, 0))
    return pl.pallas_call(
        functools.partial(_attn_kernel, tk=tk, lam_init=lam_init),
        grid=(bsz, h, s // tq),
        in_specs=[
            pl.BlockSpec((8, LANES), lambda b, hh, i: (0, 0)),
            pl.BlockSpec((1, LANES), lambda b, hh, i: (0, 0)),
            pl.BlockSpec((1, 1, tq, LANES), lambda b, hh, i: (b, hh, i, 0)),
            whole(c_len), whole(c_len), whole(s), whole(s),
        ],
        out_specs=pl.BlockSpec((1, 1, tq, LANES), lambda b, hh, i: (b, hh, i, 0)),
        out_shape=jax.ShapeDtypeStruct((bsz, h, s, LANES), BF16),
        scratch_shapes=[
            pltpu.VMEM((c_len + s, LANES), BF16),
            pltpu.VMEM((LANES + ONES_ROWS, c_len + s), BF16),
            pltpu.VMEM((2, tk, 2 * tq), F32),
            pltpu.VMEM((2, tk, 2 * tq), BF16),
            pltpu.VMEM((LANES + ONES_ROWS, 2 * tq), F32),
        ],
        compiler_params=_cparams(("arbitrary", "arbitrary", "arbitrary")),
        name="diff_attn",
    )(lam_params, subln_g, q, kc, vc, k, v)


def _conv_kernel(xa_ref, bg_ref, cg_ref, xap_ref, cgp_ref, xan_ref, cgn_ref, w_ref, o_ref):
    i = pl.program_id(1)
    tm = xa_ref.shape[1]
    halo = xap_ref.shape[1]
    z = xa_ref[0].astype(F32) * cg_ref[0].astype(F32)
    z_before = (xap_ref[0].astype(F32) * cgp_ref[0].astype(F32))[halo - 1:halo]
    z_after = (xan_ref[0].astype(F32) * cgn_ref[0].astype(F32))[0:1]
    z_before = jnp.where(i == 0, 0.0, z_before)
    z_after = jnp.where(i == pl.num_programs(1) - 1, 0.0, z_after)
    row = lax.broadcasted_iota(jnp.int32, z.shape, 0)
    z_prev = jnp.where(row == 0, z_before, pltpu.roll(z, 1, axis=0))
    z_next = jnp.where(row == tm - 1, z_after, pltpu.roll(z, tm - 1, axis=0))
    w = w_ref[...]
    conv = w[0:1] * z_prev + w[1:2] * z + w[2:3] * z_next
    o_ref[0] = (bg_ref[0].astype(F32) * conv).astype(o_ref.dtype)


def _conv_call(proj, conv_w, *, width, tm):
    bsz, s, _ = proj.shape
    halo = 16
    nh = tm // halo
    last = s // halo - 1
    main = lambda c: pl.BlockSpec((1, tm, width), lambda b, i: (b, i, c))
    prev = lambda c: pl.BlockSpec((1, halo, width), lambda b, i: (b, jnp.maximum(i * nh - 1, 0), c))
    nxt = lambda c: pl.BlockSpec((1, halo, width), lambda b, i: (b, jnp.minimum((i + 1) * nh, last), c))
    w_pad = jnp.zeros((8, width), F32).at[:conv_w.shape[0]].set(conv_w)
    return pl.pallas_call(
        _conv_kernel,
        grid=(bsz, s // tm),
        in_specs=[main(0), main(1), main(2), prev(0), prev(2), nxt(0), nxt(2),
                  pl.BlockSpec((8, width), lambda b, i: (0, 0))],
        out_specs=pl.BlockSpec((1, tm, width), lambda b, i: (b, i, 0)),
        out_shape=jax.ShapeDtypeStruct((bsz, s, width), BF16),
        compiler_params=_cparams(("arbitrary", "arbitrary")),
        name="gated_conv",
    )(proj, proj, proj, proj, proj, proj, proj, w_pad)


def _spatial_kernel(u_ref, v_ref, g_ref, b_ref, ws_ref, bs_ref, o_ref):
    tm = u_ref.shape[1]
    v = _layer_norm(v_ref[0].astype(F32), g_ref[...], b_ref[...]).astype(BF16)
    bs = bs_ref[...]
    for n in range(tm // CHUNK):
        rows = slice(n * CHUNK, (n + 1) * CHUNK)
        for g in range(GMLP_GROUPS):
            cols = slice(g * LANES, (g + 1) * LANES)
            sg = jnp.dot(ws_ref[g], v[rows, cols], preferred_element_type=F32) + bs[:, g:g + 1]
            o_ref[0, rows, cols] = (u_ref[0, rows, cols].astype(F32) * sg).astype(o_ref.dtype)


def _spatial_call(uv, v_g, v_b, w_s, b_s_t, *, tm):
    bsz, s, two_d = uv.shape
    d = two_d // 2
    return pl.pallas_call(
        _spatial_kernel,
        grid=(bsz, s // tm),
        in_specs=[
            pl.BlockSpec((1, tm, d), lambda b, i: (b, i, 0)),
            pl.BlockSpec((1, tm, d), lambda b, i: (b, i, 1)),
            pl.BlockSpec((1, d), lambda b, i: (0, 0)),
            pl.BlockSpec((1, d), lambda b, i: (0, 0)),
            pl.BlockSpec(w_s.shape, lambda b, i: (0, 0, 0)),
            pl.BlockSpec(b_s_t.shape, lambda b, i: (0, 0)),
        ],
        out_specs=pl.BlockSpec((1, tm, d), lambda b, i: (b, i, 0)),
        out_shape=jax.ShapeDtypeStruct((bsz, s, d), BF16),
        compiler_params=_cparams(("arbitrary", "arbitrary")),
        name="gmlp_spatial",
    )(uv, uv, v_g, v_b, w_s, b_s_t)


def _out_ln_kernel(a1_ref, a2_ref, w_ref, x_ref, gate_ref, g_ref, b_ref, o_ref, *, alpha):
    if len(a2_ref.shape) == 4:
        second = [a2_ref[0, h] for h in range(a2_ref.shape[1])]
    else:
        second = [a2_ref[0]]
    a = jnp.concatenate([a1_ref[0]] + second, axis=1)
    y = jnp.dot(a, w_ref[...], preferred_element_type=F32)
    r = alpha * x_ref[0] + gate_ref[0] * y
    o_ref[0] = _layer_norm(r, g_ref[...], b_ref[...])


def _out_ln_call(a1, a2, col1, col2, w, x, gate, ln_g, ln_b, *, alpha, tm):
    bsz, s, d = x.shape
    half = w.shape[0] // 2
    if a2.ndim == 4:
        a2_spec = pl.BlockSpec((1, a2.shape[1], tm, LANES), lambda b, i: (b, 0, i, 0))
    else:
        a2_spec = pl.BlockSpec((1, tm, half), lambda b, i: (b, i, col2))
    return pl.pallas_call(
        functools.partial(_out_ln_kernel, alpha=alpha),
        grid=(bsz, s // tm),
        in_specs=[
            pl.BlockSpec((1, tm, half), lambda b, i: (b, i, col1)),
            a2_spec,
            pl.BlockSpec(w.shape, lambda b, i: (0, 0)),
            pl.BlockSpec((1, tm, d), lambda b, i: (b, i, 0)),
            pl.BlockSpec((1, 1, d), lambda b, i: (b, 0, 0)),
            pl.BlockSpec((1, d), lambda b, i: (0, 0)),
            pl.BlockSpec((1, d), lambda b, i: (0, 0)),
        ],
        out_specs=pl.BlockSpec((1, tm, d), lambda b, i: (b, i, 0)),
        out_shape=jax.ShapeDtypeStruct((bsz, s, d), F32),
        compiler_params=_cparams(("arbitrary", "arbitrary")),
        name="out_proj_ln",
    )(a1, a2, w, x, gate, ln_g, ln_b)


def _route(scores, bias):
    sel = scores + bias
    rows = [sel[e:e + 1, :] for e in range(N_EXPERTS)]
    group_score = []
    for g in range(N_GROUPS):
        r = rows[g * EXPERTS_PER_GROUP:(g + 1) * EXPERTS_PER_GROUP]
        best = None
        for a in range(EXPERTS_PER_GROUP):
            for b in range(a + 1, EXPERTS_PER_GROUP):
                pair = r[a] + r[b]
                best = pair if best is None else jnp.maximum(best, pair)
        group_score.append(best)
    one = jnp.ones_like(rows[0])
    zero = jnp.zeros_like(rows[0])
    picked = []
    chosen = []
    for g in range(N_GROUPS):
        beaten = zero
        for o in range(N_GROUPS):
            if o < g:
                beaten = jnp.where(group_score[o] >= group_score[g], one, beaten)
            elif o > g:
                beaten = jnp.where(group_score[o] > group_score[g], one, beaten)
        chosen.append(1.0 - beaten)
        for a in range(EXPERTS_PER_GROUP):
            e = g * EXPERTS_PER_GROUP + a
            rank = zero
            for b in range(EXPERTS_PER_GROUP):
                o = g * EXPERTS_PER_GROUP + b
                if b < a:
                    rank = rank + jnp.where(rows[o] >= rows[e], one, zero)
                elif b > a:
                    rank = rank + jnp.where(rows[o] > rows[e], one, zero)
            picked.append(jnp.where(rank < 2.0, one, zero) * chosen[g])
    w = [picked[e] * scores[e:e + 1, :] for e in range(N_EXPERTS)]
    total = w[0]
    for e in range(1, N_EXPERTS):
        total = total + w[e]
    return [we / total for we in w], chosen


ROW_BLOCK = 128
SORT_TM = 1024


def _sort_rows(tm):
    return tm + N_GROUPS * ROW_BLOCK


def _moe_sort_kernel(x_ref, sh_ref, sc_ref, wr_ref, rb_ref, xs_ref, cs_ref, dest_ref, seg_ref):
    tm, d = x_ref.shape[1], x_ref.shape[2]
    rows = xs_ref.shape[0]
    col_chunk = 512
    h = (x_ref[0] * (1.0 + sc_ref[0]) + sh_ref[0]).astype(BF16)
    logits = lax.dot_general(wr_ref[...], h, (((1,), (1,)), ((), ())), preferred_element_type=F32)
    comb, chosen = _route(jax.nn.sigmoid(logits), rb_ref[...])
    src = lax.broadcasted_iota(jnp.int32, (tm, tm), 0)
    dst = lax.broadcasted_iota(jnp.int32, (tm, tm), 1)
    before = jnp.where(src < dst, 1.0, 0.0).astype(BF16)
    pad_rows = [jnp.zeros_like(chosen[0])] * (8 - N_GROUPS)
    rank = jnp.dot(jnp.concatenate(chosen + pad_rows, axis=0).astype(BF16), before, preferred_element_type=F32)
    dest = jnp.zeros_like(chosen[0])
    first = jnp.zeros((1, 1), F32)
    firsts, counts = [], []
    for g in range(N_GROUPS):
        count = jnp.sum(chosen[g], axis=1, keepdims=True)
        n_blocks = jnp.floor((count + (ROW_BLOCK - 1)) * (1.0 / ROW_BLOCK))
        firsts.append(first)
        counts.append(n_blocks)
        dest = dest + chosen[g] * (first * ROW_BLOCK + rank[g:g + 1])
        first = first + n_blocks
    seg = jnp.concatenate([jnp.broadcast_to(v, (1, LANES)) for v in firsts + counts], axis=0)
    seg_ref[0] = seg.astype(jnp.int32)
    dest_i = dest.astype(jnp.int32)
    dest_ref[0] = dest_i
    row_id = lax.broadcasted_iota(jnp.int32, (rows, tm), 0)
    p = jnp.where(row_id == dest_i, 1.0, 0.0).astype(BF16)
    for c in range(d // col_chunk):
        cols = slice(c * col_chunk, (c + 1) * col_chunk)
        xs_ref[:, cols] = jnp.dot(p, h[:, cols], preferred_element_type=F32).astype(BF16)
    in_group = []
    for j in range(EXPERTS_PER_GROUP):
        cj = comb[j]
        for g in range(1, N_GROUPS):
            cj = cj + comb[g * EXPERTS_PER_GROUP + j]
        in_group.append(cj)
    c_rows = jnp.concatenate(in_group + [jnp.zeros((LANES - EXPERTS_PER_GROUP, tm), F32)], axis=0)
    c_cols = jnp.transpose(c_rows)
    hi = c_cols.astype(BF16)
    lo = (c_cols - hi.astype(F32)).astype(BF16)
    cs = jnp.dot(p, jnp.concatenate([hi, lo], axis=1), preferred_element_type=F32)
    cs_ref[...] = cs[:, :LANES] + cs[:, LANES:]


def _moe_sort_call(x, shift, scale, w_router_t, router_bias, *, tm):
    bsz, s, d = x.shape
    nt = s // tm
    n_tiles = bsz * nt
    rows = _sort_rows(tm)
    tile = lambda b, i: b * nt + i
    vec = lambda: pl.BlockSpec((1, 1, d), lambda b, i: (b, 0, 0))
    return pl.pallas_call(
        _moe_sort_kernel,
        grid=(bsz, nt),
        in_specs=[
            pl.BlockSpec((1, tm, d), lambda b, i: (b, i, 0)),
            vec(), vec(),
            pl.BlockSpec(w_router_t.shape, lambda b, i: (0, 0)),
            pl.BlockSpec(router_bias.shape, lambda b, i: (0, 0)),
        ],
        out_specs=[
            pl.BlockSpec((rows, d), lambda b, i: (tile(b, i), 0)),
            pl.BlockSpec((rows, LANES), lambda b, i: (tile(b, i), 0)),
            pl.BlockSpec((1, 1, tm), lambda b, i: (tile(b, i), 0, 0)),
            pl.BlockSpec((1, 2 * N_GROUPS, LANES), lambda b, i: (tile(b, i), 0, 0)),
        ],
        out_shape=[
            jax.ShapeDtypeStruct(((n_tiles + 1) * rows, d), BF16),
            jax.ShapeDtypeStruct(((n_tiles + 1) * rows, LANES), F32),
            jax.ShapeDtypeStruct((n_tiles, 1, tm), jnp.int32),
            jax.ShapeDtypeStruct((n_tiles, 2 * N_GROUPS, LANES), jnp.int32),
        ],
        compiler_params=_cparams(("arbitrary", "arbitrary")),
        name="moe_sort",
    )(x, shift, scale, w_router_t, router_bias)


def _expert_work_table(seg, rows):
    n_tiles = seg.shape[0]
    per_tile = rows // ROW_BLOCK
    spare = n_tiles * per_tile
    max_items = spare // 2 + N_GROUPS
    first = seg[:, :N_GROUPS, 0]
    count = seg[:, N_GROUPS:, 0]
    tri = jnp.arange(n_tiles)[:, None] >= jnp.arange(n_tiles)[None, :]
    cum = jnp.sum(jnp.where(tri[:, :, None], count[None, :, :], 0), axis=1)
    total = cum[-1]
    items = (total + 1) // 2
    ends = jnp.sum(jnp.where(jnp.arange(N_GROUPS)[:, None] >= jnp.arange(N_GROUPS)[None, :], items[None, :], 0),
                   axis=1)
    n_items = ends[-1]
    step = jnp.arange(max_items + 1)
    live = step < n_items
    ref_step = jnp.where(live, step, jnp.maximum(n_items - 1, 0))
    grp = jnp.sum((ref_step[:, None] >= ends[None, :-1]).astype(jnp.int32), axis=1)
    pick = grp[:, None] == jnp.arange(N_GROUPS)[None, :]
    of_group = lambda v: jnp.sum(jnp.where(pick, v[None, :], 0), axis=1)
    start = of_group(ends - items)
    cum_s = jnp.sum(jnp.where(pick[:, None, :], cum[None, :, :], 0), axis=2)
    count_s = jnp.sum(jnp.where(pick[:, None, :], count[None, :, :], 0), axis=2)
    first_s = jnp.sum(jnp.where(pick[:, None, :], first[None, :, :], 0), axis=2)
    total_s = of_group(total)

    def block_id(k):
        done = cum_s <= k[:, None]
        t = jnp.sum(done.astype(jnp.int32), axis=1)
        skipped = jnp.sum(jnp.where(done, count_s, 0), axis=1)
        at_t = jnp.arange(n_tiles)[None, :] == t[:, None]
        return t * per_tile + jnp.sum(jnp.where(at_t, first_s, 0), axis=1) + (k - skipped)

    k_a = 2 * (step - start)
    has_b = live & (k_a + 1 < total_s)
    blk_a = jnp.where(live, block_id(k_a), spare)
    blk_b = jnp.where(has_b, block_id(k_a + 1), blk_a)
    block = jnp.arange(spare)[:, None]
    hit_a = live[None, :] & (blk_a[None, :] == block)
    hit_b = has_b[None, :] & (blk_b[None, :] == block)
    where = (jnp.sum(jnp.where(hit_a, 2 * step[None, :], 0) + jnp.where(hit_b, 2 * step[None, :] + 1, 0), axis=1)
             + jnp.where(jnp.any(hit_a | hit_b, axis=1), 0, 2 * max_items))
    as_i32 = lambda v: v.astype(jnp.int32)
    return (as_i32(blk_a), as_i32(blk_b), as_i32(grp), as_i32(n_items).reshape(1)), as_i32(where)


def _moe_expert_kernel(blk_a, blk_b, grp, n_items, xa_ref, xb_ref, ca_ref, cb_ref, wgu_ref, wd_ref, y_ref):
    f = wd_ref.shape[1] // EXPERTS_PER_GROUP

    @pl.when(pl.program_id(0) < n_items[0])
    def _():
        x = jnp.concatenate([xa_ref[...], xb_ref[...]], axis=0)
        c = jnp.concatenate([ca_ref[...], cb_ref[...]], axis=0)
        gu = jnp.dot(x, wgu_ref[0], preferred_element_type=F32)
        pieces = []
        for j in range(EXPERTS_PER_GROUP):
            gate = gu[:, j * f:(j + 1) * f]
            up = gu[:, (EXPERTS_PER_GROUP + j) * f:(EXPERTS_PER_GROUP + j + 1) * f]
            pieces.append((jax.nn.silu(gate) * up * c[:, j:j + 1]).astype(BF16))
        y = jnp.dot(jnp.concatenate(pieces, axis=1), wd_ref[0], preferred_element_type=F32)
        y_ref[...] = y.astype(BF16)

    @pl.when(pl.program_id(0) >= n_items[0])
    def _():
        y_ref[...] = jnp.zeros(y_ref.shape, BF16)


def _moe_expert_call(table, xs, cs, w_gate_up, w_down):
    d = xs.shape[1]
    steps = table[0].shape[0]
    resident = pl.Buffered(1)
    grid_spec = pltpu.PrefetchScalarGridSpec(
        num_scalar_prefetch=4,
        grid=(steps,),
        in_specs=[
            pl.BlockSpec((ROW_BLOCK, d), lambda i, a, b, g, n: (a[i], 0)),
            pl.BlockSpec((ROW_BLOCK, d), lambda i, a, b, g, n: (b[i], 0)),
            pl.BlockSpec((ROW_BLOCK, LANES), lambda i, a, b, g, n: (a[i], 0)),
            pl.BlockSpec((ROW_BLOCK, LANES), lambda i, a, b, g, n: (b[i], 0)),
            pl.BlockSpec((1,) + w_gate_up.shape[1:], lambda i, a, b, g, n: (g[i], 0, 0), pipeline_mode=resident),
            pl.BlockSpec((1,) + w_down.shape[1:], lambda i, a, b, g, n: (g[i], 0, 0), pipeline_mode=resident),
        ],
        out_specs=pl.BlockSpec((2 * ROW_BLOCK, d), lambda i, a, b, g, n: (i, 0)),
    )
    return pl.pallas_call(
        _moe_expert_kernel,
        grid_spec=grid_spec,
        out_shape=jax.ShapeDtypeStruct((steps * 2 * ROW_BLOCK, d), BF16),
        compiler_params=_cparams(("arbitrary",)),
        name="moe_experts",
    )(*table, xs, xs, cs, cs, w_gate_up, w_down)


def _moe_combine_kernel(where_ref, dest_ref, *refs, alpha, n_blocks):
    y_refs = refs[:n_blocks]
    x_ref, gate_ref, g_ref, b_ref, o_ref = refs[n_blocks:]
    tm = x_ref.shape[1]
    ys = jnp.concatenate([r[...] for r in y_refs], axis=0)
    row_id = lax.broadcasted_iota(jnp.int32, (ys.shape[0], tm), 0)
    p = jnp.where(row_id == dest_ref[0], 1.0, 0.0).astype(BF16)
    y = lax.dot_general(p, ys, (((0,), (0,)), ((), ())), preferred_element_type=F32)
    r = alpha * x_ref[0] + gate_ref[0] * y
    o_ref[0] = _layer_norm(r, g_ref[...], b_ref[...])


def _moe_combine_call(where, dest, ys, x, gate, ln_g, ln_b, *, alpha, sort_tm, tm):
    bsz, s, d = x.shape
    nt = s // sort_tm
    parts = sort_tm // tm
    n_blocks = _sort_rows(sort_tm) // ROW_BLOCK
    tile = lambda b, i: b * nt + i
    y_spec = lambda j: pl.BlockSpec((ROW_BLOCK, d), lambda b, i, k, w: (w[tile(b, i) * n_blocks + j], 0))
    grid_spec = pltpu.PrefetchScalarGridSpec(
        num_scalar_prefetch=1,
        grid=(bsz, nt, parts),
        in_specs=[pl.BlockSpec((1, 1, tm), lambda b, i, k, w: (tile(b, i), 0, k))]
        + [y_spec(j) for j in range(n_blocks)]
        + [
            pl.BlockSpec((1, tm, d), lambda b, i, k, w: (b, i * parts + k, 0)),
            pl.BlockSpec((1, 1, d), lambda b, i, k, w: (b, 0, 0)),
            pl.BlockSpec((1, d), lambda b, i, k, w: (0, 0)),
            pl.BlockSpec((1, d), lambda b, i, k, w: (0, 0)),
        ],
        out_specs=pl.BlockSpec((1, tm, d), lambda b, i, k, w: (b, i * parts + k, 0)),
    )
    return pl.pallas_call(
        functools.partial(_moe_combine_kernel, alpha=alpha, n_blocks=n_blocks),
        grid_spec=grid_spec,
        out_shape=jax.ShapeDtypeStruct((bsz, s, d), F32),
        compiler_params=_cparams(("arbitrary", "arbitrary", "arbitrary")),
        name="moe_combine",
    )(where, dest, *([ys] * n_blocks), x, gate, ln_g, ln_b)


def _moe(x, shift, scale, gate, w_router_t, router_bias, w_gate_up, w_down, ln_g, ln_b, *, alpha):
    sort_tm = min(SORT_TM, x.shape[1])
    xs, cs, dest, seg = _moe_sort_call(x, shift, scale, w_router_t, router_bias, tm=sort_tm)
    table, where = _expert_work_table(seg, _sort_rows(sort_tm))
    ys = _moe_expert_call(table, xs, cs, w_gate_up, w_down)
    return _moe_combine_call(where, dest, ys, x, gate, ln_g, ln_b, alpha=alpha, sort_tm=sort_tm,
                             tm=min(ROW_TM, sort_tm))


def _group_expert_weights(w_gate, w_up, w_down):
    e, d, f = w_gate.shape
    side = lambda w: w.reshape(N_GROUPS, EXPERTS_PER_GROUP, d, f).transpose(0, 2, 1, 3).reshape(
        N_GROUPS, d, EXPERTS_PER_GROUP * f)
    w_gate_up = jnp.concatenate([side(w_gate), side(w_up)], axis=2).astype(BF16)
    return w_gate_up, w_down.reshape(N_GROUPS, EXPERTS_PER_GROUP * f, d).astype(BF16)


def _rope_tables(n):
    freqs = LANES // 8
    inv = jnp.power(ROPE_BASE, -jnp.arange(freqs, dtype=F32) / freqs)
    tok = jnp.arange(n)
    ang_r = (tok // GRID_W).astype(F32)[:, None] * inv
    ang_c = (tok % GRID_W).astype(F32)[:, None] * inv
    ang = jnp.concatenate([ang_r, ang_c], axis=1)
    cos = jnp.tile(jnp.cos(ang), (1, 4))
    sin = jnp.tile(jnp.sin(ang), (1, 4))
    sign = jnp.where(jnp.arange(LANES) < LANES // 2, -1.0, 1.0).astype(F32)
    return cos, sin * sign


def _rope_column_perm():
    perm = np.zeros((DA_HEADS, 2, 2, 2, 16), np.int32)
    for h in range(DA_HEADS):
        for p in range(2):
            for m in range(2):
                for ax in range(2):
                    for f in range(16):
                        perm[h, p, m, ax, f] = h * LANES + m * 64 + ax * 32 + p * 16 + f
    return perm.reshape(-1)


def kernel(x, c, ctx, c_ctx, w_mod, b_mod, ln_g, ln_b, w_in_e, conv_w, lambda_q1, lambda_k1, lambda_q2, lambda_k2,
           subln_g, w_out_e, w_in_o, v_ln_g, v_ln_b, w_spatial, b_spatial, w_out_o, w_router, router_bias,
           w_gate, w_up, w_down):
    bsz, s, d = x.shape
    depth = w_mod.shape[0]
    assert depth == 2 and w_in_e.shape[0] == 1 and w_in_o.shape[0] == 1, "two-layer (even, odd) stack only"
    conv_dim = conv_w.shape[-1]
    q_dim = DA_HEADS * LANES
    q_col, k_col, v_col = 3 * conv_dim, 3 * conv_dim + q_dim, 3 * conv_dim + 2 * q_dim
    assert w_in_e.shape[2] == v_col + q_dim and s % GRID_W == 0
    alpha = float((2 * depth) ** 0.25)

    rows = 8 * ((bsz + 1 + 7) // 8)
    cond = jnp.zeros((rows, d), F32).at[:bsz].set(c).at[bsz].set(c_ctx)
    mods = _mod_call(cond, w_mod, b_mod)

    def mod_vec(l, k, ctx_row=False):
        v = mods[l, bsz:bsz + 1, k * d:(k + 1) * d] if ctx_row else mods[l, :bsz, k * d:(k + 1) * d]
        return v.reshape(-1, 1, d)

    w_router_t = w_router.T.astype(BF16)
    rbias = router_bias.reshape(-1, 1).astype(F32)

    perm = _rope_column_perm()
    w_in = w_in_e[0]
    w_in = jnp.concatenate(
        [w_in[:, :q_col], w_in[:, q_col:k_col][:, perm], w_in[:, k_col:v_col][:, perm], w_in[:, v_col:]],
        axis=1).astype(BF16)
    cos_t, sin_t = _rope_tables(s)
    tn = q_dim
    assert q_col % tn == 0
    q_tile, k_tile = q_col // tn, k_col // tn
    conv_proj, q, k, v = _proj_call(
        x, mod_vec(0, 0), mod_vec(0, 1), w_in, tm=PROJ_TM, tn=tn, n_flat=q_tile,
        rope=(cos_t, sin_t, (q_tile, k_tile), q_tile, float(64 ** -0.5 * math.log2(math.e))))
    kc, vc = _proj_call(ctx, mod_vec(0, 0, True), mod_vec(0, 1, True), w_in[:, k_col:], tm=PROJ_TM, tn=tn, n_flat=0)

    lam_init = 0.8 - 0.6 * math.exp(-0.3 * 0)
    lam_params = jnp.zeros((8, LANES), F32)
    for r, p in enumerate((lambda_q1, lambda_k1, lambda_q2, lambda_k2)):
        lam_params = lam_params.at[r, :p.shape[-1]].set(p[0].astype(F32))
    attn = _attn_call(q, k, v, kc, vc, lam_params, subln_g[0].reshape(1, LANES), lam_init=lam_init, tq=ATTN_TQ)
    conv = _conv_call(conv_proj, conv_w[0], width=conv_dim, tm=ROW_TM)
    x = _out_ln_call(conv, attn, 0, 0, w_out_e[0].astype(BF16), x, mod_vec(0, 2), ln_g[0, 0:1], ln_b[0, 0:1],
                     alpha=alpha, tm=ROW_TM)
    x = _moe(x, mod_vec(0, 3), mod_vec(0, 4), mod_vec(0, 5), w_router_t, rbias,
             *_group_expert_weights(w_gate[0], w_up[0], w_down[0]), ln_g[0, 1:2], ln_b[0, 1:2], alpha=alpha)

    w_gmlp = w_in_o[0].astype(BF16)
    uv, = _proj_call(x, mod_vec(1, 0), mod_vec(1, 1), w_gmlp, tm=PROJ_TM, tn=tn, n_flat=w_gmlp.shape[1] // tn,
                     gelu=True)
    gated = _spatial_call(uv, v_ln_g[0:1], v_ln_b[0:1], w_spatial[0].astype(BF16), b_spatial[0].T, tm=2 * CHUNK)
    x = _out_ln_call(gated, gated, 0, 1, w_out_o[0].astype(BF16), x, mod_vec(1, 2), ln_g[1, 0:1], ln_b[1, 0:1],
                     alpha=alpha, tm=ROW_TM)
    x = _moe(x, mod_vec(1, 3), mod_vec(1, 4), mod_vec(1, 5), w_router_t, rbias,
             *_group_expert_weights(w_gate[1], w_up[1], w_down[1]), ln_g[1, 1:2], ln_b[1, 1:2], alpha=alpha)
    return x
```

```python
import functools
import math

import numpy as np
import jax
import jax.numpy as jnp
from jax import lax
from jax.experimental import pallas as pl
from jax.experimental.pallas import tpu as pltpu

F32 = jnp.float32
BF16 = jnp.bfloat16

GRID_W = 64
DA_HEADS = 8
N_EXPERTS = 16
N_GROUPS = 4
EXPERTS_PER_GROUP = N_EXPERTS // N_GROUPS
ROPE_BASE = 10000.0
LN_EPS = 1e-5
CHUNK = 128
GMLP_GROUPS = 16
LANES = 128
MXU_DEPTH = 256

VMEM_LIMIT = 56 * 1024 * 1024
PROJ_TM = 1024
ROW_TM = 512
ATTN_TQ = 1024


def _cparams(sem):
    return pltpu.CompilerParams(dimension_semantics=sem, vmem_limit_bytes=VMEM_LIMIT)


def _layer_norm(r, g, b):
    mu = jnp.mean(r, axis=-1, keepdims=True)
    d = r - mu
    var = jnp.mean(d * d, axis=-1, keepdims=True)
    return d * lax.rsqrt(var + LN_EPS) * g + b


def _mod_kernel(c_ref, w_ref, b_ref, o_ref):
    c = c_ref[...]
    s = (c * jax.nn.sigmoid(c)).astype(BF16)
    o_ref[0] = jnp.dot(s, w_ref[0].astype(BF16), preferred_element_type=F32) + b_ref[0]


def _mod_call(cond, w_mod, b_mod):
    depth, d, n = w_mod.shape
    rows = cond.shape[0]
    tn = 1024
    return pl.pallas_call(
        _mod_kernel,
        grid=(depth, n // tn),
        in_specs=[
            pl.BlockSpec((rows, d), lambda l, j: (0, 0)),
            pl.BlockSpec((1, d, tn), lambda l, j: (l, 0, j)),
            pl.BlockSpec((1, 1, tn), lambda l, j: (l, 0, j)),
        ],
        out_specs=pl.BlockSpec((1, rows, tn), lambda l, j: (l, 0, j)),
        out_shape=jax.ShapeDtypeStruct((depth, rows, n), F32),
        compiler_params=_cparams(("arbitrary", "arbitrary")),
        name="adaln_mod",
    )(cond, w_mod, b_mod.reshape(depth, 1, n))


def _proj_kernel(*refs, n_flat, n_head_tiles, rope, gelu):
    x_ref, sh_ref, sc_ref, w_ref = refs[:4]
    pos = 4
    if rope is not None:
        cos_ref, sin_ref = refs[pos:pos + 2]
        pos += 2
    flat_ref = refs[pos] if n_flat else None
    pos += 1 if n_flat else 0
    head_refs = refs[pos:pos + n_head_tiles]
    h_ref = refs[pos + n_head_tiles]
    j = pl.program_id(2)

    @pl.when(j == 0)
    def _():
        h_ref[...] = (x_ref[0] * (1.0 + sc_ref[0]) + sh_ref[0]).astype(BF16)

    tn = w_ref.shape[1]
    halves = [slice(c * (tn // 2), (c + 1) * (tn // 2)) for c in range(2)]

    def product(cols):
        return jnp.dot(h_ref[...], w_ref[:, cols], preferred_element_type=F32)

    if n_flat:
        @pl.when(j < n_flat)
        def _():
            for cols in halves:
                acc = product(cols)
                flat_ref[0, :, cols] = (jax.nn.gelu(acc, approximate=True) if gelu else acc).astype(flat_ref.dtype)

    for t in range(n_head_tiles):
        tile = n_flat + t
        o_ref = head_refs[t]

        @pl.when(j == tile)
        def _(tile=tile, o_ref=o_ref):
            rotary = rope is not None and tile in rope[0]
            if rotary:
                scale = rope[2] if tile == rope[1] else 1.0
                cs = cos_ref[...] * scale
                sn = sin_ref[...] * scale
            for cols in halves:
                acc = product(cols)
                for h in range(acc.shape[1] // LANES):
                    piece = acc[:, h * LANES:(h + 1) * LANES]
                    if rotary:
                        piece = piece * cs + pltpu.roll(piece, LANES // 2, axis=1) * sn
                    o_ref[0, cols.start // LANES + h] = piece.astype(o_ref.dtype)


def _proj_call(x, shift, scale, w, *, tm, tn, n_flat, rope=None, gelu=False):
    bsz, s, d = x.shape
    n_tiles = w.shape[1] // tn
    n_head_tiles = n_tiles - n_flat
    heads = tn // LANES
    tm = min(tm, s)
    per_batch = shift.shape[0] > 1
    mod_map = (lambda b, i, j: (b, 0, 0)) if per_batch else (lambda b, i, j: (0, 0, 0))
    in_specs = [
        pl.BlockSpec((1, tm, d), lambda b, i, j: (b, i, 0)),
        pl.BlockSpec((1, 1, d), mod_map),
        pl.BlockSpec((1, 1, d), mod_map),
        pl.BlockSpec((d, tn), lambda b, i, j: (0, j)),
    ]
    args = [x, shift, scale, w]
    rope_static = None
    if rope is not None:
        cos_t, sin_t, rope_tiles, q_tile, q_scale = rope
        in_specs += [pl.BlockSpec((tm, LANES), lambda b, i, j: (i, 0))] * 2
        args += [cos_t, sin_t]
        rope_static = (tuple(rope_tiles), q_tile, q_scale)
    out_specs, out_shape = [], []
    if n_flat:
        out_specs.append(pl.BlockSpec((1, tm, tn), lambda b, i, j: (b, i, jnp.minimum(j, n_flat - 1))))
        out_shape.append(jax.ShapeDtypeStruct((bsz, s, n_flat * tn), BF16))
    for _ in range(n_head_tiles):
        out_specs.append(pl.BlockSpec((1, heads, tm, LANES), lambda b, i, j: (b, 0, i, 0)))
        out_shape.append(jax.ShapeDtypeStruct((bsz, heads, s, LANES), BF16))
    return pl.pallas_call(
        functools.partial(_proj_kernel, n_flat=n_flat, n_head_tiles=n_head_tiles, rope=rope_static, gelu=gelu),
        grid=(bsz, s // tm, n_tiles),
        in_specs=in_specs,
        out_specs=out_specs,
        out_shape=out_shape,
        scratch_shapes=[pltpu.VMEM((tm, d), BF16)],
        compiler_params=_cparams(("arbitrary", "arbitrary", "arbitrary")),
        name="mod_proj",
    )(*args)


ONES_ROWS = 16


def _attn_kernel(lam_ref, g_ref, q_ref, kc_ref, vc_ref, k_ref, v_ref, o_ref, kf_ref, vt_ref, s_ref, p_ref, acc_ref,
                 *, tk, lam_init):
    tq = q_ref.shape[2]
    c_len = kc_ref.shape[2]
    s_len = k_ref.shape[2]
    n_chunks = (c_len + s_len) // tk
    vt_chunk = 512

    @pl.when(pl.program_id(2) == 0)
    def _():
        kf_ref[:c_len, :] = kc_ref[0, 0]
        kf_ref[c_len:, :] = k_ref[0, 0]
        vt_ref[LANES:, :] = jnp.ones((ONES_ROWS, vt_ref.shape[1]), BF16)
        vt_ref[:LANES, :c_len] = jnp.transpose(vc_ref[0, 0].astype(F32)).astype(BF16)
        for n in range(s_len // vt_chunk):
            rows = slice(n * vt_chunk, (n + 1) * vt_chunk)
            cols = slice(c_len + n * vt_chunk, c_len + (n + 1) * vt_chunk)
            vt_ref[:LANES, cols] = jnp.transpose(v_ref[0, 0, rows, :].astype(F32)).astype(BF16)

    q_t = jnp.transpose(q_ref[0, 0].astype(F32))
    dim = lax.broadcasted_iota(jnp.int32, q_t.shape, 0)
    first_map = (dim % (LANES // 2)) < (LANES // 4)
    qs_t = jnp.concatenate([jnp.where(first_map, q_t, 0.0), jnp.where(first_map, 0.0, q_t)], axis=1).astype(BF16)

    acc_ref[...] = jnp.zeros(acc_ref.shape, F32)

    def scores(t):
        off = t * tk if isinstance(t, int) else pl.multiple_of(t * tk, tk)
        return jnp.dot(kf_ref[pl.ds(off, tk), :], qs_t, preferred_element_type=F32)

    def softmax(slot, m_old):
        s_t = s_ref[slot]
        m_new = jnp.maximum(m_old, jnp.max(s_t, axis=0, keepdims=True))
        p_ref[slot] = jnp.exp2((s_t - m_new).astype(BF16))
        return m_new, jnp.exp2(m_old - m_new)

    def values(t, slot, alpha):
        off = t * tk if isinstance(t, int) else pl.multiple_of(t * tk, tk)
        pv = jnp.dot(vt_ref[:, pl.ds(off, tk)], p_ref[slot], preferred_element_type=F32)
        acc_ref[...] = alpha * acc_ref[...] + pv

    def tick(t, parity, m, alpha_prev):
        s_ref[parity] = scores(t)
        m, alpha = softmax(1 - parity, m)
        values(t - 2, parity, alpha_prev)
        return m, alpha

    s_ref[0] = scores(0)
    s_ref[1] = scores(1)
    m, alpha = softmax(0, jnp.full((1, 2 * tq), -jnp.inf, F32))

    def pair(jj, carry):
        m, alpha = tick(2 + 2 * jj, 0, *carry)
        return tick(3 + 2 * jj, 1, m, alpha)

    n_full = n_chunks - 2
    m, alpha = lax.fori_loop(0, n_full // 2, pair, (m, alpha))
    if n_full % 2:
        m, alpha = tick(n_chunks - 1, (n_chunks - 1) % 2, m, alpha)
    last = (n_chunks - 1) % 2
    m, alpha_last = softmax(last, m)
    values(n_chunks - 2, 1 - last, alpha)
    values(n_chunks - 1, last, alpha_last)

    lp = lam_ref[...]
    lam = (jnp.exp(jnp.sum(lp[0:1] * lp[1:2], axis=-1, keepdims=True))
           - jnp.exp(jnp.sum(lp[2:3] * lp[3:4], axis=-1, keepdims=True)) + lam_init)
    acc = acc_ref[...]
    o_t = acc[:LANES] / acc[LANES:LANES + 1]
    o_t = o_t[:, :tq] - lam * o_t[:, tq:]
    o_t = o_t * lax.rsqrt(jnp.mean(o_t * o_t, axis=0, keepdims=True) + LN_EPS)
    o_ref[0, 0] = (jnp.transpose(o_t) * g_ref[...] * (1.0 - lam_init)).astype(o_ref.dtype)


def _attn_call(q, k, v, kc, vc, lam_params, subln_g, *, lam_init, tq):
    bsz, h, s, _ = q.shape
    c_len = kc.shape[2]
    tq = min(tq, s)
    tk = max(t for t in (MXU_DEPTH, 2 * MXU_DEPTH, 3 * MXU_DEPTH) if (c_len + s) % t == 0)
    assert (c_len + s) // tk >= 3 and s % 512 == 0
    whole = lambda n: pl.BlockSpec((1, 1, n, LANES), lambda b, hh, i: (b, hh, 0, 0))
    return pl.pallas_call(
        functools.partial(_attn_kernel, tk=tk, lam_init=lam_init),
        grid=(bsz, h, s // tq),
        in_specs=[
            pl.BlockSpec((8, LANES), lambda b, hh, i: (0, 0)),
            pl.BlockSpec((1, LANES), lambda b, hh, i: (0, 0)),
            pl.BlockSpec((1, 1, tq, LANES), lambda b, hh, i: (b, hh, i, 0)),
            whole(c_len), whole(c_len), whole(s), whole(s),
        ],
        out_specs=pl.BlockSpec((1, 1, tq, LANES), lambda b, hh, i: (b, hh, i, 0)),
        out_shape=jax.ShapeDtypeStruct((bsz, h, s, LANES), BF16),
        scratch_shapes=[
            pltpu.VMEM((c_len + s, LANES), BF16),
            pltpu.VMEM((LANES + ONES_ROWS, c_len + s), BF16),
            pltpu.VMEM((2, tk, 2 * tq), F32),
            pltpu.VMEM((2, tk, 2 * tq), BF16),
            pltpu.VMEM((LANES + ONES_ROWS, 2 * tq), F32),
        ],
        compiler_params=_cparams(("arbitrary", "arbitrary", "arbitrary")),
        name="diff_attn",
    )(lam_params, subln_g, q, kc, vc, k, v)


def _conv_kernel(xa_ref, bg_ref, cg_ref, xap_ref, cgp_ref, xan_ref, cgn_ref, w_ref, o_ref):
    i = pl.program_id(1)
    tm = xa_ref.shape[1]
    halo = xap_ref.shape[1]
    z = xa_ref[0].astype(F32) * cg_ref[0].astype(F32)
    z_before = (xap_ref[0].astype(F32) * cgp_ref[0].astype(F32))[halo - 1:halo]
    z_after = (xan_ref[0].astype(F32) * cgn_ref[0].astype(F32))[0:1]
    z_before = jnp.where(i == 0, 0.0, z_before)
    z_after = jnp.where(i == pl.num_programs(1) - 1, 0.0, z_after)
    row = lax.broadcasted_iota(jnp.int32, z.shape, 0)
    z_prev = jnp.where(row == 0, z_before, pltpu.roll(z, 1, axis=0))
    z_next = jnp.where(row == tm - 1, z_after, pltpu.roll(z, tm - 1, axis=0))
    w = w_ref[...]
    conv = w[0:1] * z_prev + w[1:2] * z + w[2:3] * z_next
    o_ref[0] = (bg_ref[0].astype(F32) * conv).astype(o_ref.dtype)


def _conv_call(proj, conv_w, *, width, tm):
    bsz, s, _ = proj.shape
    halo = 16
    nh = tm // halo
    last = s // halo - 1
    main = lambda c: pl.BlockSpec((1, tm, width), lambda b, i: (b, i, c))
    prev = lambda c: pl.BlockSpec((1, halo, width), lambda b, i: (b, jnp.maximum(i * nh - 1, 0), c))
    nxt = lambda c: pl.BlockSpec((1, halo, width), lambda b, i: (b, jnp.minimum((i + 1) * nh, last), c))
    w_pad = jnp.zeros((8, width), F32).at[:conv_w.shape[0]].set(conv_w)
    return pl.pallas_call(
        _conv_kernel,
        grid=(bsz, s // tm),
        in_specs=[main(0), main(1), main(2), prev(0), prev(2), nxt(0), nxt(2),
                  pl.BlockSpec((8, width), lambda b, i: (0, 0))],
        out_specs=pl.BlockSpec((1, tm, width), lambda b, i: (b, i, 0)),
        out_shape=jax.ShapeDtypeStruct((bsz, s, width), BF16),
        compiler_params=_cparams(("arbitrary", "arbitrary")),
        name="gated_conv",
    )(proj, proj, proj, proj, proj, proj, proj, w_pad)


def _spatial_kernel(u_ref, v_ref, g_ref, b_ref, ws_ref, bs_ref, o_ref):
    tm = u_ref.shape[1]
    v = _layer_norm(v_ref[0].astype(F32), g_ref[...], b_ref[...]).astype(BF16)
    bs = bs_ref[...]
    for n in range(tm // CHUNK):
        rows = slice(n * CHUNK, (n + 1) * CHUNK)
        for g in range(GMLP_GROUPS):
            cols = slice(g * LANES, (g + 1) * LANES)
            sg = jnp.dot(ws_ref[g], v[rows, cols], preferred_element_type=F32) + bs[:, g:g + 1]
            o_ref[0, rows, cols] = (u_ref[0, rows, cols].astype(F32) * sg).astype(o_ref.dtype)


def _spatial_call(uv, v_g, v_b, w_s, b_s_t, *, tm):
    bsz, s, two_d = uv.shape
    d = two_d // 2
    return pl.pallas_call(
        _spatial_kernel,
        grid=(bsz, s // tm),
        in_specs=[
            pl.BlockSpec((1, tm, d), lambda b, i: (b, i, 0)),
            pl.BlockSpec((1, tm, d), lambda b, i: (b, i, 1)),
            pl.BlockSpec((1, d), lambda b, i: (0, 0)),
            pl.BlockSpec((1, d), lambda b, i: (0, 0)),
            pl.BlockSpec(w_s.shape, lambda b, i: (0, 0, 0)),
            pl.BlockSpec(b_s_t.shape, lambda b, i: (0, 0)),
        ],
        out_specs=pl.BlockSpec((1, tm, d), lambda b, i: (b, i, 0)),
        out_shape=jax.ShapeDtypeStruct((bsz, s, d), BF16),
        compiler_params=_cparams(("arbitrary", "arbitrary")),
        name="gmlp_spatial",
    )(uv, uv, v_g, v_b, w_s, b_s_t)


def _out_ln_kernel(a1_ref, a2_ref, w_ref, x_ref, gate_ref, g_ref, b_ref, o_ref, *, alpha):
    if len(a2_ref.shape) == 4:
        second = [a2_ref[0, h] for h in range(a2_ref.shape[1])]
    else:
        second = [a2_ref[0]]
    a = jnp.concatenate([a1_ref[0]] + second, axis=1)
    half = a.shape[0] // 2
    for rows in (slice(0, half), slice(half, 2 * half)):
        y = jnp.dot(a[rows], w_ref[...], preferred_element_type=F32)
        r = alpha * x_ref[0, rows, :] + gate_ref[0] * y
        o_ref[0, rows, :] = _layer_norm(r, g_ref[...], b_ref[...])


def _out_ln_call(a1, a2, col1, col2, w, x, gate, ln_g, ln_b, *, alpha, tm):
    bsz, s, d = x.shape
    half = w.shape[0] // 2
    if a2.ndim == 4:
        a2_spec = pl.BlockSpec((1, a2.shape[1], tm, LANES), lambda b, i: (b, 0, i, 0))
    else:
        a2_spec = pl.BlockSpec((1, tm, half), lambda b, i: (b, i, col2))
    return pl.pallas_call(
        functools.partial(_out_ln_kernel, alpha=alpha),
        grid=(bsz, s // tm),
        in_specs=[
            pl.BlockSpec((1, tm, half), lambda b, i: (b, i, col1)),
            a2_spec,
            pl.BlockSpec(w.shape, lambda b, i: (0, 0)),
            pl.BlockSpec((1, tm, d), lambda b, i: (b, i, 0)),
            pl.BlockSpec((1, 1, d), lambda b, i: (b, 0, 0)),
            pl.BlockSpec((1, d), lambda b, i: (0, 0)),
            pl.BlockSpec((1, d), lambda b, i: (0, 0)),
        ],
        out_specs=pl.BlockSpec((1, tm, d), lambda b, i: (b, i, 0)),
        out_shape=jax.ShapeDtypeStruct((bsz, s, d), F32),
        compiler_params=_cparams(("arbitrary", "arbitrary")),
        name="out_proj_ln",
    )(a1, a2, w, x, gate, ln_g, ln_b)


def _route(scores, bias):
    sel = scores + bias
    rows = [sel[e:e + 1, :] for e in range(N_EXPERTS)]
    group_score = []
    for g in range(N_GROUPS):
        r = rows[g * EXPERTS_PER_GROUP:(g + 1) * EXPERTS_PER_GROUP]
        best = None
        for a in range(EXPERTS_PER_GROUP):
            for b in range(a + 1, EXPERTS_PER_GROUP):
                pair = r[a] + r[b]
                best = pair if best is None else jnp.maximum(best, pair)
        group_score.append(best)
    one = jnp.ones_like(rows[0])
    zero = jnp.zeros_like(rows[0])
    picked = []
    chosen = []
    for g in range(N_GROUPS):
        beaten = zero
        for o in range(N_GROUPS):
            if o < g:
                beaten = jnp.where(group_score[o] >= group_score[g], one, beaten)
            elif o > g:
                beaten = jnp.where(group_score[o] > group_score[g], one, beaten)
        chosen.append(1.0 - beaten)
        for a in range(EXPERTS_PER_GROUP):
            e = g * EXPERTS_PER_GROUP + a
            rank = zero
            for b in range(EXPERTS_PER_GROUP):
                o = g * EXPERTS_PER_GROUP + b
                if b < a:
                    rank = rank + jnp.where(rows[o] >= rows[e], one, zero)
                elif b > a:
                    rank = rank + jnp.where(rows[o] > rows[e], one, zero)
            picked.append(jnp.where(rank < 2.0, one, zero) * chosen[g])
    w = [picked[e] * scores[e:e + 1, :] for e in range(N_EXPERTS)]
    total = w[0]
    for e in range(1, N_EXPERTS):
        total = total + w[e]
    return [we / total for we in w], chosen


ROW_BLOCK = 128
SORT_TM = 1024


def _sort_rows(tm):
    return tm + N_GROUPS * ROW_BLOCK


def _moe_sort_kernel(x_ref, sh_ref, sc_ref, wr_ref, rb_ref, xs_ref, cs_ref, dest_ref, seg_ref):
    tm, d = x_ref.shape[1], x_ref.shape[2]
    rows = xs_ref.shape[0]
    col_chunk = 512
    h = (x_ref[0] * (1.0 + sc_ref[0]) + sh_ref[0]).astype(BF16)
    logits = lax.dot_general(wr_ref[...], h, (((1,), (1,)), ((), ())), preferred_element_type=F32)
    comb, chosen = _route(jax.nn.sigmoid(logits), rb_ref[...])
    src = lax.broadcasted_iota(jnp.int32, (tm, tm), 0)
    dst = lax.broadcasted_iota(jnp.int32, (tm, tm), 1)
    before = jnp.where(src < dst, 1.0, 0.0).astype(BF16)
    pad_rows = [jnp.zeros_like(chosen[0])] * (8 - N_GROUPS)
    rank = jnp.dot(jnp.concatenate(chosen + pad_rows, axis=0).astype(BF16), before, preferred_element_type=F32)
    dest = jnp.zeros_like(chosen[0])
    first = jnp.zeros((1, 1), F32)
    firsts, counts = [], []
    for g in range(N_GROUPS):
        count = jnp.sum(chosen[g], axis=1, keepdims=True)
        n_blocks = jnp.floor((count + (ROW_BLOCK - 1)) * (1.0 / ROW_BLOCK))
        firsts.append(first)
        counts.append(n_blocks)
        dest = dest + chosen[g] * (first * ROW_BLOCK + rank[g:g + 1])
        first = first + n_blocks
    seg = jnp.concatenate([jnp.broadcast_to(v, (1, LANES)) for v in firsts + counts], axis=0)
    seg_ref[0] = seg.astype(jnp.int32)
    dest_i = dest.astype(jnp.int32)
    dest_ref[0] = dest_i
    row_id = lax.broadcasted_iota(jnp.int32, (rows, tm), 0)
    p = jnp.where(row_id == dest_i, 1.0, 0.0).astype(BF16)
    for c in range(d // col_chunk):
        cols = slice(c * col_chunk, (c + 1) * col_chunk)
        xs_ref[:, cols] = jnp.dot(p, h[:, cols], preferred_element_type=F32).astype(BF16)
    in_group = []
    for j in range(EXPERTS_PER_GROUP):
        cj = comb[j]
        for g in range(1, N_GROUPS):
            cj = cj + comb[g * EXPERTS_PER_GROUP + j]
        in_group.append(cj)
    c_rows = jnp.concatenate(in_group + [jnp.zeros((LANES - EXPERTS_PER_GROUP, tm), F32)], axis=0)
    c_cols = jnp.transpose(c_rows)
    hi = c_cols.astype(BF16)
    lo = (c_cols - hi.astype(F32)).astype(BF16)
    cs = jnp.dot(p, jnp.concatenate([hi, lo], axis=1), preferred_element_type=F32)
    cs_ref[...] = cs[:, :LANES] + cs[:, LANES:]


def _moe_sort_call(x, shift, scale, w_router_t, router_bias, *, tm):
    bsz, s, d = x.shape
    nt = s // tm
    n_tiles = bsz * nt
    rows = _sort_rows(tm)
    tile = lambda b, i: b * nt + i
    vec = lambda: pl.BlockSpec((1, 1, d), lambda b, i: (b, 0, 0))
    return pl.pallas_call(
        _moe_sort_kernel,
        grid=(bsz, nt),
        in_specs=[
            pl.BlockSpec((1, tm, d), lambda b, i: (b, i, 0)),
            vec(), vec(),
            pl.BlockSpec(w_router_t.shape, lambda b, i: (0, 0)),
            pl.BlockSpec(router_bias.shape, lambda b, i: (0, 0)),
        ],
        out_specs=[
            pl.BlockSpec((rows, d), lambda b, i: (tile(b, i), 0)),
            pl.BlockSpec((rows, LANES), lambda b, i: (tile(b, i), 0)),
            pl.BlockSpec((1, 1, tm), lambda b, i: (tile(b, i), 0, 0)),
            pl.BlockSpec((1, 2 * N_GROUPS, LANES), lambda b, i: (tile(b, i), 0, 0)),
        ],
        out_shape=[
            jax.ShapeDtypeStruct(((n_tiles + 1) * rows, d), BF16),
            jax.ShapeDtypeStruct(((n_tiles + 1) * rows, LANES), F32),
            jax.ShapeDtypeStruct((n_tiles, 1, tm), jnp.int32),
            jax.ShapeDtypeStruct((n_tiles, 2 * N_GROUPS, LANES), jnp.int32),
        ],
        compiler_params=_cparams(("arbitrary", "arbitrary")),
        name="moe_sort",
    )(x, shift, scale, w_router_t, router_bias)


def _expert_work_table(seg, rows):
    n_tiles = seg.shape[0]
    per_tile = rows // ROW_BLOCK
    spare = n_tiles * per_tile
    max_items = spare // 2 + N_GROUPS
    first = seg[:, :N_GROUPS, 0]
    count = seg[:, N_GROUPS:, 0]
    tri = jnp.arange(n_tiles)[:, None] >= jnp.arange(n_tiles)[None, :]
    cum = jnp.sum(jnp.where(tri[:, :, None], count[None, :, :], 0), axis=1)
    total = cum[-1]
    items = (total + 1) // 2
    ends = jnp.sum(jnp.where(jnp.arange(N_GROUPS)[:, None] >= jnp.arange(N_GROUPS)[None, :], items[None, :], 0),
                   axis=1)
    n_items = ends[-1]
    step = jnp.arange(max_items + 1)
    live = step < n_items
    ref_step = jnp.where(live, step, jnp.maximum(n_items - 1, 0))
    grp = jnp.sum((ref_step[:, None] >= ends[None, :-1]).astype(jnp.int32), axis=1)
    pick = grp[:, None] == jnp.arange(N_GROUPS)[None, :]
    of_group = lambda v: jnp.sum(jnp.where(pick, v[None, :], 0), axis=1)
    start = of_group(ends - items)
    cum_s = jnp.sum(jnp.where(pick[:, None, :], cum[None, :, :], 0), axis=2)
    count_s = jnp.sum(jnp.where(pick[:, None, :], count[None, :, :], 0), axis=2)
    first_s = jnp.sum(jnp.where(pick[:, None, :], first[None, :, :], 0), axis=2)
    total_s = of_group(total)

    def block_id(k):
        done = cum_s <= k[:, None]
        t = jnp.sum(done.astype(jnp.int32), axis=1)
        skipped = jnp.sum(jnp.where(done, count_s, 0), axis=1)
        at_t = jnp.arange(n_tiles)[None, :] == t[:, None]
        return t * per_tile + jnp.sum(jnp.where(at_t, first_s, 0), axis=1) + (k - skipped)

    k_a = 2 * (step - start)
    has_b = live & (k_a + 1 < total_s)
    blk_a = jnp.where(live, block_id(k_a), spare)
    blk_b = jnp.where(has_b, block_id(k_a + 1), blk_a)
    block = jnp.arange(spare)[:, None]
    hit_a = live[None, :] & (blk_a[None, :] == block)
    hit_b = has_b[None, :] & (blk_b[None, :] == block)
    where = (jnp.sum(jnp.where(hit_a, 2 * step[None, :], 0) + jnp.where(hit_b, 2 * step[None, :] + 1, 0), axis=1)
             + jnp.where(jnp.any(hit_a | hit_b, axis=1), 0, 2 * max_items))
    as_i32 = lambda v: v.astype(jnp.int32)
    return (as_i32(blk_a), as_i32(blk_b), as_i32(grp), as_i32(n_items).reshape(1)), as_i32(where)


def _moe_expert_kernel(blk_a, blk_b, grp, n_items, xa_ref, xb_ref, ca_ref, cb_ref, wg_ref, wu_ref, wd_ref, y_ref):
    f, d = wd_ref.shape[2], wd_ref.shape[3]

    @pl.when(pl.program_id(0) < n_items[0])
    def _():
        x = jnp.concatenate([xa_ref[...], xb_ref[...]], axis=0)
        c = jnp.concatenate([ca_ref[...], cb_ref[...]], axis=0)
        pieces = []
        for j in range(EXPERTS_PER_GROUP):
            gate = jnp.dot(x, wg_ref[0, j], preferred_element_type=F32)
            up = jnp.dot(x, wu_ref[0, j], preferred_element_type=F32)
            pieces.append((jax.nn.silu(gate) * up * c[:, j:j + 1]).astype(BF16))
        w_down = wd_ref[0].reshape(EXPERTS_PER_GROUP * f, d)
        y = jnp.dot(jnp.concatenate(pieces, axis=1), w_down, preferred_element_type=F32)
        y_ref[...] = y.astype(BF16)

    @pl.when(pl.program_id(0) >= n_items[0])
    def _():
        y_ref[...] = jnp.zeros(y_ref.shape, BF16)


def _moe_expert_call(table, xs, cs, w_gate, w_up, w_down):
    d = xs.shape[1]
    steps = table[0].shape[0]
    weights = lambda w: pl.BlockSpec((1,) + w.shape[1:], lambda i, a, b, g, n: (g[i], 0, 0, 0),
                                     pipeline_mode=pl.Buffered(1))
    grid_spec = pltpu.PrefetchScalarGridSpec(
        num_scalar_prefetch=4,
        grid=(steps,),
        in_specs=[
            pl.BlockSpec((ROW_BLOCK, d), lambda i, a, b, g, n: (a[i], 0)),
            pl.BlockSpec((ROW_BLOCK, d), lambda i, a, b, g, n: (b[i], 0)),
            pl.BlockSpec((ROW_BLOCK, LANES), lambda i, a, b, g, n: (a[i], 0)),
            pl.BlockSpec((ROW_BLOCK, LANES), lambda i, a, b, g, n: (b[i], 0)),
            weights(w_gate), weights(w_up), weights(w_down),
        ],
        out_specs=pl.BlockSpec((2 * ROW_BLOCK, d), lambda i, a, b, g, n: (i, 0)),
    )
    return pl.pallas_call(
        _moe_expert_kernel,
        grid_spec=grid_spec,
        out_shape=jax.ShapeDtypeStruct((steps * 2 * ROW_BLOCK, d), BF16),
        compiler_params=_cparams(("arbitrary",)),
        name="moe_experts",
    )(*table, xs, xs, cs, cs, w_gate, w_up, w_down)


def _moe_combine_kernel(where_ref, dest_ref, *refs, alpha, n_blocks):
    y_refs = refs[:n_blocks]
    x_ref, gate_ref, g_ref, b_ref, o_ref = refs[n_blocks:]
    tm = x_ref.shape[1]
    ys = jnp.concatenate([r[...] for r in y_refs], axis=0)
    row_id = lax.broadcasted_iota(jnp.int32, (ys.shape[0], tm), 0)
    p = jnp.where(row_id == dest_ref[0], 1.0, 0.0).astype(BF16)
    y = lax.dot_general(p, ys, (((0,), (0,)), ((), ())), preferred_element_type=F32)
    r = alpha * x_ref[0] + gate_ref[0] * y
    o_ref[0] = _layer_norm(r, g_ref[...], b_ref[...])


def _moe_combine_call(where, dest, ys, x, gate, ln_g, ln_b, *, alpha, sort_tm, tm):
    bsz, s, d = x.shape
    nt = s // sort_tm
    parts = sort_tm // tm
    n_blocks = _sort_rows(sort_tm) // ROW_BLOCK
    tile = lambda b, i: b * nt + i
    y_spec = lambda j: pl.BlockSpec((ROW_BLOCK, d), lambda b, i, k, w: (w[tile(b, i) * n_blocks + j], 0))
    grid_spec = pltpu.PrefetchScalarGridSpec(
        num_scalar_prefetch=1,
        grid=(bsz, nt, parts),
        in_specs=[pl.BlockSpec((1, 1, tm), lambda b, i, k, w: (tile(b, i), 0, k))]
        + [y_spec(j) for j in range(n_blocks)]
        + [
            pl.BlockSpec((1, tm, d), lambda b, i, k, w: (b, i * parts + k, 0)),
            pl.BlockSpec((1, 1, d), lambda b, i, k, w: (b, 0, 0)),
            pl.BlockSpec((1, d), lambda b, i, k, w: (0, 0)),
            pl.BlockSpec((1, d), lambda b, i, k, w: (0, 0)),
        ],
        out_specs=pl.BlockSpec((1, tm, d), lambda b, i, k, w: (b, i * parts + k, 0)),
    )
    return pl.pallas_call(
        functools.partial(_moe_combine_kernel, alpha=alpha, n_blocks=n_blocks),
        grid_spec=grid_spec,
        out_shape=jax.ShapeDtypeStruct((bsz, s, d), F32),
        compiler_params=_cparams(("arbitrary", "arbitrary", "arbitrary")),
        name="moe_combine",
    )(where, dest, *([ys] * n_blocks), x, gate, ln_g, ln_b)


def _moe(x, shift, scale, gate, w_router_t, router_bias, w_gate, w_up, w_down, ln_g, ln_b, *, alpha):
    sort_tm = min(SORT_TM, x.shape[1])
    xs, cs, dest, seg = _moe_sort_call(x, shift, scale, w_router_t, router_bias, tm=sort_tm)
    table, where = _expert_work_table(seg, _sort_rows(sort_tm))
    by_group = lambda w: w.reshape((N_GROUPS, EXPERTS_PER_GROUP) + w.shape[1:]).astype(BF16)
    ys = _moe_expert_call(table, xs, cs, by_group(w_gate), by_group(w_up), by_group(w_down))
    return _moe_combine_call(where, dest, ys, x, gate, ln_g, ln_b, alpha=alpha, sort_tm=sort_tm,
                             tm=min(ROW_TM, sort_tm))


def _rope_tables(n):
    freqs = LANES // 8
    inv = jnp.power(ROPE_BASE, -jnp.arange(freqs, dtype=F32) / freqs)
    tok = jnp.arange(n)
    ang_r = (tok // GRID_W).astype(F32)[:, None] * inv
    ang_c = (tok % GRID_W).astype(F32)[:, None] * inv
    ang = jnp.concatenate([ang_r, ang_c], axis=1)
    cos = jnp.tile(jnp.cos(ang), (1, 4))
    sin = jnp.tile(jnp.sin(ang), (1, 4))
    sign = jnp.where(jnp.arange(LANES) < LANES // 2, -1.0, 1.0).astype(F32)
    return cos, sin * sign


def _rope_column_perm():
    perm = np.zeros((DA_HEADS, 2, 2, 2, 16), np.int32)
    for h in range(DA_HEADS):
        for p in range(2):
            for m in range(2):
                for ax in range(2):
                    for f in range(16):
                        perm[h, p, m, ax, f] = h * LANES + m * 64 + ax * 32 + p * 16 + f
    return perm.reshape(-1)


def kernel(x, c, ctx, c_ctx, w_mod, b_mod, ln_g, ln_b, w_in_e, conv_w, lambda_q1, lambda_k1, lambda_q2, lambda_k2,
           subln_g, w_out_e, w_in_o, v_ln_g, v_ln_b, w_spatial, b_spatial, w_out_o, w_router, router_bias,
           w_gate, w_up, w_down):
    bsz, s, d = x.shape
    depth = w_mod.shape[0]
    assert depth == 2 and w_in_e.shape[0] == 1 and w_in_o.shape[0] == 1, "two-layer (even, odd) stack only"
    conv_dim = conv_w.shape[-1]
    q_dim = DA_HEADS * LANES
    q_col, k_col, v_col = 3 * conv_dim, 3 * conv_dim + q_dim, 3 * conv_dim + 2 * q_dim
    assert w_in_e.shape[2] == v_col + q_dim and s % GRID_W == 0
    alpha = float((2 * depth) ** 0.25)

    rows = 8 * ((bsz + 1 + 7) // 8)
    cond = jnp.zeros((rows, d), F32).at[:bsz].set(c).at[bsz].set(c_ctx)
    mods = _mod_call(cond, w_mod, b_mod)

    def mod_vec(l, k, ctx_row=False):
        v = mods[l, bsz:bsz + 1, k * d:(k + 1) * d] if ctx_row else mods[l, :bsz, k * d:(k + 1) * d]
        return v.reshape(-1, 1, d)

    w_router_t = w_router.T.astype(BF16)
    rbias = router_bias.reshape(-1, 1).astype(F32)

    perm = _rope_column_perm()
    w_in = w_in_e[0]
    w_in = jnp.concatenate(
        [w_in[:, :q_col], w_in[:, q_col:k_col][:, perm], w_in[:, k_col:v_col][:, perm], w_in[:, v_col:]],
        axis=1).astype(BF16)
    cos_t, sin_t = _rope_tables(s)
    tn = q_dim
    assert q_col % tn == 0
    q_tile, k_tile = q_col // tn, k_col // tn
    conv_proj, q, k, v = _proj_call(
        x, mod_vec(0, 0), mod_vec(0, 1), w_in, tm=PROJ_TM, tn=tn, n_flat=q_tile,
        rope=(cos_t, sin_t, (q_tile, k_tile), q_tile, float(64 ** -0.5 * math.log2(math.e))))
    kc, vc = _proj_call(ctx, mod_vec(0, 0, True), mod_vec(0, 1, True), w_in[:, k_col:], tm=PROJ_TM, tn=tn, n_flat=0)

    lam_init = 0.8 - 0.6 * math.exp(-0.3 * 0)
    lam_params = jnp.zeros((8, LANES), F32)
    for r, p in enumerate((lambda_q1, lambda_k1, lambda_q2, lambda_k2)):
        lam_params = lam_params.at[r, :p.shape[-1]].set(p[0].astype(F32))
    attn = _attn_call(q, k, v, kc, vc, lam_params, subln_g[0].reshape(1, LANES), lam_init=lam_init, tq=ATTN_TQ)
    conv = _conv_call(conv_proj, conv_w[0], width=conv_dim, tm=ROW_TM)
    x = _out_ln_call(conv, attn, 0, 0, w_out_e[0].astype(BF16), x, mod_vec(0, 2), ln_g[0, 0:1], ln_b[0, 0:1],
                     alpha=alpha, tm=ROW_TM)
    x = _moe(x, mod_vec(0, 3), mod_vec(0, 4), mod_vec(0, 5), w_router_t, rbias,
             w_gate[0], w_up[0], w_down[0], ln_g[0, 1:2], ln_b[0, 1:2], alpha=alpha)

    w_gmlp = w_in_o[0].astype(BF16)
    uv, = _proj_call(x, mod_vec(1, 0), mod_vec(1, 1), w_gmlp, tm=PROJ_TM, tn=tn, n_flat=w_gmlp.shape[1] // tn,
                     gelu=True)
    gated = _spatial_call(uv, v_ln_g[0:1], v_ln_b[0:1], w_spatial[0].astype(BF16), b_spatial[0].T, tm=2 * CHUNK)
    x = _out_ln_call(gated, gated, 0, 1, w_out_o[0].astype(BF16), x, mod_vec(1, 2), ln_g[1, 0:1], ln_b[1, 0:1],
                     alpha=alpha, tm=ROW_TM)
    x = _moe(x, mod_vec(1, 3), mod_vec(1, 4), mod_vec(1, 5), w_router_t, rbias,
             w_gate[1], w_up[1], w_down[1], ln_g[1, 1:2], ln_b[1, 1:2], alpha=alpha)
    return x
```

```python
import functools
import math

import numpy as np
import jax
import jax.numpy as jnp
from jax import lax
from jax.experimental import pallas as pl
from jax.experimental.pallas import tpu as pltpu

F32 = jnp.float32
BF16 = jnp.bfloat16

GRID_W = 64
DA_HEADS = 8
N_EXPERTS = 16
N_GROUPS = 4
EXPERTS_PER_GROUP = N_EXPERTS // N_GROUPS
ROPE_BASE = 10000.0
LN_EPS = 1e-5
CHUNK = 128
GMLP_GROUPS = 16
LANES = 128
MXU_DEPTH = 256

VMEM_LIMIT = 56 * 1024 * 1024
PROJ_TM = 1024
ROW_TM = 512
ATTN_TQ = 1024


def _cparams(sem):
    return pltpu.CompilerParams(dimension_semantics=sem, vmem_limit_bytes=VMEM_LIMIT)


def _layer_norm(r, g, b):
    mu = jnp.mean(r, axis=-1, keepdims=True)
    d = r - mu
    var = jnp.mean(d * d, axis=-1, keepdims=True)
    return d * lax.rsqrt(var + LN_EPS) * g + b


def _mod_kernel(c_ref, w_ref, b_ref, o_ref):
    c = c_ref[...]
    s = (c * jax.nn.sigmoid(c)).astype(BF16)
    o_ref[0] = jnp.dot(s, w_ref[0].astype(BF16), preferred_element_type=F32) + b_ref[0]


def _mod_call(cond, w_mod, b_mod):
    depth, d, n = w_mod.shape
    rows = cond.shape[0]
    tn = 1024
    return pl.pallas_call(
        _mod_kernel,
        grid=(depth, n // tn),
        in_specs=[
            pl.BlockSpec((rows, d), lambda l, j: (0, 0)),
            pl.BlockSpec((1, d, tn), lambda l, j: (l, 0, j)),
            pl.BlockSpec((1, 1, tn), lambda l, j: (l, 0, j)),
        ],
        out_specs=pl.BlockSpec((1, rows, tn), lambda l, j: (l, 0, j)),
        out_shape=jax.ShapeDtypeStruct((depth, rows, n), F32),
        compiler_params=_cparams(("arbitrary", "arbitrary")),
        name="adaln_mod",
    )(cond, w_mod, b_mod.reshape(depth, 1, n))


def _proj_kernel(*refs, n_flat, n_head_tiles, rope, gelu):
    x_ref, sh_ref, sc_ref, w_ref = refs[:4]
    pos = 4
    if rope is not None:
        cos_ref, sin_ref = refs[pos:pos + 2]
        pos += 2
    flat_ref = refs[pos] if n_flat else None
    pos += 1 if n_flat else 0
    head_refs = refs[pos:pos + n_head_tiles]
    h_ref = refs[pos + n_head_tiles]
    j = pl.program_id(2)

    @pl.when(j == 0)
    def _():
        h_ref[...] = (x_ref[0] * (1.0 + sc_ref[0]) + sh_ref[0]).astype(BF16)

    tn = w_ref.shape[1]
    halves = [slice(c * (tn // 2), (c + 1) * (tn // 2)) for c in range(2)]

    def product(cols):
        return jnp.dot(h_ref[...], w_ref[:, cols], preferred_element_type=F32)

    if n_flat:
        @pl.when(j < n_flat)
        def _():
            for cols in halves:
                acc = product(cols)
                flat_ref[0, :, cols] = (jax.nn.gelu(acc, approximate=True) if gelu else acc).astype(flat_ref.dtype)

    for t in range(n_head_tiles):
        tile = n_flat + t
        o_ref = head_refs[t]

        @pl.when(j == tile)
        def _(tile=tile, o_ref=o_ref):
            rotary = rope is not None and tile in rope[0]
            if rotary:
                scale = rope[2] if tile == rope[1] else 1.0
                cs = cos_ref[...] * scale
                sn = sin_ref[...] * scale
            for cols in halves:
                acc = product(cols)
                for h in range(acc.shape[1] // LANES):
                    piece = acc[:, h * LANES:(h + 1) * LANES]
                    if rotary:
                        piece = piece * cs + pltpu.roll(piece, LANES // 2, axis=1) * sn
                    o_ref[0, cols.start // LANES + h] = piece.astype(o_ref.dtype)


def _proj_call(x, shift, scale, w, *, tm, tn, n_flat, rope=None, gelu=False):
    bsz, s, d = x.shape
    n_tiles = w.shape[1] // tn
    n_head_tiles = n_tiles - n_flat
    heads = tn // LANES
    tm = min(tm, s)
    per_batch = shift.shape[0] > 1
    mod_map = (lambda b, i, j: (b, 0, 0)) if per_batch else (lambda b, i, j: (0, 0, 0))
    in_specs = [
        pl.BlockSpec((1, tm, d), lambda b, i, j: (b, i, 0)),
        pl.BlockSpec((1, 1, d), mod_map),
        pl.BlockSpec((1, 1, d), mod_map),
        pl.BlockSpec((d, tn), lambda b, i, j: (0, j)),
    ]
    args = [x, shift, scale, w]
    rope_static = None
    if rope is not None:
        cos_t, sin_t, rope_tiles, q_tile, q_scale = rope
        in_specs += [pl.BlockSpec((tm, LANES), lambda b, i, j: (i, 0))] * 2
        args += [cos_t, sin_t]
        rope_static = (tuple(rope_tiles), q_tile, q_scale)
    out_specs, out_shape = [], []
    if n_flat:
        out_specs.append(pl.BlockSpec((1, tm, tn), lambda b, i, j: (b, i, jnp.minimum(j, n_flat - 1))))
        out_shape.append(jax.ShapeDtypeStruct((bsz, s, n_flat * tn), BF16))
    for _ in range(n_head_tiles):
        out_specs.append(pl.BlockSpec((1, heads, tm, LANES), lambda b, i, j: (b, 0, i, 0)))
        out_shape.append(jax.ShapeDtypeStruct((bsz, heads, s, LANES), BF16))
    return pl.pallas_call(
        functools.partial(_proj_kernel, n_flat=n_flat, n_head_tiles=n_head_tiles, rope=rope_static, gelu=gelu),
        grid=(bsz, s // tm, n_tiles),
        in_specs=in_specs,
        out_specs=out_specs,
        out_shape=out_shape,
        scratch_shapes=[pltpu.VMEM((tm, d), BF16)],
        compiler_params=_cparams(("arbitrary", "arbitrary", "arbitrary")),
        name="mod_proj",
    )(*args)


ONES_ROWS = 16


def _attn_kernel(lam_ref, g_ref, q_ref, kc_ref, vc_ref, k_ref, v_ref, o_ref, kf_ref, vt_ref, s_ref, p_ref, acc_ref,
                 *, tk, lam_init):
    tq = q_ref.shape[2]
    c_len = kc_ref.shape[2]
    s_len = k_ref.shape[2]
    n_chunks = (c_len + s_len) // tk
    vt_chunk = 512

    @pl.when(pl.program_id(2) == 0)
    def _():
        kf_ref[:c_len, :] = kc_ref[0, 0]
        kf_ref[c_len:, :] = k_ref[0, 0]
        vt_ref[LANES:, :] = jnp.ones((ONES_ROWS, vt_ref.shape[1]), BF16)
        vt_ref[:LANES, :c_len] = jnp.transpose(vc_ref[0, 0].astype(F32)).astype(BF16)
        for n in range(s_len // vt_chunk):
            rows = slice(n * vt_chunk, (n + 1) * vt_chunk)
            cols = slice(c_len + n * vt_chunk, c_len + (n + 1) * vt_chunk)
            vt_ref[:LANES, cols] = jnp.transpose(v_ref[0, 0, rows, :].astype(F32)).astype(BF16)

    q_t = jnp.transpose(q_ref[0, 0].astype(F32))
    dim = lax.broadcasted_iota(jnp.int32, q_t.shape, 0)
    first_map = (dim % (LANES // 2)) < (LANES // 4)
    qs_t = jnp.concatenate([jnp.where(first_map, q_t, 0.0), jnp.where(first_map, 0.0, q_t)], axis=1).astype(BF16)

    acc_ref[...] = jnp.zeros(acc_ref.shape, F32)

    def scores(t):
        off = t * tk if isinstance(t, int) else pl.multiple_of(t * tk, tk)
        return jnp.dot(kf_ref[pl.ds(off, tk), :], qs_t, preferred_element_type=F32)

    def softmax(slot, m_old):
        s_t = s_ref[slot]
        m_new = jnp.maximum(m_old, jnp.max(s_t, axis=0, keepdims=True))
        p_ref[slot] = jnp.exp2((s_t - m_new).astype(BF16))
        return m_new, jnp.exp2(m_old - m_new)

    def values(t, slot, alpha):
        off = t * tk if isinstance(t, int) else pl.multiple_of(t * tk, tk)
        pv = jnp.dot(vt_ref[:, pl.ds(off, tk)], p_ref[slot], preferred_element_type=F32)
        acc_ref[...] = alpha * acc_ref[...] + pv

    def tick(t, parity, m, alpha_prev):
        s_ref[parity] = scores(t)
        m, alpha = softmax(1 - parity, m)
        values(t - 2, parity, alpha_prev)
        return m, alpha

    s_ref[0] = scores(0)
    s_ref[1] = scores(1)
    m, alpha = softmax(0, jnp.full((1, 2 * tq), -jnp.inf, F32))

    def pair(jj, carry):
        m, alpha = tick(2 + 2 * jj, 0, *carry)
        return tick(3 + 2 * jj, 1, m, alpha)

    n_full = n_chunks - 2
    m, alpha = lax.fori_loop(0, n_full // 2, pair, (m, alpha))
    if n_full % 2:
        m, alpha = tick(n_chunks - 1, (n_chunks - 1) % 2, m, alpha)
    last = (n_chunks - 1) % 2
    m, alpha_last = softmax(last, m)
    values(n_chunks - 2, 1 - last, alpha)
    values(n_chunks - 1, last, alpha_last)

    lp = lam_ref[...]
    lam = (jnp.exp(jnp.sum(lp[0:1] * lp[1:2], axis=-1, keepdims=True))
           - jnp.exp(jnp.sum(lp[2:3] * lp[3:4], axis=-1, keepdims=True)) + lam_init)
    acc = acc_ref[...]
    o_t = acc[:LANES] / acc[LANES:LANES + 1]
    o_t = o_t[:, :tq] - lam * o_t[:, tq:]
    o_t = o_t * lax.rsqrt(jnp.mean(o_t * o_t, axis=0, keepdims=True) + LN_EPS)
    o_ref[0, 0] = (jnp.transpose(o_t) * g_ref[...] * (1.0 - lam_init)).astype(o_ref.dtype)


def _attn_call(q, k, v, kc, vc, lam_params, subln_g, *, lam_init, tq):
    bsz, h, s, _ = q.shape
    c_len = kc.shape[2]
    tq = min(tq, s)
    tk = max(t for t in (MXU_DEPTH, 2 * MXU_DEPTH, 3 * MXU_DEPTH) if (c_len + s) % t == 0)
    assert (c_len + s) // tk >= 3 and s % 512 == 0
    whole = lambda n: pl.BlockSpec((1, 1, n, LANES), lambda b, hh, i: (b, hh, 0, 0))
    return pl.pallas_call(
        functools.partial(_attn_kernel, tk=tk, lam_init=lam_init),
        grid=(bsz, h, s // tq),
        in_specs=[
            pl.BlockSpec((8, LANES), lambda b, hh, i: (0, 0)),
            pl.BlockSpec((1, LANES), lambda b, hh, i: (0, 0)),
            pl.BlockSpec((1, 1, tq, LANES), lambda b, hh, i: (b, hh, i, 0)),
            whole(c_len), whole(c_len), whole(s), whole(s),
        ],
        out_specs=pl.BlockSpec((1, 1, tq, LANES), lambda b, hh, i: (b, hh, i, 0)),
        out_shape=jax.ShapeDtypeStruct((bsz, h, s, LANES), BF16),
        scratch_shapes=[
            pltpu.VMEM((c_len + s, LANES), BF16),
            pltpu.VMEM((LANES + ONES_ROWS, c_len + s), BF16),
            pltpu.VMEM((2, tk, 2 * tq), F32),
            pltpu.VMEM((2, tk, 2 * tq), BF16),
            pltpu.VMEM((LANES + ONES_ROWS, 2 * tq), F32),
        ],
        compiler_params=_cparams(("arbitrary", "arbitrary", "arbitrary")),
        name="diff_attn",
    )(lam_params, subln_g, q, kc, vc, k, v)


def _gated_conv(xa_ref, bg_ref, cg_ref, xap_ref, cgp_ref, xan_ref, cgn_ref, w_ref):
    i = pl.program_id(1)
    tm = xa_ref.shape[1]
    halo = xap_ref.shape[1]
    z = xa_ref[0].astype(F32) * cg_ref[0].astype(F32)
    z_before = (xap_ref[0].astype(F32) * cgp_ref[0].astype(F32))[halo - 1:halo]
    z_after = (xan_ref[0].astype(F32) * cgn_ref[0].astype(F32))[0:1]
    z_before = jnp.where(i == 0, 0.0, z_before)
    z_after = jnp.where(i == pl.num_programs(1) - 1, 0.0, z_after)
    row = lax.broadcasted_iota(jnp.int32, z.shape, 0)
    z_prev = jnp.where(row == 0, z_before, pltpu.roll(z, 1, axis=0))
    z_next = jnp.where(row == tm - 1, z_after, pltpu.roll(z, tm - 1, axis=0))
    w = w_ref[...]
    conv = w[0:1] * z_prev + w[1:2] * z + w[2:3] * z_next
    return bg_ref[0].astype(F32) * conv


def _gated_conv_operands(proj, conv_w, *, width, tm):
    s = proj.shape[1]
    halo = 16
    nh = tm // halo
    last = s // halo - 1
    main = lambda c: pl.BlockSpec((1, tm, width), lambda b, i: (b, i, c))
    prev = lambda c: pl.BlockSpec((1, halo, width), lambda b, i: (b, jnp.maximum(i * nh - 1, 0), c))
    nxt = lambda c: pl.BlockSpec((1, halo, width), lambda b, i: (b, jnp.minimum((i + 1) * nh, last), c))
    w_pad = jnp.zeros((8, width), F32).at[:conv_w.shape[0]].set(conv_w)
    specs = [main(0), main(1), main(2), prev(0), prev(2), nxt(0), nxt(2), pl.BlockSpec((8, width), lambda b, i: (0, 0))]
    return specs, [proj] * 7 + [w_pad]


def _spatial_kernel(u_ref, v_ref, g_ref, b_ref, ws_ref, bs_ref, o_ref):
    tm = u_ref.shape[1]
    v = _layer_norm(v_ref[0].astype(F32), g_ref[...], b_ref[...]).astype(BF16)
    bs = bs_ref[...]
    for n in range(tm // CHUNK):
        rows = slice(n * CHUNK, (n + 1) * CHUNK)
        for g in range(GMLP_GROUPS):
            cols = slice(g * LANES, (g + 1) * LANES)
            sg = jnp.dot(ws_ref[g], v[rows, cols], preferred_element_type=F32) + bs[:, g:g + 1]
            o_ref[0, rows, cols] = (u_ref[0, rows, cols].astype(F32) * sg).astype(o_ref.dtype)


def _spatial_call(uv, v_g, v_b, w_s, b_s_t, *, tm):
    bsz, s, two_d = uv.shape
    d = two_d // 2
    return pl.pallas_call(
        _spatial_kernel,
        grid=(bsz, s // tm),
        in_specs=[
            pl.BlockSpec((1, tm, d), lambda b, i: (b, i, 0)),
            pl.BlockSpec((1, tm, d), lambda b, i: (b, i, 1)),
            pl.BlockSpec((1, d), lambda b, i: (0, 0)),
            pl.BlockSpec((1, d), lambda b, i: (0, 0)),
            pl.BlockSpec(w_s.shape, lambda b, i: (0, 0, 0)),
            pl.BlockSpec(b_s_t.shape, lambda b, i: (0, 0)),
        ],
        out_specs=pl.BlockSpec((1, tm, d), lambda b, i: (b, i, 0)),
        out_shape=jax.ShapeDtypeStruct((bsz, s, d), BF16),
        compiler_params=_cparams(("arbitrary", "arbitrary")),
        name="gmlp_spatial",
    )(uv, uv, v_g, v_b, w_s, b_s_t)


N_CONV_OPERANDS = 8


def _out_ln_kernel(*refs, alpha, conv):
    n_first = N_CONV_OPERANDS if conv else 1
    a2_ref, w_ref, x_ref, gate_ref, g_ref, b_ref, o_ref = refs[n_first:]
    first = _gated_conv(*refs[:n_first]).astype(BF16) if conv else refs[0][0]
    if len(a2_ref.shape) == 4:
        second = [a2_ref[0, h] for h in range(a2_ref.shape[1])]
    else:
        second = [a2_ref[0]]
    a = jnp.concatenate([first] + second, axis=1)
    half = a.shape[0] // 2
    for rows in (slice(0, half), slice(half, 2 * half)):
        y = jnp.dot(a[rows], w_ref[...], preferred_element_type=F32)
        r = alpha * x_ref[0, rows, :] + gate_ref[0] * y
        o_ref[0, rows, :] = _layer_norm(r, g_ref[...], b_ref[...])


def _out_ln_call(a1, a2, w, x, gate, ln_g, ln_b, *, alpha, tm, conv_w=None):
    bsz, s, d = x.shape
    half = w.shape[0] // 2
    if conv_w is not None:
        first_specs, first_args = _gated_conv_operands(a1, conv_w, width=half, tm=tm)
        a2_spec = pl.BlockSpec((1, a2.shape[1], tm, LANES), lambda b, i: (b, 0, i, 0))
    else:
        first_specs, first_args = [pl.BlockSpec((1, tm, half), lambda b, i: (b, i, 0))], [a1]
        a2_spec = pl.BlockSpec((1, tm, half), lambda b, i: (b, i, 1))
    return pl.pallas_call(
        functools.partial(_out_ln_kernel, alpha=alpha, conv=conv_w is not None),
        grid=(bsz, s // tm),
        in_specs=first_specs + [
            a2_spec,
            pl.BlockSpec(w.shape, lambda b, i: (0, 0)),
            pl.BlockSpec((1, tm, d), lambda b, i: (b, i, 0)),
            pl.BlockSpec((1, 1, d), lambda b, i: (b, 0, 0)),
            pl.BlockSpec((1, d), lambda b, i: (0, 0)),
            pl.BlockSpec((1, d), lambda b, i: (0, 0)),
        ],
        out_specs=pl.BlockSpec((1, tm, d), lambda b, i: (b, i, 0)),
        out_shape=jax.ShapeDtypeStruct((bsz, s, d), F32),
        compiler_params=_cparams(("arbitrary", "arbitrary")),
        name="out_proj_ln",
    )(*first_args, a2, w, x, gate, ln_g, ln_b)


def _route(scores, bias):
    sel = scores + bias
    rows = [sel[e:e + 1, :] for e in range(N_EXPERTS)]
    group_score = []
    for g in range(N_GROUPS):
        r = rows[g * EXPERTS_PER_GROUP:(g + 1) * EXPERTS_PER_GROUP]
        best = None
        for a in range(EXPERTS_PER_GROUP):
            for b in range(a + 1, EXPERTS_PER_GROUP):
                pair = r[a] + r[b]
                best = pair if best is None else jnp.maximum(best, pair)
        group_score.append(best)
    one = jnp.ones_like(rows[0])
    zero = jnp.zeros_like(rows[0])
    picked = []
    chosen = []
    for g in range(N_GROUPS):
        beaten = zero
        for o in range(N_GROUPS):
            if o < g:
                beaten = jnp.where(group_score[o] >= group_score[g], one, beaten)
            elif o > g:
                beaten = jnp.where(group_score[o] > group_score[g], one, beaten)
        chosen.append(1.0 - beaten)
        for a in range(EXPERTS_PER_GROUP):
            e = g * EXPERTS_PER_GROUP + a
            rank = zero
            for b in range(EXPERTS_PER_GROUP):
                o = g * EXPERTS_PER_GROUP + b
                if b < a:
                    rank = rank + jnp.where(rows[o] >= rows[e], one, zero)
                elif b > a:
                    rank = rank + jnp.where(rows[o] > rows[e], one, zero)
            picked.append(jnp.where(rank < 2.0, one, zero) * chosen[g])
    w = [picked[e] * scores[e:e + 1, :] for e in range(N_EXPERTS)]
    total = w[0]
    for e in range(1, N_EXPERTS):
        total = total + w[e]
    return [we / total for we in w], chosen


ROW_BLOCK = 64
STEP_BLOCKS = 4
SORT_TM = 1024


def _sort_rows(tm):
    return tm + N_GROUPS * ROW_BLOCK


def _moe_sort_kernel(x_ref, sh_ref, sc_ref, wr_ref, rb_ref, xs_ref, cs_ref, dest_ref, seg_ref):
    tm, d = x_ref.shape[1], x_ref.shape[2]
    rows = xs_ref.shape[0]
    col_chunk = 512
    h = (x_ref[0] * (1.0 + sc_ref[0]) + sh_ref[0]).astype(BF16)
    logits = lax.dot_general(wr_ref[...], h, (((1,), (1,)), ((), ())), preferred_element_type=F32)
    comb, chosen = _route(jax.nn.sigmoid(logits), rb_ref[...])
    src = lax.broadcasted_iota(jnp.int32, (tm, tm), 0)
    dst = lax.broadcasted_iota(jnp.int32, (tm, tm), 1)
    before = jnp.where(src < dst, 1.0, 0.0).astype(BF16)
    pad_rows = [jnp.zeros_like(chosen[0])] * (8 - N_GROUPS)
    rank = jnp.dot(jnp.concatenate(chosen + pad_rows, axis=0).astype(BF16), before, preferred_element_type=F32)
    dest = jnp.zeros_like(chosen[0])
    first = jnp.zeros((1, 1), F32)
    firsts, counts = [], []
    for g in range(N_GROUPS):
        count = jnp.sum(chosen[g], axis=1, keepdims=True)
        n_blocks = jnp.floor((count + (ROW_BLOCK - 1)) * (1.0 / ROW_BLOCK))
        firsts.append(first)
        counts.append(n_blocks)
        dest = dest + chosen[g] * (first * ROW_BLOCK + rank[g:g + 1])
        first = first + n_blocks
    seg = jnp.concatenate([jnp.broadcast_to(v, (1, LANES)) for v in firsts + counts], axis=0)
    seg_ref[0] = seg.astype(jnp.int32)
    dest_i = dest.astype(jnp.int32)
    dest_ref[0] = dest_i
    row_id = lax.broadcasted_iota(jnp.int32, (rows, tm), 0)
    p = jnp.where(row_id == dest_i, 1.0, 0.0).astype(BF16)
    for c in range(d // col_chunk):
        cols = slice(c * col_chunk, (c + 1) * col_chunk)
        xs_ref[:, cols] = jnp.dot(p, h[:, cols], preferred_element_type=F32).astype(BF16)
    in_group = []
    for j in range(EXPERTS_PER_GROUP):
        cj = comb[j]
        for g in range(1, N_GROUPS):
            cj = cj + comb[g * EXPERTS_PER_GROUP + j]
        in_group.append(cj)
    c_rows = jnp.concatenate(in_group + [jnp.zeros((LANES - EXPERTS_PER_GROUP, tm), F32)], axis=0)
    c_cols = jnp.transpose(c_rows)
    hi = c_cols.astype(BF16)
    lo = (c_cols - hi.astype(F32)).astype(BF16)
    cs = jnp.dot(p, jnp.concatenate([hi, lo], axis=1), preferred_element_type=F32)
    cs_ref[...] = cs[:, :LANES] + cs[:, LANES:]


def _moe_sort_call(x, shift, scale, w_router_t, router_bias, *, tm):
    bsz, s, d = x.shape
    nt = s // tm
    n_tiles = bsz * nt
    rows = _sort_rows(tm)
    tile = lambda b, i: b * nt + i
    vec = lambda: pl.BlockSpec((1, 1, d), lambda b, i: (b, 0, 0))
    return pl.pallas_call(
        _moe_sort_kernel,
        grid=(bsz, nt),
        in_specs=[
            pl.BlockSpec((1, tm, d), lambda b, i: (b, i, 0)),
            vec(), vec(),
            pl.BlockSpec(w_router_t.shape, lambda b, i: (0, 0)),
            pl.BlockSpec(router_bias.shape, lambda b, i: (0, 0)),
        ],
        out_specs=[
            pl.BlockSpec((rows, d), lambda b, i: (tile(b, i), 0)),
            pl.BlockSpec((rows, LANES), lambda b, i: (tile(b, i), 0)),
            pl.BlockSpec((1, 1, tm), lambda b, i: (tile(b, i), 0, 0)),
            pl.BlockSpec((1, 2 * N_GROUPS, LANES), lambda b, i: (tile(b, i), 0, 0)),
        ],
        out_shape=[
            jax.ShapeDtypeStruct(((n_tiles + 1) * rows, d), BF16),
            jax.ShapeDtypeStruct(((n_tiles + 1) * rows, LANES), F32),
            jax.ShapeDtypeStruct((n_tiles, 1, tm), jnp.int32),
            jax.ShapeDtypeStruct((n_tiles, 2 * N_GROUPS, LANES), jnp.int32),
        ],
        compiler_params=_cparams(("arbitrary", "arbitrary")),
        name="moe_sort",
    )(x, shift, scale, w_router_t, router_bias)


def _expert_work_table(seg, rows):
    n_tiles = seg.shape[0]
    per_tile = rows // ROW_BLOCK
    spare = n_tiles * per_tile
    max_items = spare // STEP_BLOCKS + N_GROUPS
    first = seg[:, :N_GROUPS, 0]
    count = seg[:, N_GROUPS:, 0]
    tri = jnp.arange(n_tiles)[:, None] >= jnp.arange(n_tiles)[None, :]
    cum = jnp.sum(jnp.where(tri[:, :, None], count[None, :, :], 0), axis=1)
    total = cum[-1]
    items = (total + (STEP_BLOCKS - 1)) // STEP_BLOCKS
    ends = jnp.sum(jnp.where(jnp.arange(N_GROUPS)[:, None] >= jnp.arange(N_GROUPS)[None, :], items[None, :], 0),
                   axis=1)
    n_items = ends[-1]
    step = jnp.arange(max_items + 1)
    live = step < n_items
    ref_step = jnp.where(live, step, jnp.maximum(n_items - 1, 0))
    grp = jnp.sum((ref_step[:, None] >= ends[None, :-1]).astype(jnp.int32), axis=1)
    pick = grp[:, None] == jnp.arange(N_GROUPS)[None, :]
    of_group = lambda v: jnp.sum(jnp.where(pick, v[None, :], 0), axis=1)
    start = of_group(ends - items)
    cum_s = jnp.sum(jnp.where(pick[:, None, :], cum[None, :, :], 0), axis=2)
    count_s = jnp.sum(jnp.where(pick[:, None, :], count[None, :, :], 0), axis=2)
    first_s = jnp.sum(jnp.where(pick[:, None, :], first[None, :, :], 0), axis=2)
    total_s = of_group(total)

    def block_id(k):
        done = cum_s <= k[:, None]
        t = jnp.sum(done.astype(jnp.int32), axis=1)
        skipped = jnp.sum(jnp.where(done, count_s, 0), axis=1)
        at_t = jnp.arange(n_tiles)[None, :] == t[:, None]
        return t * per_tile + jnp.sum(jnp.where(at_t, first_s, 0), axis=1) + (k - skipped)

    as_i32 = lambda v: v.astype(jnp.int32)
    block = jnp.arange(spare)[:, None]
    blocks = []
    where = jnp.zeros((spare,), jnp.int32)
    found = jnp.zeros((spare,), jnp.bool_)
    for u in range(STEP_BLOCKS):
        k_u = STEP_BLOCKS * (step - start) + u
        has_u = live & (k_u < total_s)
        blk_u = jnp.where(has_u, block_id(k_u), blocks[0] if u else spare)
        blocks.append(as_i32(blk_u))
        hit = has_u[None, :] & (blk_u[None, :] == block)
        where = where + jnp.sum(jnp.where(hit, STEP_BLOCKS * step[None, :] + u, 0), axis=1)
        found = found | jnp.any(hit, axis=1)
    where = jnp.where(found, where, STEP_BLOCKS * max_items)
    return (*blocks, as_i32(grp), as_i32(n_items).reshape(1)), as_i32(where)


def _moe_expert_kernel(*refs):
    n_items = refs[STEP_BLOCKS + 1]
    pos = STEP_BLOCKS + 2
    x_refs = refs[pos:pos + STEP_BLOCKS]
    c_refs = refs[pos + STEP_BLOCKS:pos + 2 * STEP_BLOCKS]
    wg_ref, wu_ref, wd_ref, y_ref = refs[pos + 2 * STEP_BLOCKS:]
    f, d = wd_ref.shape[2], wd_ref.shape[3]

    @pl.when(pl.program_id(0) < n_items[0])
    def _():
        x = jnp.concatenate([r[...] for r in x_refs], axis=0)
        c = jnp.concatenate([r[...] for r in c_refs], axis=0)
        pieces = []
        for j in range(EXPERTS_PER_GROUP):
            gate = jnp.dot(x, wg_ref[0, j], preferred_element_type=F32)
            up = jnp.dot(x, wu_ref[0, j], preferred_element_type=F32)
            pieces.append((jax.nn.silu(gate) * up * c[:, j:j + 1]).astype(BF16))
        w_down = wd_ref[0].reshape(EXPERTS_PER_GROUP * f, d)
        y = jnp.dot(jnp.concatenate(pieces, axis=1), w_down, preferred_element_type=F32)
        y_ref[...] = y.astype(BF16)

    @pl.when(pl.program_id(0) >= n_items[0])
    def _():
        y_ref[...] = jnp.zeros(y_ref.shape, BF16)


def _moe_expert_call(table, xs, cs, w_gate, w_up, w_down):
    d = xs.shape[1]
    steps = table[0].shape[0]
    n_pre = len(table)
    grp_at = STEP_BLOCKS
    rows_of = lambda u, width: pl.BlockSpec((ROW_BLOCK, width), lambda i, *pre: (pre[u][i], 0))
    weights = lambda w: pl.BlockSpec((1,) + w.shape[1:], lambda i, *pre: (pre[grp_at][i], 0, 0, 0),
                                     pipeline_mode=pl.Buffered(1))
    grid_spec = pltpu.PrefetchScalarGridSpec(
        num_scalar_prefetch=n_pre,
        grid=(steps,),
        in_specs=[rows_of(u, d) for u in range(STEP_BLOCKS)] + [rows_of(u, LANES) for u in range(STEP_BLOCKS)]
        + [weights(w_gate), weights(w_up), weights(w_down)],
        out_specs=pl.BlockSpec((STEP_BLOCKS * ROW_BLOCK, d), lambda i, *pre: (i, 0)),
    )
    return pl.pallas_call(
        _moe_expert_kernel,
        grid_spec=grid_spec,
        out_shape=jax.ShapeDtypeStruct((steps * STEP_BLOCKS * ROW_BLOCK, d), BF16),
        compiler_params=_cparams(("arbitrary",)),
        name="moe_experts",
    )(*table, *([xs] * STEP_BLOCKS), *([cs] * STEP_BLOCKS), w_gate, w_up, w_down)


def _moe_combine_kernel(where_ref, dest_ref, *refs, alpha, n_blocks):
    y_refs = refs[:n_blocks]
    x_ref, gate_ref, g_ref, b_ref, o_ref = refs[n_blocks:]
    tm = x_ref.shape[1]
    ys = jnp.concatenate([r[...] for r in y_refs], axis=0)
    row_id = lax.broadcasted_iota(jnp.int32, (ys.shape[0], tm), 0)
    p = jnp.where(row_id == dest_ref[0], 1.0, 0.0).astype(BF16)
    y = lax.dot_general(p, ys, (((0,), (0,)), ((), ())), preferred_element_type=F32)
    r = alpha * x_ref[0] + gate_ref[0] * y
    o_ref[0] = _layer_norm(r, g_ref[...], b_ref[...])


def _moe_combine_call(where, dest, ys, x, gate, ln_g, ln_b, *, alpha, sort_tm, tm):
    bsz, s, d = x.shape
    nt = s // sort_tm
    parts = sort_tm // tm
    n_blocks = _sort_rows(sort_tm) // ROW_BLOCK
    tile = lambda b, i: b * nt + i
    y_spec = lambda j: pl.BlockSpec((ROW_BLOCK, d), lambda b, i, k, w: (w[tile(b, i) * n_blocks + j], 0))
    grid_spec = pltpu.PrefetchScalarGridSpec(
        num_scalar_prefetch=1,
        grid=(bsz, nt, parts),
        in_specs=[pl.BlockSpec((1, 1, tm), lambda b, i, k, w: (tile(b, i), 0, k))]
        + [y_spec(j) for j in range(n_blocks)]
        + [
            pl.BlockSpec((1, tm, d), lambda b, i, k, w: (b, i * parts + k, 0)),
            pl.BlockSpec((1, 1, d), lambda b, i, k, w: (b, 0, 0)),
            pl.BlockSpec((1, d), lambda b, i, k, w: (0, 0)),
            pl.BlockSpec((1, d), lambda b, i, k, w: (0, 0)),
        ],
        out_specs=pl.BlockSpec((1, tm, d), lambda b, i, k, w: (b, i * parts + k, 0)),
    )
    return pl.pallas_call(
        functools.partial(_moe_combine_kernel, alpha=alpha, n_blocks=n_blocks),
        grid_spec=grid_spec,
        out_shape=jax.ShapeDtypeStruct((bsz, s, d), F32),
        compiler_params=_cparams(("arbitrary", "arbitrary", "arbitrary")),
        name="moe_combine",
    )(where, dest, *([ys] * n_blocks), x, gate, ln_g, ln_b)


def _moe(x, shift, scale, gate, w_router_t, router_bias, w_gate, w_up, w_down, ln_g, ln_b, *, alpha):
    sort_tm = min(SORT_TM, x.shape[1])
    xs, cs, dest, seg = _moe_sort_call(x, shift, scale, w_router_t, router_bias, tm=sort_tm)
    table, where = _expert_work_table(seg, _sort_rows(sort_tm))
    by_group = lambda w: w.reshape((N_GROUPS, EXPERTS_PER_GROUP) + w.shape[1:]).astype(BF16)
    ys = _moe_expert_call(table, xs, cs, by_group(w_gate), by_group(w_up), by_group(w_down))
    return _moe_combine_call(where, dest, ys, x, gate, ln_g, ln_b, alpha=alpha, sort_tm=sort_tm,
                             tm=min(ROW_TM, sort_tm))


def _rope_tables(n):
    freqs = LANES // 8
    inv = jnp.power(ROPE_BASE, -jnp.arange(freqs, dtype=F32) / freqs)
    tok = jnp.arange(n)
    ang_r = (tok // GRID_W).astype(F32)[:, None] * inv
    ang_c = (tok % GRID_W).astype(F32)[:, None] * inv
    ang = jnp.concatenate([ang_r, ang_c], axis=1)
    cos = jnp.tile(jnp.cos(ang), (1, 4))
    sin = jnp.tile(jnp.sin(ang), (1, 4))
    sign = jnp.where(jnp.arange(LANES) < LANES // 2, -1.0, 1.0).astype(F32)
    return cos, sin * sign


def _rope_column_perm():
    perm = np.zeros((DA_HEADS, 2, 2, 2, 16), np.int32)
    for h in range(DA_HEADS):
        for p in range(2):
            for m in range(2):
                for ax in range(2):
                    for f in range(16):
                        perm[h, p, m, ax, f] = h * LANES + m * 64 + ax * 32 + p * 16 + f
    return perm.reshape(-1)


def kernel(x, c, ctx, c_ctx, w_mod, b_mod, ln_g, ln_b, w_in_e, conv_w, lambda_q1, lambda_k1, lambda_q2, lambda_k2,
           subln_g, w_out_e, w_in_o, v_ln_g, v_ln_b, w_spatial, b_spatial, w_out_o, w_router, router_bias,
           w_gate, w_up, w_down):
    bsz, s, d = x.shape
    depth = w_mod.shape[0]
    assert depth == 2 and w_in_e.shape[0] == 1 and w_in_o.shape[0] == 1, "two-layer (even, odd) stack only"
    conv_dim = conv_w.shape[-1]
    q_dim = DA_HEADS * LANES
    q_col, k_col, v_col = 3 * conv_dim, 3 * conv_dim + q_dim, 3 * conv_dim + 2 * q_dim
    assert w_in_e.shape[2] == v_col + q_dim and s % GRID_W == 0
    alpha = float((2 * depth) ** 0.25)

    rows = 8 * ((bsz + 1 + 7) // 8)
    cond = jnp.zeros((rows, d), F32).at[:bsz].set(c).at[bsz].set(c_ctx)
    mods = _mod_call(cond, w_mod, b_mod)

    def mod_vec(l, k, ctx_row=False):
        v = mods[l, bsz:bsz + 1, k * d:(k + 1) * d] if ctx_row else mods[l, :bsz, k * d:(k + 1) * d]
        return v.reshape(-1, 1, d)

    w_router_t = w_router.T.astype(BF16)
    rbias = router_bias.reshape(-1, 1).astype(F32)

    perm = _rope_column_perm()
    w_in = w_in_e[0]
    w_in = jnp.concatenate(
        [w_in[:, :q_col], w_in[:, q_col:k_col][:, perm], w_in[:, k_col:v_col][:, perm], w_in[:, v_col:]],
        axis=1).astype(BF16)
    cos_t, sin_t = _rope_tables(s)
    tn = q_dim
    assert q_col % tn == 0
    q_tile, k_tile = q_col // tn, k_col // tn
    conv_proj, q, k, v = _proj_call(
        x, mod_vec(0, 0), mod_vec(0, 1), w_in, tm=PROJ_TM, tn=tn, n_flat=q_tile,
        rope=(cos_t, sin_t, (q_tile, k_tile), q_tile, float(64 ** -0.5 * math.log2(math.e))))
    kc, vc = _proj_call(ctx, mod_vec(0, 0, True), mod_vec(0, 1, True), w_in[:, k_col:], tm=PROJ_TM, tn=tn, n_flat=0)

    lam_init = 0.8 - 0.6 * math.exp(-0.3 * 0)
    lam_params = jnp.zeros((8, LANES), F32)
    for r, p in enumerate((lambda_q1, lambda_k1, lambda_q2, lambda_k2)):
        lam_params = lam_params.at[r, :p.shape[-1]].set(p[0].astype(F32))
    attn = _attn_call(q, k, v, kc, vc, lam_params, subln_g[0].reshape(1, LANES), lam_init=lam_init, tq=ATTN_TQ)
    x = _out_ln_call(conv_proj, attn, w_out_e[0].astype(BF16), x, mod_vec(0, 2), ln_g[0, 0:1], ln_b[0, 0:1],
                     alpha=alpha, tm=ROW_TM, conv_w=conv_w[0])
    x = _moe(x, mod_vec(0, 3), mod_vec(0, 4), mod_vec(0, 5), w_router_t, rbias,
             w_gate[0], w_up[0], w_down[0], ln_g[0, 1:2], ln_b[0, 1:2], alpha=alpha)

    w_gmlp = w_in_o[0].astype(BF16)
    uv, = _proj_call(x, mod_vec(1, 0), mod_vec(1, 1), w_gmlp, tm=PROJ_TM, tn=tn, n_flat=w_gmlp.shape[1] // tn,
                     gelu=True)
    gated = _spatial_call(uv, v_ln_g[0:1], v_ln_b[0:1], w_spatial[0].astype(BF16), b_spatial[0].T, tm=2 * CHUNK)
    x = _out_ln_call(gated, gated, w_out_o[0].astype(BF16), x, mod_vec(1, 2), ln_g[1, 0:1], ln_b[1, 0:1],
                     alpha=alpha, tm=ROW_TM)
    x = _moe(x, mod_vec(1, 3), mod_vec(1, 4), mod_vec(1, 5), w_router_t, rbias,
             w_gate[1], w_up[1], w_down[1], ln_g[1, 1:2], ln_b[1, 1:2], alpha=alpha)
    return x
```

```python
import functools
import math

import numpy as np
import jax
import jax.numpy as jnp
from jax import lax
from jax.experimental import pallas as pl
from jax.experimental.pallas import tpu as pltpu

F32 = jnp.float32
BF16 = jnp.bfloat16

GRID_W = 64
DA_HEADS = 8
N_EXPERTS = 16
N_GROUPS = 4
EXPERTS_PER_GROUP = N_EXPERTS // N_GROUPS
ROPE_BASE = 10000.0
LN_EPS = 1e-5
CHUNK = 128
GMLP_GROUPS = 16
LANES = 128
MXU_DEPTH = 256

VMEM_LIMIT = 56 * 1024 * 1024
PROJ_TM = 1024
ROW_TM = 512
ATTN_TQ = 1024


def _cparams(sem):
    return pltpu.CompilerParams(dimension_semantics=sem, vmem_limit_bytes=VMEM_LIMIT)


def _layer_norm(r, g, b):
    mu = jnp.mean(r, axis=-1, keepdims=True)
    d = r - mu
    var = jnp.mean(d * d, axis=-1, keepdims=True)
    return d * lax.rsqrt(var + LN_EPS) * g + b


def _mod_kernel(c_ref, w_ref, b_ref, o_ref):
    c = c_ref[...]
    s = (c * jax.nn.sigmoid(c)).astype(BF16)
    o_ref[0] = jnp.dot(s, w_ref[0].astype(BF16), preferred_element_type=F32) + b_ref[0]


def _mod_call(cond, w_mod, b_mod):
    depth, d, n = w_mod.shape
    rows = cond.shape[0]
    tn = 1024
    return pl.pallas_call(
        _mod_kernel,
        grid=(depth, n // tn),
        in_specs=[
            pl.BlockSpec((rows, d), lambda l, j: (0, 0)),
            pl.BlockSpec((1, d, tn), lambda l, j: (l, 0, j)),
            pl.BlockSpec((1, 1, tn), lambda l, j: (l, 0, j)),
        ],
        out_specs=pl.BlockSpec((1, rows, tn), lambda l, j: (l, 0, j)),
        out_shape=jax.ShapeDtypeStruct((depth, rows, n), F32),
        compiler_params=_cparams(("arbitrary", "arbitrary")),
        name="adaln_mod",
    )(cond, w_mod, b_mod.reshape(depth, 1, n))


def _proj_kernel(*refs, n_flat, n_head_tiles, rope, gelu):
    x_ref, sh_ref, sc_ref, w_ref = refs[:4]
    pos = 4
    if rope is not None:
        cos_ref, sin_ref = refs[pos:pos + 2]
        pos += 2
    flat_ref = refs[pos] if n_flat else None
    pos += 1 if n_flat else 0
    head_refs = refs[pos:pos + n_head_tiles]
    h_ref = refs[pos + n_head_tiles]
    j = pl.program_id(2)

    @pl.when(j == 0)
    def _():
        h_ref[...] = (x_ref[0] * (1.0 + sc_ref[0]) + sh_ref[0]).astype(BF16)

    tn = w_ref.shape[1]
    halves = [slice(c * (tn // 2), (c + 1) * (tn // 2)) for c in range(2)]

    def product(cols):
        return jnp.dot(h_ref[...], w_ref[:, cols], preferred_element_type=F32)

    if n_flat:
        @pl.when(j < n_flat)
        def _():
            for cols in halves:
                acc = product(cols)
                flat_ref[0, :, cols] = (jax.nn.gelu(acc, approximate=True) if gelu else acc).astype(flat_ref.dtype)

    for t in range(n_head_tiles):
        tile = n_flat + t
        o_ref = head_refs[t]

        @pl.when(j == tile)
        def _(tile=tile, o_ref=o_ref):
            rotary = rope is not None and tile in rope[0]
            if rotary:
                scale = rope[2] if tile == rope[1] else 1.0
                cs = cos_ref[...] * scale
                sn = sin_ref[...] * scale
            for cols in halves:
                acc = product(cols)
                for h in range(acc.shape[1] // LANES):
                    piece = acc[:, h * LANES:(h + 1) * LANES]
                    if rotary:
                        piece = piece * cs + pltpu.roll(piece, LANES // 2, axis=1) * sn
                    o_ref[0, cols.start // LANES + h] = piece.astype(o_ref.dtype)


def _proj_call(x, shift, scale, w, *, tm, tn, n_flat, rope=None, gelu=False):
    bsz, s, d = x.shape
    n_tiles = w.shape[1] // tn
    n_head_tiles = n_tiles - n_flat
    heads = tn // LANES
    tm = min(tm, s)
    per_batch = shift.shape[0] > 1
    mod_map = (lambda b, i, j: (b, 0, 0)) if per_batch else (lambda b, i, j: (0, 0, 0))
    in_specs = [
        pl.BlockSpec((1, tm, d), lambda b, i, j: (b, i, 0)),
        pl.BlockSpec((1, 1, d), mod_map),
        pl.BlockSpec((1, 1, d), mod_map),
        pl.BlockSpec((d, tn), lambda b, i, j: (0, j)),
    ]
    args = [x, shift, scale, w]
    rope_static = None
    if rope is not None:
        cos_t, sin_t, rope_tiles, q_tile, q_scale = rope
        in_specs += [pl.BlockSpec((tm, LANES), lambda b, i, j: (i, 0))] * 2
        args += [cos_t, sin_t]
        rope_static = (tuple(rope_tiles), q_tile, q_scale)
    out_specs, out_shape = [], []
    if n_flat:
        out_specs.append(pl.BlockSpec((1, tm, tn), lambda b, i, j: (b, i, jnp.minimum(j, n_flat - 1))))
        out_shape.append(jax.ShapeDtypeStruct((bsz, s, n_flat * tn), BF16))
    for _ in range(n_head_tiles):
        out_specs.append(pl.BlockSpec((1, heads, tm, LANES), lambda b, i, j: (b, 0, i, 0)))
        out_shape.append(jax.ShapeDtypeStruct((bsz, heads, s, LANES), BF16))
    return pl.pallas_call(
        functools.partial(_proj_kernel, n_flat=n_flat, n_head_tiles=n_head_tiles, rope=rope_static, gelu=gelu),
        grid=(bsz, s // tm, n_tiles),
        in_specs=in_specs,
        out_specs=out_specs,
        out_shape=out_shape,
        scratch_shapes=[pltpu.VMEM((tm, d), BF16)],
        compiler_params=_cparams(("arbitrary", "arbitrary", "arbitrary")),
        name="mod_proj",
    )(*args)


ONES_ROWS = 16


def _attn_kernel(lam_ref, g_ref, q_ref, kc_ref, vc_ref, k_ref, v_ref, o_ref, kf_ref, vt_ref, s_ref, p_ref, acc_ref,
                 *, tk, lam_init):
    tq = q_ref.shape[2]
    c_len = kc_ref.shape[2]
    s_len = k_ref.shape[2]
    n_chunks = (c_len + s_len) // tk
    vt_chunk = 512

    @pl.when(pl.program_id(2) == 0)
    def _():
        kf_ref[:c_len, :] = kc_ref[0, 0]
        kf_ref[c_len:, :] = k_ref[0, 0]
        vt_ref[LANES:, :] = jnp.ones((ONES_ROWS, vt_ref.shape[1]), BF16)
        vt_ref[:LANES, :c_len] = jnp.transpose(vc_ref[0, 0].astype(F32)).astype(BF16)
        for n in range(s_len // vt_chunk):
            rows = slice(n * vt_chunk, (n + 1) * vt_chunk)
            cols = slice(c_len + n * vt_chunk, c_len + (n + 1) * vt_chunk)
            vt_ref[:LANES, cols] = jnp.transpose(v_ref[0, 0, rows, :].astype(F32)).astype(BF16)

    q_t = jnp.transpose(q_ref[0, 0].astype(F32))
    dim = lax.broadcasted_iota(jnp.int32, q_t.shape, 0)
    first_map = (dim % (LANES // 2)) < (LANES // 4)
    qs_t = jnp.concatenate([jnp.where(first_map, q_t, 0.0), jnp.where(first_map, 0.0, q_t)], axis=1).astype(BF16)

    acc_ref[...] = jnp.zeros(acc_ref.shape, F32)

    def scores(t):
        off = t * tk if isinstance(t, int) else pl.multiple_of(t * tk, tk)
        return jnp.dot(kf_ref[pl.ds(off, tk), :], qs_t, preferred_element_type=F32)

    def softmax(slot, m_old):
        s_t = s_ref[slot]
        m_new = jnp.maximum(m_old, jnp.max(s_t, axis=0, keepdims=True))
        p_ref[slot] = jnp.exp2((s_t - m_new).astype(BF16))
        return m_new, jnp.exp2(m_old - m_new)

    def values(t, slot, alpha):
        off = t * tk if isinstance(t, int) else pl.multiple_of(t * tk, tk)
        pv = jnp.dot(vt_ref[:, pl.ds(off, tk)], p_ref[slot], preferred_element_type=F32)
        acc_ref[...] = alpha * acc_ref[...] + pv

    def tick(t, parity, m, alpha_prev):
        s_ref[parity] = scores(t)
        m, alpha = softmax(1 - parity, m)
        values(t - 2, parity, alpha_prev)
        return m, alpha

    s_ref[0] = scores(0)
    s_ref[1] = scores(1)
    m, alpha = softmax(0, jnp.full((1, 2 * tq), -jnp.inf, F32))

    def pair(jj, carry):
        m, alpha = tick(2 + 2 * jj, 0, *carry)
        return tick(3 + 2 * jj, 1, m, alpha)

    n_full = n_chunks - 2
    m, alpha = lax.fori_loop(0, n_full // 2, pair, (m, alpha))
    if n_full % 2:
        m, alpha = tick(n_chunks - 1, (n_chunks - 1) % 2, m, alpha)
    last = (n_chunks - 1) % 2
    m, alpha_last = softmax(last, m)
    values(n_chunks - 2, 1 - last, alpha)
    values(n_chunks - 1, last, alpha_last)

    lp = lam_ref[...]
    lam = (jnp.exp(jnp.sum(lp[0:1] * lp[1:2], axis=-1, keepdims=True))
           - jnp.exp(jnp.sum(lp[2:3] * lp[3:4], axis=-1, keepdims=True)) + lam_init)
    acc = acc_ref[...]
    o_t = acc[:LANES] / acc[LANES:LANES + 1]
    o_t = o_t[:, :tq] - lam * o_t[:, tq:]
    o_t = o_t * lax.rsqrt(jnp.mean(o_t * o_t, axis=0, keepdims=True) + LN_EPS)
    o_ref[0, 0] = (jnp.transpose(o_t) * g_ref[...] * (1.0 - lam_init)).astype(o_ref.dtype)


def _attn_call(q, k, v, kc, vc, lam_params, subln_g, *, lam_init, tq):
    bsz, h, s, _ = q.shape
    c_len = kc.shape[2]
    tq = min(tq, s)
    tk = max(t for t in (MXU_DEPTH, 2 * MXU_DEPTH, 3 * MXU_DEPTH) if (c_len + s) % t == 0)
    assert (c_len + s) // tk >= 3 and s % 512 == 0
    whole = lambda n: pl.BlockSpec((1, 1, n, LANES), lambda b, hh, i: (b, hh, 0, 0))
    return pl.pallas_call(
        functools.partial(_attn_kernel, tk=tk, lam_init=lam_init),
        grid=(bsz, h, s // tq),
        in_specs=[
            pl.BlockSpec((8, LANES), lambda b, hh, i: (0, 0)),
            pl.BlockSpec((1, LANES), lambda b, hh, i: (0, 0)),
            pl.BlockSpec((1, 1, tq, LANES), lambda b, hh, i: (b, hh, i, 0)),
            whole(c_len), whole(c_len), whole(s), whole(s),
        ],
        out_specs=pl.BlockSpec((1, 1, tq, LANES), lambda b, hh, i: (b, hh, i, 0)),
        out_shape=jax.ShapeDtypeStruct((bsz, h, s, LANES), BF16),
        scratch_shapes=[
            pltpu.VMEM((c_len + s, LANES), BF16),
            pltpu.VMEM((LANES + ONES_ROWS, c_len + s), BF16),
            pltpu.VMEM((2, tk, 2 * tq), F32),
            pltpu.VMEM((2, tk, 2 * tq), BF16),
            pltpu.VMEM((LANES + ONES_ROWS, 2 * tq), F32),
        ],
        compiler_params=_cparams(("arbitrary", "arbitrary", "arbitrary")),
        name="diff_attn",
    )(lam_params, subln_g, q, kc, vc, k, v)


def _gated_conv(xa_ref, bg_ref, cg_ref, xap_ref, cgp_ref, xan_ref, cgn_ref, w_ref):
    i = pl.program_id(1)
    tm = xa_ref.shape[1]
    halo = xap_ref.shape[1]
    z = xa_ref[0].astype(F32) * cg_ref[0].astype(F32)
    z_before = (xap_ref[0].astype(F32) * cgp_ref[0].astype(F32))[halo - 1:halo]
    z_after = (xan_ref[0].astype(F32) * cgn_ref[0].astype(F32))[0:1]
    z_before = jnp.where(i == 0, 0.0, z_before)
    z_after = jnp.where(i == pl.num_programs(1) - 1, 0.0, z_after)
    row = lax.broadcasted_iota(jnp.int32, z.shape, 0)
    z_prev = jnp.where(row == 0, z_before, pltpu.roll(z, 1, axis=0))
    z_next = jnp.where(row == tm - 1, z_after, pltpu.roll(z, tm - 1, axis=0))
    w = w_ref[...]
    conv = w[0:1] * z_prev + w[1:2] * z + w[2:3] * z_next
    return bg_ref[0].astype(F32) * conv


def _gated_conv_operands(proj, conv_w, *, width, tm):
    s = proj.shape[1]
    halo = 16
    nh = tm // halo
    last = s // halo - 1
    main = lambda c: pl.BlockSpec((1, tm, width), lambda b, i: (b, i, c))
    prev = lambda c: pl.BlockSpec((1, halo, width), lambda b, i: (b, jnp.maximum(i * nh - 1, 0), c))
    nxt = lambda c: pl.BlockSpec((1, halo, width), lambda b, i: (b, jnp.minimum((i + 1) * nh, last), c))
    w_pad = jnp.zeros((8, width), F32).at[:conv_w.shape[0]].set(conv_w)
    specs = [main(0), main(1), main(2), prev(0), prev(2), nxt(0), nxt(2), pl.BlockSpec((8, width), lambda b, i: (0, 0))]
    return specs, [proj] * 7 + [w_pad]


def _gmlp_gate(u_ref, v_ref, g_ref, b_ref, ws_ref, bs_ref, a_ref):
    tm = u_ref.shape[1]
    v = _layer_norm(v_ref[0].astype(F32), g_ref[...], b_ref[...]).astype(BF16)
    bs = bs_ref[...]
    for n in range(tm // CHUNK):
        rows = slice(n * CHUNK, (n + 1) * CHUNK)
        for g in range(GMLP_GROUPS):
            cols = slice(g * LANES, (g + 1) * LANES)
            sg = jnp.dot(ws_ref[g], v[rows, cols], preferred_element_type=F32) + bs[:, g:g + 1]
            a_ref[rows, cols] = (u_ref[0, rows, cols].astype(F32) * sg).astype(a_ref.dtype)


def _out_ln_kernel(*refs, alpha, mixer):
    if mixer == "conv_attn":
        conv_refs, (heads_ref, w_ref, x_ref, gate_ref, g_ref, b_ref, o_ref) = refs[:8], refs[8:]
        a = jnp.concatenate([_gated_conv(*conv_refs).astype(BF16)]
                            + [heads_ref[0, h] for h in range(heads_ref.shape[1])], axis=1)
    else:
        gmlp_refs, (w_ref, x_ref, gate_ref, g_ref, b_ref, o_ref, a_ref) = refs[:6], refs[6:]
        _gmlp_gate(*gmlp_refs, a_ref)
        a = a_ref[...]
    half = a.shape[0] // 2
    for rows in (slice(0, half), slice(half, 2 * half)):
        y = jnp.dot(a[rows], w_ref[...], preferred_element_type=F32)
        r = alpha * x_ref[0, rows, :] + gate_ref[0] * y
        o_ref[0, rows, :] = _layer_norm(r, g_ref[...], b_ref[...])


def _out_ln_call(mixer, mixer_specs, mixer_args, w, x, gate, ln_g, ln_b, *, alpha, tm):
    bsz, s, d = x.shape
    scratch = [pltpu.VMEM((tm, w.shape[0]), BF16)] if mixer == "gmlp" else []
    return pl.pallas_call(
        functools.partial(_out_ln_kernel, alpha=alpha, mixer=mixer),
        grid=(bsz, s // tm),
        in_specs=mixer_specs + [
            pl.BlockSpec(w.shape, lambda b, i: (0, 0)),
            pl.BlockSpec((1, tm, d), lambda b, i: (b, i, 0)),
            pl.BlockSpec((1, 1, d), lambda b, i: (b, 0, 0)),
            pl.BlockSpec((1, d), lambda b, i: (0, 0)),
            pl.BlockSpec((1, d), lambda b, i: (0, 0)),
        ],
        out_specs=pl.BlockSpec((1, tm, d), lambda b, i: (b, i, 0)),
        out_shape=jax.ShapeDtypeStruct((bsz, s, d), F32),
        scratch_shapes=scratch,
        compiler_params=_cparams(("arbitrary", "arbitrary")),
        name="out_proj_ln",
    )(*mixer_args, w, x, gate, ln_g, ln_b)


def _conv_attn_operands(proj, conv_w, attn, *, tm):
    specs, args = _gated_conv_operands(proj, conv_w, width=conv_w.shape[-1], tm=tm)
    specs.append(pl.BlockSpec((1, attn.shape[1], tm, LANES), lambda b, i: (b, 0, i, 0)))
    return specs, args + [attn]


def _gmlp_operands(uv, v_g, v_b, w_s, b_s_t, *, tm):
    d = uv.shape[2] // 2
    specs = [
        pl.BlockSpec((1, tm, d), lambda b, i: (b, i, 0)),
        pl.BlockSpec((1, tm, d), lambda b, i: (b, i, 1)),
        pl.BlockSpec((1, d), lambda b, i: (0, 0)),
        pl.BlockSpec((1, d), lambda b, i: (0, 0)),
        pl.BlockSpec(w_s.shape, lambda b, i: (0, 0, 0)),
        pl.BlockSpec(b_s_t.shape, lambda b, i: (0, 0)),
    ]
    return specs, [uv, uv, v_g, v_b, w_s, b_s_t]


def _route(scores, bias):
    sel = scores + bias
    rows = [sel[e:e + 1, :] for e in range(N_EXPERTS)]
    group_score = []
    for g in range(N_GROUPS):
        r = rows[g * EXPERTS_PER_GROUP:(g + 1) * EXPERTS_PER_GROUP]
        best = None
        for a in range(EXPERTS_PER_GROUP):
            for b in range(a + 1, EXPERTS_PER_GROUP):
                pair = r[a] + r[b]
                best = pair if best is None else jnp.maximum(best, pair)
        group_score.append(best)
    one = jnp.ones_like(rows[0])
    zero = jnp.zeros_like(rows[0])
    picked = []
    chosen = []
    for g in range(N_GROUPS):
        beaten = zero
        for o in range(N_GROUPS):
            if o < g:
                beaten = jnp.where(group_score[o] >= group_score[g], one, beaten)
            elif o > g:
                beaten = jnp.where(group_score[o] > group_score[g], one, beaten)
        chosen.append(1.0 - beaten)
        for a in range(EXPERTS_PER_GROUP):
            e = g * EXPERTS_PER_GROUP + a
            rank = zero
            for b in range(EXPERTS_PER_GROUP):
                o = g * EXPERTS_PER_GROUP + b
                if b < a:
                    rank = rank + jnp.where(rows[o] >= rows[e], one, zero)
                elif b > a:
                    rank = rank + jnp.where(rows[o] > rows[e], one, zero)
            picked.append(jnp.where(rank < 2.0, one, zero) * chosen[g])
    w = [picked[e] * scores[e:e + 1, :] for e in range(N_EXPERTS)]
    total = w[0]
    for e in range(1, N_EXPERTS):
        total = total + w[e]
    return [we / total for we in w], chosen


ROW_BLOCK = 64
STEP_BLOCKS = 4
DEST_LANE = EXPERTS_PER_GROUP
SORT_TM = 1024


def _sort_rows(tm):
    return tm + N_GROUPS * ROW_BLOCK


def _moe_sort_kernel(x_ref, sh_ref, sc_ref, wr_ref, rb_ref, xs_ref, cs_ref, tok_ref, seg_ref):
    tm, d = x_ref.shape[1], x_ref.shape[2]
    rows = xs_ref.shape[0]
    col_chunk = 512
    h = (x_ref[0] * (1.0 + sc_ref[0]) + sh_ref[0]).astype(BF16)
    logits = lax.dot_general(wr_ref[...], h, (((1,), (1,)), ((), ())), preferred_element_type=F32)
    comb, chosen = _route(jax.nn.sigmoid(logits), rb_ref[...])
    src = lax.broadcasted_iota(jnp.int32, (tm, tm), 0)
    dst = lax.broadcasted_iota(jnp.int32, (tm, tm), 1)
    before = jnp.where(src < dst, 1.0, 0.0).astype(BF16)
    pad_rows = [jnp.zeros_like(chosen[0])] * (8 - N_GROUPS)
    rank = jnp.dot(jnp.concatenate(chosen + pad_rows, axis=0).astype(BF16), before, preferred_element_type=F32)
    dest = jnp.zeros_like(chosen[0])
    first = jnp.zeros((1, 1), F32)
    firsts, counts = [], []
    for g in range(N_GROUPS):
        count = jnp.sum(chosen[g], axis=1, keepdims=True)
        n_blocks = jnp.floor((count + (ROW_BLOCK - 1)) * (1.0 / ROW_BLOCK))
        firsts.append(first)
        counts.append(n_blocks)
        dest = dest + chosen[g] * (first * ROW_BLOCK + rank[g:g + 1])
        first = first + n_blocks
    seg = jnp.concatenate([jnp.broadcast_to(v, (1, LANES)) for v in firsts + counts], axis=0)
    seg_ref[0] = seg.astype(jnp.int32)
    dest_i = dest.astype(jnp.int32)
    row_id = lax.broadcasted_iota(jnp.int32, (rows, tm), 0)
    p = jnp.where(row_id == dest_i, 1.0, 0.0).astype(BF16)
    for c in range(d // col_chunk):
        cols = slice(c * col_chunk, (c + 1) * col_chunk)
        xs_ref[:, cols] = jnp.dot(p, h[:, cols], preferred_element_type=F32).astype(BF16)
    in_group = []
    for j in range(EXPERTS_PER_GROUP):
        cj = comb[j]
        for g in range(1, N_GROUPS):
            cj = cj + comb[g * EXPERTS_PER_GROUP + j]
        in_group.append(cj)
    c_rows = jnp.concatenate(in_group + [dest, jnp.zeros((LANES - DEST_LANE - 1, tm), F32)], axis=0)
    c_cols = jnp.transpose(c_rows)
    tok_ref[...] = c_cols
    hi = c_cols.astype(BF16)
    lo = (c_cols - hi.astype(F32)).astype(BF16)
    cs = jnp.dot(p, jnp.concatenate([hi, lo], axis=1), preferred_element_type=F32)
    cs_ref[...] = cs[:, :LANES] + cs[:, LANES:]


def _moe_sort_call(x, shift, scale, w_router_t, router_bias, *, tm):
    bsz, s, d = x.shape
    nt = s // tm
    n_tiles = bsz * nt
    rows = _sort_rows(tm)
    tile = lambda b, i: b * nt + i
    vec = lambda: pl.BlockSpec((1, 1, d), lambda b, i: (b, 0, 0))
    return pl.pallas_call(
        _moe_sort_kernel,
        grid=(bsz, nt),
        in_specs=[
            pl.BlockSpec((1, tm, d), lambda b, i: (b, i, 0)),
            vec(), vec(),
            pl.BlockSpec(w_router_t.shape, lambda b, i: (0, 0)),
            pl.BlockSpec(router_bias.shape, lambda b, i: (0, 0)),
        ],
        out_specs=[
            pl.BlockSpec((rows, d), lambda b, i: (tile(b, i), 0)),
            pl.BlockSpec((rows, LANES), lambda b, i: (tile(b, i), 0)),
            pl.BlockSpec((tm, LANES), lambda b, i: (tile(b, i), 0)),
            pl.BlockSpec((1, 2 * N_GROUPS, LANES), lambda b, i: (tile(b, i), 0, 0)),
        ],
        out_shape=[
            jax.ShapeDtypeStruct(((n_tiles + 1) * rows, d), BF16),
            jax.ShapeDtypeStruct(((n_tiles + 1) * rows, LANES), F32),
            jax.ShapeDtypeStruct((n_tiles * tm, LANES), F32),
            jax.ShapeDtypeStruct((n_tiles, 2 * N_GROUPS, LANES), jnp.int32),
        ],
        compiler_params=_cparams(("arbitrary", "arbitrary")),
        name="moe_sort",
    )(x, shift, scale, w_router_t, router_bias)


def _expert_work_table(seg, rows):
    n_tiles = seg.shape[0]
    per_tile = rows // ROW_BLOCK
    spare = n_tiles * per_tile
    max_items = spare // STEP_BLOCKS + N_GROUPS
    first = seg[:, :N_GROUPS, 0]
    count = seg[:, N_GROUPS:, 0]
    tri = jnp.arange(n_tiles)[:, None] >= jnp.arange(n_tiles)[None, :]
    cum = jnp.sum(jnp.where(tri[:, :, None], count[None, :, :], 0), axis=1)
    total = cum[-1]
    items = (total + (STEP_BLOCKS - 1)) // STEP_BLOCKS
    ends = jnp.sum(jnp.where(jnp.arange(N_GROUPS)[:, None] >= jnp.arange(N_GROUPS)[None, :], items[None, :], 0),
                   axis=1)
    n_items = ends[-1]
    step = jnp.arange(max_items + 1)
    live = step < n_items
    ref_step = jnp.where(live, step, jnp.maximum(n_items - 1, 0))
    grp = jnp.sum((ref_step[:, None] >= ends[None, :-1]).astype(jnp.int32), axis=1)
    pick = grp[:, None] == jnp.arange(N_GROUPS)[None, :]
    of_group = lambda v: jnp.sum(jnp.where(pick, v[None, :], 0), axis=1)
    start = of_group(ends - items)
    cum_s = jnp.sum(jnp.where(pick[:, None, :], cum[None, :, :], 0), axis=2)
    count_s = jnp.sum(jnp.where(pick[:, None, :], count[None, :, :], 0), axis=2)
    first_s = jnp.sum(jnp.where(pick[:, None, :], first[None, :, :], 0), axis=2)
    total_s = of_group(total)

    def block_id(k):
        done = cum_s <= k[:, None]
        t = jnp.sum(done.astype(jnp.int32), axis=1)
        skipped = jnp.sum(jnp.where(done, count_s, 0), axis=1)
        at_t = jnp.arange(n_tiles)[None, :] == t[:, None]
        return t * per_tile + jnp.sum(jnp.where(at_t, first_s, 0), axis=1) + (k - skipped)

    as_i32 = lambda v: v.astype(jnp.int32)
    block = jnp.arange(spare)[:, None]
    blocks = []
    where = jnp.zeros((spare,), jnp.int32)
    found = jnp.zeros((spare,), jnp.bool_)
    for u in range(STEP_BLOCKS):
        k_u = STEP_BLOCKS * (step - start) + u
        has_u = live & (k_u < total_s)
        blk_u = jnp.where(has_u, block_id(k_u), blocks[0] if u else spare)
        blocks.append(as_i32(blk_u))
        hit = has_u[None, :] & (blk_u[None, :] == block)
        where = where + jnp.sum(jnp.where(hit, STEP_BLOCKS * step[None, :] + u, 0), axis=1)
        found = found | jnp.any(hit, axis=1)
    where = jnp.where(found, where, STEP_BLOCKS * max_items)
    return (*blocks, as_i32(grp), as_i32(n_items).reshape(1)), as_i32(where)


def _moe_expert_kernel(*refs):
    n_items = refs[STEP_BLOCKS + 1]
    pos = STEP_BLOCKS + 2
    x_refs = refs[pos:pos + STEP_BLOCKS]
    c_refs = refs[pos + STEP_BLOCKS:pos + 2 * STEP_BLOCKS]
    wg_ref, wu_ref, wd_ref, y_ref = refs[pos + 2 * STEP_BLOCKS:]
    f, d = wd_ref.shape[2], wd_ref.shape[3]

    @pl.when(pl.program_id(0) < n_items[0])
    def _():
        x = jnp.concatenate([r[...] for r in x_refs], axis=0)
        c = jnp.concatenate([r[...] for r in c_refs], axis=0)
        pieces = []
        for j in range(EXPERTS_PER_GROUP):
            gate = jnp.dot(x, wg_ref[0, j], preferred_element_type=F32)
            up = jnp.dot(x, wu_ref[0, j], preferred_element_type=F32)
            pieces.append((jax.nn.silu(gate) * up * c[:, j:j + 1]).astype(BF16))
        w_down = wd_ref[0].reshape(EXPERTS_PER_GROUP * f, d)
        y = jnp.dot(jnp.concatenate(pieces, axis=1), w_down, preferred_element_type=F32)
        y_ref[...] = y.astype(BF16)

    @pl.when(pl.program_id(0) >= n_items[0])
    def _():
        y_ref[...] = jnp.zeros(y_ref.shape, BF16)


def _moe_expert_call(table, xs, cs, w_gate, w_up, w_down):
    d = xs.shape[1]
    steps = table[0].shape[0]
    n_pre = len(table)
    grp_at = STEP_BLOCKS
    rows_of = lambda u, width: pl.BlockSpec((ROW_BLOCK, width), lambda i, *pre: (pre[u][i], 0))
    weights = lambda w: pl.BlockSpec((1,) + w.shape[1:], lambda i, *pre: (pre[grp_at][i], 0, 0, 0),
                                     pipeline_mode=pl.Buffered(1))
    grid_spec = pltpu.PrefetchScalarGridSpec(
        num_scalar_prefetch=n_pre,
        grid=(steps,),
        in_specs=[rows_of(u, d) for u in range(STEP_BLOCKS)] + [rows_of(u, LANES) for u in range(STEP_BLOCKS)]
        + [weights(w_gate), weights(w_up), weights(w_down)],
        out_specs=pl.BlockSpec((STEP_BLOCKS * ROW_BLOCK, d), lambda i, *pre: (i, 0)),
    )
    return pl.pallas_call(
        _moe_expert_kernel,
        grid_spec=grid_spec,
        out_shape=jax.ShapeDtypeStruct((steps * STEP_BLOCKS * ROW_BLOCK, d), BF16),
        compiler_params=_cparams(("arbitrary",)),
        name="moe_experts",
    )(*table, *([xs] * STEP_BLOCKS), *([cs] * STEP_BLOCKS), w_gate, w_up, w_down)


def _moe_combine_kernel(where_ref, tok_ref, *refs, alpha, n_blocks):
    y_refs = refs[:n_blocks]
    x_ref, gate_ref, g_ref, b_ref, o_ref = refs[n_blocks:]
    tm = x_ref.shape[1]
    ys = jnp.concatenate([r[...] for r in y_refs], axis=0)
    half = tm // 2
    row_id = lax.broadcasted_iota(jnp.int32, (half, ys.shape[0]), 1)
    for rows in (slice(0, half), slice(half, tm)):
        dest = tok_ref[rows, DEST_LANE:DEST_LANE + 1].astype(jnp.int32)
        p_t = jnp.where(row_id == dest, 1.0, 0.0).astype(BF16)
        y = jnp.dot(p_t, ys, preferred_element_type=F32)
        r = alpha * x_ref[0, rows, :] + gate_ref[0] * y
        o_ref[0, rows, :] = _layer_norm(r, g_ref[...], b_ref[...])


def _moe_combine_call(where, tok, ys, x, gate, ln_g, ln_b, *, alpha, sort_tm, tm):
    bsz, s, d = x.shape
    nt = s // sort_tm
    parts = sort_tm // tm
    n_blocks = _sort_rows(sort_tm) // ROW_BLOCK
    tile = lambda b, i: b * nt + i
    y_spec = lambda j: pl.BlockSpec((ROW_BLOCK, d), lambda b, i, k, w: (w[tile(b, i) * n_blocks + j], 0))
    grid_spec = pltpu.PrefetchScalarGridSpec(
        num_scalar_prefetch=1,
        grid=(bsz, nt, parts),
        in_specs=[pl.BlockSpec((tm, LANES), lambda b, i, k, w: (tile(b, i) * parts + k, 0))]
        + [y_spec(j) for j in range(n_blocks)]
        + [
            pl.BlockSpec((1, tm, d), lambda b, i, k, w: (b, i * parts + k, 0)),
            pl.BlockSpec((1, 1, d), lambda b, i, k, w: (b, 0, 0)),
            pl.BlockSpec((1, d), lambda b, i, k, w: (0, 0)),
            pl.BlockSpec((1, d), lambda b, i, k, w: (0, 0)),
        ],
        out_specs=pl.BlockSpec((1, tm, d), lambda b, i, k, w: (b, i * parts + k, 0)),
    )
    return pl.pallas_call(
        functools.partial(_moe_combine_kernel, alpha=alpha, n_blocks=n_blocks),
        grid_spec=grid_spec,
        out_shape=jax.ShapeDtypeStruct((bsz, s, d), F32),
        compiler_params=_cparams(("arbitrary", "arbitrary", "arbitrary")),
        name="moe_combine",
    )(where, tok, *([ys] * n_blocks), x, gate, ln_g, ln_b)


def _moe(x, shift, scale, gate, w_router_t, router_bias, w_gate, w_up, w_down, ln_g, ln_b, *, alpha):
    sort_tm = min(SORT_TM, x.shape[1])
    xs, cs, tok, seg = _moe_sort_call(x, shift, scale, w_router_t, router_bias, tm=sort_tm)
    table, where = _expert_work_table(seg, _sort_rows(sort_tm))
    by_group = lambda w: w.reshape((N_GROUPS, EXPERTS_PER_GROUP) + w.shape[1:]).astype(BF16)
    ys = _moe_expert_call(table, xs, cs, by_group(w_gate), by_group(w_up), by_group(w_down))
    return _moe_combine_call(where, tok, ys, x, gate, ln_g, ln_b, alpha=alpha, sort_tm=sort_tm,
                             tm=min(ROW_TM, sort_tm))


def _rope_tables(n):
    freqs = LANES // 8
    inv = jnp.power(ROPE_BASE, -jnp.arange(freqs, dtype=F32) / freqs)
    tok = jnp.arange(n)
    ang_r = (tok // GRID_W).astype(F32)[:, None] * inv
    ang_c = (tok % GRID_W).astype(F32)[:, None] * inv
    ang = jnp.concatenate([ang_r, ang_c], axis=1)
    cos = jnp.tile(jnp.cos(ang), (1, 4))
    sin = jnp.tile(jnp.sin(ang), (1, 4))
    sign = jnp.where(jnp.arange(LANES) < LANES // 2, -1.0, 1.0).astype(F32)
    return cos, sin * sign


def _rope_column_perm():
    perm = np.zeros((DA_HEADS, 2, 2, 2, 16), np.int32)
    for h in range(DA_HEADS):
        for p in range(2):
            for m in range(2):
                for ax in range(2):
                    for f in range(16):
                        perm[h, p, m, ax, f] = h * LANES + m * 64 + ax * 32 + p * 16 + f
    return perm.reshape(-1)


def kernel(x, c, ctx, c_ctx, w_mod, b_mod, ln_g, ln_b, w_in_e, conv_w, lambda_q1, lambda_k1, lambda_q2, lambda_k2,
           subln_g, w_out_e, w_in_o, v_ln_g, v_ln_b, w_spatial, b_spatial, w_out_o, w_router, router_bias,
           w_gate, w_up, w_down):
    bsz, s, d = x.shape
    depth = w_mod.shape[0]
    assert depth == 2 and w_in_e.shape[0] == 1 and w_in_o.shape[0] == 1, "two-layer (even, odd) stack only"
    conv_dim = conv_w.shape[-1]
    q_dim = DA_HEADS * LANES
    q_col, k_col, v_col = 3 * conv_dim, 3 * conv_dim + q_dim, 3 * conv_dim + 2 * q_dim
    assert w_in_e.shape[2] == v_col + q_dim and s % GRID_W == 0
    alpha = float((2 * depth) ** 0.25)

    rows = 8 * ((bsz + 1 + 7) // 8)
    cond = jnp.zeros((rows, d), F32).at[:bsz].set(c).at[bsz].set(c_ctx)
    mods = _mod_call(cond, w_mod, b_mod)

    def mod_vec(l, k, ctx_row=False):
        v = mods[l, bsz:bsz + 1, k * d:(k + 1) * d] if ctx_row else mods[l, :bsz, k * d:(k + 1) * d]
        return v.reshape(-1, 1, d)

    w_router_t = w_router.T.astype(BF16)
    rbias = router_bias.reshape(-1, 1).astype(F32)

    perm = _rope_column_perm()
    w_in = w_in_e[0]
    w_in = jnp.concatenate(
        [w_in[:, :q_col], w_in[:, q_col:k_col][:, perm], w_in[:, k_col:v_col][:, perm], w_in[:, v_col:]],
        axis=1).astype(BF16)
    cos_t, sin_t = _rope_tables(s)
    tn = q_dim
    assert q_col % tn == 0
    q_tile, k_tile = q_col // tn, k_col // tn
    conv_proj, q, k, v = _proj_call(
        x, mod_vec(0, 0), mod_vec(0, 1), w_in, tm=PROJ_TM, tn=tn, n_flat=q_tile,
        rope=(cos_t, sin_t, (q_tile, k_tile), q_tile, float(64 ** -0.5 * math.log2(math.e))))
    kc, vc = _proj_call(ctx, mod_vec(0, 0, True), mod_vec(0, 1, True), w_in[:, k_col:], tm=PROJ_TM, tn=tn, n_flat=0)

    lam_init = 0.8 - 0.6 * math.exp(-0.3 * 0)
    lam_params = jnp.zeros((8, LANES), F32)
    for r, p in enumerate((lambda_q1, lambda_k1, lambda_q2, lambda_k2)):
        lam_params = lam_params.at[r, :p.shape[-1]].set(p[0].astype(F32))
    attn = _attn_call(q, k, v, kc, vc, lam_params, subln_g[0].reshape(1, LANES), lam_init=lam_init, tq=ATTN_TQ)
    x = _out_ln_call("conv_attn", *_conv_attn_operands(conv_proj, conv_w[0], attn, tm=ROW_TM), w_out_e[0].astype(BF16),
                     x, mod_vec(0, 2), ln_g[0, 0:1], ln_b[0, 0:1], alpha=alpha, tm=ROW_TM)
    x = _moe(x, mod_vec(0, 3), mod_vec(0, 4), mod_vec(0, 5), w_router_t, rbias,
             w_gate[0], w_up[0], w_down[0], ln_g[0, 1:2], ln_b[0, 1:2], alpha=alpha)

    w_gmlp = w_in_o[0].astype(BF16)
    uv, = _proj_call(x, mod_vec(1, 0), mod_vec(1, 1), w_gmlp, tm=PROJ_TM, tn=tn, n_flat=w_gmlp.shape[1] // tn,
                     gelu=True)
    gmlp = _gmlp_operands(uv, v_ln_g[0:1], v_ln_b[0:1], w_spatial[0].astype(BF16), b_spatial[0].T, tm=ROW_TM)
    x = _out_ln_call("gmlp", *gmlp, w_out_o[0].astype(BF16), x, mod_vec(1, 2), ln_g[1, 0:1], ln_b[1, 0:1],
                     alpha=alpha, tm=ROW_TM)
    x = _moe(x, mod_vec(1, 3), mod_vec(1, 4), mod_vec(1, 5), w_router_t, rbias,
             w_gate[1], w_up[1], w_down[1], ln_g[1, 1:2], ln_b[1, 1:2], alpha=alpha)
    return x
```

```python
import functools
import math

import numpy as np
import jax
import jax.numpy as jnp
from jax import lax
from jax.experimental import pallas as pl
from jax.experimental.pallas import tpu as pltpu

F32 = jnp.float32
BF16 = jnp.bfloat16

GRID_W = 64
DA_HEADS = 8
N_EXPERTS = 16
N_GROUPS = 4
EXPERTS_PER_GROUP = N_EXPERTS // N_GROUPS
ROPE_BASE = 10000.0
LN_EPS = 1e-5
CHUNK = 128
GMLP_GROUPS = 16
LANES = 128
MXU_DEPTH = 256

VMEM_LIMIT = 56 * 1024 * 1024
PROJ_TM = 1024
ROW_TM = 512
ATTN_TQ = 1024


def _cparams(sem):
    return pltpu.CompilerParams(dimension_semantics=sem, vmem_limit_bytes=VMEM_LIMIT)


def _layer_norm(r, g, b):
    mu = jnp.mean(r, axis=-1, keepdims=True)
    d = r - mu
    var = jnp.mean(d * d, axis=-1, keepdims=True)
    return d * lax.rsqrt(var + LN_EPS) * g + b


def _mod_kernel(c_ref, w_ref, b_ref, o_ref):
    c = c_ref[...]
    s = (c * jax.nn.sigmoid(c)).astype(BF16)
    o_ref[0] = jnp.dot(s, w_ref[0].astype(BF16), preferred_element_type=F32) + b_ref[0]


def _mod_call(cond, w_mod, b_mod):
    depth, d, n = w_mod.shape
    rows = cond.shape[0]
    tn = 1024
    return pl.pallas_call(
        _mod_kernel,
        grid=(depth, n // tn),
        in_specs=[
            pl.BlockSpec((rows, d), lambda l, j: (0, 0)),
            pl.BlockSpec((1, d, tn), lambda l, j: (l, 0, j)),
            pl.BlockSpec((1, 1, tn), lambda l, j: (l, 0, j)),
        ],
        out_specs=pl.BlockSpec((1, rows, tn), lambda l, j: (l, 0, j)),
        out_shape=jax.ShapeDtypeStruct((depth, rows, n), F32),
        compiler_params=_cparams(("arbitrary", "arbitrary")),
        name="adaln_mod",
    )(cond, w_mod, b_mod.reshape(depth, 1, n))


def _proj_kernel(*refs, n_flat, n_head_tiles, rope, gelu):
    x_ref, sh_ref, sc_ref, w_ref = refs[:4]
    pos = 4
    if rope is not None:
        cos_ref, sin_ref = refs[pos:pos + 2]
        pos += 2
    flat_ref = refs[pos] if n_flat else None
    pos += 1 if n_flat else 0
    head_refs = refs[pos:pos + n_head_tiles]
    h_ref = refs[pos + n_head_tiles]
    j = pl.program_id(2)

    @pl.when(j == 0)
    def _():
        h_ref[...] = (x_ref[0] * (1.0 + sc_ref[0]) + sh_ref[0]).astype(BF16)

    tn = w_ref.shape[1]
    halves = [slice(c * (tn // 2), (c + 1) * (tn // 2)) for c in range(2)]

    def product(cols):
        return jnp.dot(h_ref[...], w_ref[:, cols], preferred_element_type=F32)

    if n_flat:
        @pl.when(j < n_flat)
        def _():
            for cols in halves:
                acc = product(cols)
                flat_ref[0, :, cols] = (jax.nn.gelu(acc, approximate=True) if gelu else acc).astype(flat_ref.dtype)

    for t in range(n_head_tiles):
        tile = n_flat + t
        o_ref = head_refs[t]

        @pl.when(j == tile)
        def _(tile=tile, o_ref=o_ref):
            rotary = rope is not None and tile in rope[0]
            if rotary:
                scale = rope[2] if tile == rope[1] else 1.0
                cs = cos_ref[...] * scale
                sn = sin_ref[...] * scale
            for cols in halves:
                acc = product(cols)
                for h in range(acc.shape[1] // LANES):
                    piece = acc[:, h * LANES:(h + 1) * LANES]
                    if rotary:
                        piece = piece * cs + pltpu.roll(piece, LANES // 2, axis=1) * sn
                    o_ref[0, cols.start // LANES + h] = piece.astype(o_ref.dtype)


def _proj_call(x, shift, scale, w, *, tm, tn, n_flat, rope=None, gelu=False):
    bsz, s, d = x.shape
    n_tiles = w.shape[1] // tn
    n_head_tiles = n_tiles - n_flat
    heads = tn // LANES
    tm = min(tm, s)
    per_batch = shift.shape[0] > 1
    mod_map = (lambda b, i, j: (b, 0, 0)) if per_batch else (lambda b, i, j: (0, 0, 0))
    in_specs = [
        pl.BlockSpec((1, tm, d), lambda b, i, j: (b, i, 0)),
        pl.BlockSpec((1, 1, d), mod_map),
        pl.BlockSpec((1, 1, d), mod_map),
        pl.BlockSpec((d, tn), lambda b, i, j: (0, j)),
    ]
    args = [x, shift, scale, w]
    rope_static = None
    if rope is not None:
        cos_t, sin_t, rope_tiles, q_tile, q_scale = rope
        in_specs += [pl.BlockSpec((tm, LANES), lambda b, i, j: (i, 0))] * 2
        args += [cos_t, sin_t]
        rope_static = (tuple(rope_tiles), q_tile, q_scale)
    out_specs, out_shape = [], []
    if n_flat:
        out_specs.append(pl.BlockSpec((1, tm, tn), lambda b, i, j: (b, i, jnp.minimum(j, n_flat - 1))))
        out_shape.append(jax.ShapeDtypeStruct((bsz, s, n_flat * tn), BF16))
    for _ in range(n_head_tiles):
        out_specs.append(pl.BlockSpec((1, heads, tm, LANES), lambda b, i, j: (b, 0, i, 0)))
        out_shape.append(jax.ShapeDtypeStruct((bsz, heads, s, LANES), BF16))
    return pl.pallas_call(
        functools.partial(_proj_kernel, n_flat=n_flat, n_head_tiles=n_head_tiles, rope=rope_static, gelu=gelu),
        grid=(bsz, s // tm, n_tiles),
        in_specs=in_specs,
        out_specs=out_specs,
        out_shape=out_shape,
        scratch_shapes=[pltpu.VMEM((tm, d), BF16)],
        compiler_params=_cparams(("arbitrary", "arbitrary", "arbitrary")),
        name="mod_proj",
    )(*args)


ONES_ROWS = 16


def _attn_kernel(lam_ref, g_ref, q_ref, kc_ref, vc_ref, k_ref, v_ref, o_ref, kf_ref, vt_ref, s_ref, p_ref, acc_ref,
                 *, tk, lam_init):
    tq = q_ref.shape[2]
    c_len = kc_ref.shape[2]
    s_len = k_ref.shape[2]
    n_chunks = (c_len + s_len) // tk
    vt_chunk = 512

    @pl.when(pl.program_id(2) == 0)
    def _():
        kf_ref[:c_len, :] = kc_ref[0, 0]
        kf_ref[c_len:, :] = k_ref[0, 0]
        vt_ref[LANES:, :] = jnp.ones((ONES_ROWS, vt_ref.shape[1]), BF16)
        vt_ref[:LANES, :c_len] = jnp.transpose(vc_ref[0, 0].astype(F32)).astype(BF16)
        for n in range(s_len // vt_chunk):
            rows = slice(n * vt_chunk, (n + 1) * vt_chunk)
            cols = slice(c_len + n * vt_chunk, c_len + (n + 1) * vt_chunk)
            vt_ref[:LANES, cols] = jnp.transpose(v_ref[0, 0, rows, :].astype(F32)).astype(BF16)

    q_t = jnp.transpose(q_ref[0, 0].astype(F32))
    dim = lax.broadcasted_iota(jnp.int32, q_t.shape, 0)
    first_map = (dim % (LANES // 2)) < (LANES // 4)
    qs_t = jnp.concatenate([jnp.where(first_map, q_t, 0.0), jnp.where(first_map, 0.0, q_t)], axis=1).astype(BF16)

    acc_ref[...] = jnp.zeros(acc_ref.shape, F32)

    def scores(t):
        off = t * tk if isinstance(t, int) else pl.multiple_of(t * tk, tk)
        return jnp.dot(kf_ref[pl.ds(off, tk), :], qs_t, preferred_element_type=F32)

    def softmax(slot, m_old):
        s_t = s_ref[slot]
        m_new = jnp.maximum(m_old, jnp.max(s_t, axis=0, keepdims=True))
        p_ref[slot] = jnp.exp2((s_t - m_new).astype(BF16))
        return m_new, jnp.exp2(m_old - m_new)

    def values(t, slot, alpha):
        off = t * tk if isinstance(t, int) else pl.multiple_of(t * tk, tk)
        pv = jnp.dot(vt_ref[:, pl.ds(off, tk)], p_ref[slot], preferred_element_type=F32)
        acc_ref[...] = alpha * acc_ref[...] + pv

    def tick(t, parity, m, alpha_prev):
        s_ref[parity] = scores(t)
        m, alpha = softmax(1 - parity, m)
        values(t - 2, parity, alpha_prev)
        return m, alpha

    s_ref[0] = scores(0)
    s_ref[1] = scores(1)
    m, alpha = softmax(0, jnp.full((1, 2 * tq), -jnp.inf, F32))

    def pair(jj, carry):
        m, alpha = tick(2 + 2 * jj, 0, *carry)
        return tick(3 + 2 * jj, 1, m, alpha)

    n_full = n_chunks - 2
    m, alpha = lax.fori_loop(0, n_full // 2, pair, (m, alpha))
    if n_full % 2:
        m, alpha = tick(n_chunks - 1, (n_chunks - 1) % 2, m, alpha)
    last = (n_chunks - 1) % 2
    m, alpha_last = softmax(last, m)
    values(n_chunks - 2, 1 - last, alpha)
    values(n_chunks - 1, last, alpha_last)

    lp = lam_ref[...]
    lam = (jnp.exp(jnp.sum(lp[0:1] * lp[1:2], axis=-1, keepdims=True))
           - jnp.exp(jnp.sum(lp[2:3] * lp[3:4], axis=-1, keepdims=True)) + lam_init)
    acc = acc_ref[...]
    o_t = acc[:LANES] / acc[LANES:LANES + 1]
    o_t = o_t[:, :tq] - lam * o_t[:, tq:]
    o_t = o_t * lax.rsqrt(jnp.mean(o_t * o_t, axis=0, keepdims=True) + LN_EPS)
    o_ref[0, 0] = (jnp.transpose(o_t) * g_ref[...] * (1.0 - lam_init)).astype(o_ref.dtype)


def _attn_call(q, k, v, kc, vc, lam_params, subln_g, *, lam_init, tq):
    bsz, h, s, _ = q.shape
    c_len = kc.shape[2]
    tq = min(tq, s)
    tk = max(t for t in (MXU_DEPTH, 2 * MXU_DEPTH, 3 * MXU_DEPTH) if (c_len + s) % t == 0)
    assert (c_len + s) // tk >= 3 and s % 512 == 0
    whole = lambda n: pl.BlockSpec((1, 1, n, LANES), lambda b, hh, i: (b, hh, 0, 0))
    return pl.pallas_call(
        functools.partial(_attn_kernel, tk=tk, lam_init=lam_init),
        grid=(bsz, h, s // tq),
        in_specs=[
            pl.BlockSpec((8, LANES), lambda b, hh, i: (0, 0)),
            pl.BlockSpec((1, LANES), lambda b, hh, i: (0, 0)),
            pl.BlockSpec((1, 1, tq, LANES), lambda b, hh, i: (b, hh, i, 0)),
            whole(c_len), whole(c_len), whole(s), whole(s),
        ],
        out_specs=pl.BlockSpec((1, 1, tq, LANES), lambda b, hh, i: (b, hh, i, 0)),
        out_shape=jax.ShapeDtypeStruct((bsz, h, s, LANES), BF16),
        scratch_shapes=[
            pltpu.VMEM((c_len + s, LANES), BF16),
            pltpu.VMEM((LANES + ONES_ROWS, c_len + s), BF16),
            pltpu.VMEM((2, tk, 2 * tq), F32),
            pltpu.VMEM((2, tk, 2 * tq), BF16),
            pltpu.VMEM((LANES + ONES_ROWS, 2 * tq), F32),
        ],
        compiler_params=_cparams(("arbitrary", "arbitrary", "arbitrary")),
        name="diff_attn",
    )(lam_params, subln_g, q, kc, vc, k, v)


def _gated_conv(xa_ref, bg_ref, cg_ref, xap_ref, cgp_ref, xan_ref, cgn_ref, w_ref):
    i = pl.program_id(1)
    tm = xa_ref.shape[1]
    halo = xap_ref.shape[1]
    z = xa_ref[0].astype(F32) * cg_ref[0].astype(F32)
    z_before = (xap_ref[0].astype(F32) * cgp_ref[0].astype(F32))[halo - 1:halo]
    z_after = (xan_ref[0].astype(F32) * cgn_ref[0].astype(F32))[0:1]
    z_before = jnp.where(i == 0, 0.0, z_before)
    z_after = jnp.where(i == pl.num_programs(1) - 1, 0.0, z_after)
    row = lax.broadcasted_iota(jnp.int32, z.shape, 0)
    z_prev = jnp.where(row == 0, z_before, pltpu.roll(z, 1, axis=0))
    z_next = jnp.where(row == tm - 1, z_after, pltpu.roll(z, tm - 1, axis=0))
    w = w_ref[...]
    conv = w[0:1] * z_prev + w[1:2] * z + w[2:3] * z_next
    return bg_ref[0].astype(F32) * conv


def _gated_conv_operands(proj, conv_w, *, width, tm):
    s = proj.shape[1]
    halo = 16
    nh = tm // halo
    last = s // halo - 1
    main = lambda c: pl.BlockSpec((1, tm, width), lambda b, i: (b, i, c))
    prev = lambda c: pl.BlockSpec((1, halo, width), lambda b, i: (b, jnp.maximum(i * nh - 1, 0), c))
    nxt = lambda c: pl.BlockSpec((1, halo, width), lambda b, i: (b, jnp.minimum((i + 1) * nh, last), c))
    w_pad = jnp.zeros((8, width), F32).at[:conv_w.shape[0]].set(conv_w)
    specs = [main(0), main(1), main(2), prev(0), prev(2), nxt(0), nxt(2), pl.BlockSpec((8, width), lambda b, i: (0, 0))]
    return specs, [proj] * 7 + [w_pad]


def _gmlp_gate(u_ref, v_ref, g_ref, b_ref, ws_ref, bs_ref, a_ref):
    tm = u_ref.shape[1]
    v = _layer_norm(v_ref[0].astype(F32), g_ref[...], b_ref[...]).astype(BF16)
    bs = bs_ref[...]
    for n in range(tm // CHUNK):
        rows = slice(n * CHUNK, (n + 1) * CHUNK)
        for g in range(GMLP_GROUPS):
            cols = slice(g * LANES, (g + 1) * LANES)
            sg = jnp.dot(ws_ref[g], v[rows, cols], preferred_element_type=F32) + bs[:, g:g + 1]
            a_ref[rows, cols] = (u_ref[0, rows, cols].astype(F32) * sg).astype(a_ref.dtype)


def _out_ln_kernel(*refs, alpha, mixer):
    if mixer == "conv_attn":
        conv_refs, (heads_ref, w_ref, x_ref, gate_ref, g_ref, b_ref, o_ref) = refs[:8], refs[8:]
        a = jnp.concatenate([_gated_conv(*conv_refs).astype(BF16)]
                            + [heads_ref[0, h] for h in range(heads_ref.shape[1])], axis=1)
    else:
        gmlp_refs, (w_ref, x_ref, gate_ref, g_ref, b_ref, o_ref, a_ref) = refs[:6], refs[6:]
        _gmlp_gate(*gmlp_refs, a_ref)
        a = a_ref[...]
    half = a.shape[0] // 2
    for rows in (slice(0, half), slice(half, 2 * half)):
        y = jnp.dot(a[rows], w_ref[...], preferred_element_type=F32)
        r = alpha * x_ref[0, rows, :] + gate_ref[0] * y
        o_ref[0, rows, :] = _layer_norm(r, g_ref[...], b_ref[...])


def _out_ln_call(mixer, mixer_specs, mixer_args, w, x, gate, ln_g, ln_b, *, alpha, tm):
    bsz, s, d = x.shape
    scratch = [pltpu.VMEM((tm, w.shape[0]), BF16)] if mixer == "gmlp" else []
    return pl.pallas_call(
        functools.partial(_out_ln_kernel, alpha=alpha, mixer=mixer),
        grid=(bsz, s // tm),
        in_specs=mixer_specs + [
            pl.BlockSpec(w.shape, lambda b, i: (0, 0)),
            pl.BlockSpec((1, tm, d), lambda b, i: (b, i, 0)),
            pl.BlockSpec((1, 1, d), lambda b, i: (b, 0, 0)),
            pl.BlockSpec((1, d), lambda b, i: (0, 0)),
            pl.BlockSpec((1, d), lambda b, i: (0, 0)),
        ],
        out_specs=pl.BlockSpec((1, tm, d), lambda b, i: (b, i, 0)),
        out_shape=jax.ShapeDtypeStruct((bsz, s, d), F32),
        scratch_shapes=scratch,
        compiler_params=_cparams(("arbitrary", "arbitrary")),
        name="out_proj_ln",
    )(*mixer_args, w, x, gate, ln_g, ln_b)


def _conv_attn_operands(proj, conv_w, attn, *, tm):
    specs, args = _gated_conv_operands(proj, conv_w, width=conv_w.shape[-1], tm=tm)
    specs.append(pl.BlockSpec((1, attn.shape[1], tm, LANES), lambda b, i: (b, 0, i, 0)))
    return specs, args + [attn]


def _gmlp_operands(uv, v_g, v_b, w_s, b_s_t, *, tm):
    d = uv.shape[2] // 2
    specs = [
        pl.BlockSpec((1, tm, d), lambda b, i: (b, i, 0)),
        pl.BlockSpec((1, tm, d), lambda b, i: (b, i, 1)),
        pl.BlockSpec((1, d), lambda b, i: (0, 0)),
        pl.BlockSpec((1, d), lambda b, i: (0, 0)),
        pl.BlockSpec(w_s.shape, lambda b, i: (0, 0, 0)),
        pl.BlockSpec(b_s_t.shape, lambda b, i: (0, 0)),
    ]
    return specs, [uv, uv, v_g, v_b, w_s, b_s_t]


def _route(scores, bias):
    sel = scores + bias
    rows = [sel[e:e + 1, :] for e in range(N_EXPERTS)]
    group_score = []
    for g in range(N_GROUPS):
        r = rows[g * EXPERTS_PER_GROUP:(g + 1) * EXPERTS_PER_GROUP]
        best = None
        for a in range(EXPERTS_PER_GROUP):
            for b in range(a + 1, EXPERTS_PER_GROUP):
                pair = r[a] + r[b]
                best = pair if best is None else jnp.maximum(best, pair)
        group_score.append(best)
    one = jnp.ones_like(rows[0])
    zero = jnp.zeros_like(rows[0])
    picked = []
    chosen = []
    for g in range(N_GROUPS):
        beaten = zero
        for o in range(N_GROUPS):
            if o < g:
                beaten = jnp.where(group_score[o] >= group_score[g], one, beaten)
            elif o > g:
                beaten = jnp.where(group_score[o] > group_score[g], one, beaten)
        chosen.append(1.0 - beaten)
        for a in range(EXPERTS_PER_GROUP):
            e = g * EXPERTS_PER_GROUP + a
            rank = zero
            for b in range(EXPERTS_PER_GROUP):
                o = g * EXPERTS_PER_GROUP + b
                if b < a:
                    rank = rank + jnp.where(rows[o] >= rows[e], one, zero)
                elif b > a:
                    rank = rank + jnp.where(rows[o] > rows[e], one, zero)
            picked.append(jnp.where(rank < 2.0, one, zero) * chosen[g])
    w = [picked[e] * scores[e:e + 1, :] for e in range(N_EXPERTS)]
    total = w[0]
    for e in range(1, N_EXPERTS):
        total = total + w[e]
    return [we / total for we in w], chosen


ROW_BLOCK = 32
STEP_BLOCKS = 8
DEST_LANE = EXPERTS_PER_GROUP
SORT_TM = 512


def _sort_rows(tm):
    return tm + N_GROUPS * ROW_BLOCK


def _moe_sort_kernel(x_ref, sh_ref, sc_ref, wr_ref, rb_ref, xs_ref, cs_ref, tok_ref, seg_ref):
    tm, d = x_ref.shape[1], x_ref.shape[2]
    rows = xs_ref.shape[0]
    col_chunk = 512
    h = (x_ref[0] * (1.0 + sc_ref[0]) + sh_ref[0]).astype(BF16)
    logits = lax.dot_general(wr_ref[...], h, (((1,), (1,)), ((), ())), preferred_element_type=F32)
    comb, chosen = _route(jax.nn.sigmoid(logits), rb_ref[...])
    src = lax.broadcasted_iota(jnp.int32, (tm, tm), 0)
    dst = lax.broadcasted_iota(jnp.int32, (tm, tm), 1)
    before = jnp.where(src < dst, 1.0, 0.0).astype(BF16)
    pad_rows = [jnp.zeros_like(chosen[0])] * (8 - N_GROUPS)
    rank = jnp.dot(jnp.concatenate(chosen + pad_rows, axis=0).astype(BF16), before, preferred_element_type=F32)
    dest = jnp.zeros_like(chosen[0])
    first = jnp.zeros((1, 1), F32)
    firsts, counts = [], []
    for g in range(N_GROUPS):
        count = jnp.sum(chosen[g], axis=1, keepdims=True)
        n_blocks = jnp.floor((count + (ROW_BLOCK - 1)) * (1.0 / ROW_BLOCK))
        firsts.append(first)
        counts.append(n_blocks)
        dest = dest + chosen[g] * (first * ROW_BLOCK + rank[g:g + 1])
        first = first + n_blocks
    seg = jnp.concatenate([jnp.broadcast_to(v, (1, LANES)) for v in firsts + counts], axis=0)
    seg_ref[0] = seg.astype(jnp.int32)
    dest_i = dest.astype(jnp.int32)
    row_id = lax.broadcasted_iota(jnp.int32, (rows, tm), 0)
    p = jnp.where(row_id == dest_i, 1.0, 0.0).astype(BF16)
    for c in range(d // col_chunk):
        cols = slice(c * col_chunk, (c + 1) * col_chunk)
        xs_ref[:, cols] = jnp.dot(p, h[:, cols], preferred_element_type=F32).astype(BF16)
    in_group = []
    for j in range(EXPERTS_PER_GROUP):
        cj = comb[j]
        for g in range(1, N_GROUPS):
            cj = cj + comb[g * EXPERTS_PER_GROUP + j]
        in_group.append(cj)
    c_rows = jnp.concatenate(in_group + [dest, jnp.zeros((LANES - DEST_LANE - 1, tm), F32)], axis=0)
    c_cols = jnp.transpose(c_rows)
    tok_ref[...] = c_cols
    hi = c_cols.astype(BF16)
    lo = (c_cols - hi.astype(F32)).astype(BF16)
    cs = jnp.dot(p, jnp.concatenate([hi, lo], axis=1), preferred_element_type=F32)
    cs_ref[...] = cs[:, :LANES] + cs[:, LANES:]


def _moe_sort_call(x, shift, scale, w_router_t, router_bias, *, tm):
    bsz, s, d = x.shape
    nt = s // tm
    n_tiles = bsz * nt
    rows = _sort_rows(tm)
    tile = lambda b, i: b * nt + i
    vec = lambda: pl.BlockSpec((1, 1, d), lambda b, i: (b, 0, 0))
    return pl.pallas_call(
        _moe_sort_kernel,
        grid=(bsz, nt),
        in_specs=[
            pl.BlockSpec((1, tm, d), lambda b, i: (b, i, 0)),
            vec(), vec(),
            pl.BlockSpec(w_router_t.shape, lambda b, i: (0, 0)),
            pl.BlockSpec(router_bias.shape, lambda b, i: (0, 0)),
        ],
        out_specs=[
            pl.BlockSpec((rows, d), lambda b, i: (tile(b, i), 0)),
            pl.BlockSpec((rows, LANES), lambda b, i: (tile(b, i), 0)),
            pl.BlockSpec((tm, LANES), lambda b, i: (tile(b, i), 0)),
            pl.BlockSpec((1, 2 * N_GROUPS, LANES), lambda b, i: (tile(b, i), 0, 0)),
        ],
        out_shape=[
            jax.ShapeDtypeStruct(((n_tiles + 1) * rows, d), BF16),
            jax.ShapeDtypeStruct(((n_tiles + 1) * rows, LANES), F32),
            jax.ShapeDtypeStruct((n_tiles * tm, LANES), F32),
            jax.ShapeDtypeStruct((n_tiles, 2 * N_GROUPS, LANES), jnp.int32),
        ],
        compiler_params=_cparams(("arbitrary", "arbitrary")),
        name="moe_sort",
    )(x, shift, scale, w_router_t, router_bias)


def _expert_work_table(seg, rows):
    n_tiles = seg.shape[0]
    per_tile = rows // ROW_BLOCK
    spare = n_tiles * per_tile
    max_items = spare // STEP_BLOCKS + N_GROUPS
    first = seg[:, :N_GROUPS, 0]
    count = seg[:, N_GROUPS:, 0]
    tri = jnp.arange(n_tiles)[:, None] >= jnp.arange(n_tiles)[None, :]
    cum = jnp.sum(jnp.where(tri[:, :, None], count[None, :, :], 0), axis=1)
    total = cum[-1]
    items = (total + (STEP_BLOCKS - 1)) // STEP_BLOCKS
    ends = jnp.sum(jnp.where(jnp.arange(N_GROUPS)[:, None] >= jnp.arange(N_GROUPS)[None, :], items[None, :], 0),
                   axis=1)
    n_items = ends[-1]
    step = jnp.arange(max_items + 1)
    live = step < n_items
    ref_step = jnp.where(live, step, jnp.maximum(n_items - 1, 0))
    grp = jnp.sum((ref_step[:, None] >= ends[None, :-1]).astype(jnp.int32), axis=1)
    pick = grp[:, None] == jnp.arange(N_GROUPS)[None, :]
    of_group = lambda v: jnp.sum(jnp.where(pick, v[None, :], 0), axis=1)
    start = of_group(ends - items)
    cum_s = jnp.sum(jnp.where(pick[:, None, :], cum[None, :, :], 0), axis=2)
    count_s = jnp.sum(jnp.where(pick[:, None, :], count[None, :, :], 0), axis=2)
    first_s = jnp.sum(jnp.where(pick[:, None, :], first[None, :, :], 0), axis=2)
    total_s = of_group(total)

    def block_id(k):
        done = cum_s <= k[:, None]
        t = jnp.sum(done.astype(jnp.int32), axis=1)
        skipped = jnp.sum(jnp.where(done, count_s, 0), axis=1)
        at_t = jnp.arange(n_tiles)[None, :] == t[:, None]
        return t * per_tile + jnp.sum(jnp.where(at_t, first_s, 0), axis=1) + (k - skipped)

    as_i32 = lambda v: v.astype(jnp.int32)
    block = jnp.arange(spare)[:, None]
    blocks = []
    where = jnp.zeros((spare,), jnp.int32)
    found = jnp.zeros((spare,), jnp.bool_)
    for u in range(STEP_BLOCKS):
        k_u = STEP_BLOCKS * (step - start) + u
        has_u = live & (k_u < total_s)
        blk_u = jnp.where(has_u, block_id(k_u), blocks[0] if u else spare)
        blocks.append(as_i32(blk_u))
        hit = has_u[None, :] & (blk_u[None, :] == block)
        where = where + jnp.sum(jnp.where(hit, STEP_BLOCKS * step[None, :] + u, 0), axis=1)
        found = found | jnp.any(hit, axis=1)
    where = jnp.where(found, where, STEP_BLOCKS * max_items)
    return (*blocks, as_i32(grp), as_i32(n_items).reshape(1)), as_i32(where)


def _moe_expert_kernel(*refs):
    n_items = refs[STEP_BLOCKS + 1]
    pos = STEP_BLOCKS + 2
    x_refs = refs[pos:pos + STEP_BLOCKS]
    c_refs = refs[pos + STEP_BLOCKS:pos + 2 * STEP_BLOCKS]
    wg_ref, wu_ref, wd_ref, y_ref = refs[pos + 2 * STEP_BLOCKS:]
    f, d = wd_ref.shape[2], wd_ref.shape[3]

    @pl.when(pl.program_id(0) < n_items[0])
    def _():
        x = jnp.concatenate([r[...] for r in x_refs], axis=0)
        c = jnp.concatenate([r[...] for r in c_refs], axis=0)
        pieces = []
        for j in range(EXPERTS_PER_GROUP):
            gate = jnp.dot(x, wg_ref[0, j], preferred_element_type=F32)
            up = jnp.dot(x, wu_ref[0, j], preferred_element_type=F32)
            pieces.append((jax.nn.silu(gate) * up * c[:, j:j + 1]).astype(BF16))
        w_down = wd_ref[0].reshape(EXPERTS_PER_GROUP * f, d)
        y = jnp.dot(jnp.concatenate(pieces, axis=1), w_down, preferred_element_type=F32)
        y_ref[...] = y.astype(BF16)

    @pl.when(pl.program_id(0) >= n_items[0])
    def _():
        y_ref[...] = jnp.zeros(y_ref.shape, BF16)


def _moe_expert_call(table, xs, cs, w_gate, w_up, w_down):
    d = xs.shape[1]
    steps = table[0].shape[0]
    n_pre = len(table)
    grp_at = STEP_BLOCKS
    rows_of = lambda u, width: pl.BlockSpec((ROW_BLOCK, width), lambda i, *pre: (pre[u][i], 0))
    weights = lambda w: pl.BlockSpec((1,) + w.shape[1:], lambda i, *pre: (pre[grp_at][i], 0, 0, 0),
                                     pipeline_mode=pl.Buffered(1))
    grid_spec = pltpu.PrefetchScalarGridSpec(
        num_scalar_prefetch=n_pre,
        grid=(steps,),
        in_specs=[rows_of(u, d) for u in range(STEP_BLOCKS)] + [rows_of(u, LANES) for u in range(STEP_BLOCKS)]
        + [weights(w_gate), weights(w_up), weights(w_down)],
        out_specs=pl.BlockSpec((STEP_BLOCKS * ROW_BLOCK, d), lambda i, *pre: (i, 0)),
    )
    return pl.pallas_call(
        _moe_expert_kernel,
        grid_spec=grid_spec,
        out_shape=jax.ShapeDtypeStruct((steps * STEP_BLOCKS * ROW_BLOCK, d), BF16),
        compiler_params=_cparams(("arbitrary",)),
        name="moe_experts",
    )(*table, *([xs] * STEP_BLOCKS), *([cs] * STEP_BLOCKS), w_gate, w_up, w_down)


def _moe_combine_kernel(where_ref, tok_ref, *refs, alpha, n_blocks):
    y_refs = refs[:n_blocks]
    x_ref, gate_ref, g_ref, b_ref, o_ref = refs[n_blocks:]
    tm = x_ref.shape[1]
    ys = jnp.concatenate([r[...] for r in y_refs], axis=0)
    half = tm // 2
    row_id = lax.broadcasted_iota(jnp.int32, (half, ys.shape[0]), 1)
    for rows in (slice(0, half), slice(half, tm)):
        dest = tok_ref[rows, DEST_LANE:DEST_LANE + 1].astype(jnp.int32)
        p_t = jnp.where(row_id == dest, 1.0, 0.0).astype(BF16)
        y = jnp.dot(p_t, ys, preferred_element_type=F32)
        r = alpha * x_ref[0, rows, :] + gate_ref[0] * y
        o_ref[0, rows, :] = _layer_norm(r, g_ref[...], b_ref[...])


def _moe_combine_call(where, tok, ys, x, gate, ln_g, ln_b, *, alpha, sort_tm, tm):
    bsz, s, d = x.shape
    nt = s // sort_tm
    parts = sort_tm // tm
    n_blocks = _sort_rows(sort_tm) // ROW_BLOCK
    tile = lambda b, i: b * nt + i
    y_spec = lambda j: pl.BlockSpec((ROW_BLOCK, d), lambda b, i, k, w: (w[tile(b, i) * n_blocks + j], 0))
    grid_spec = pltpu.PrefetchScalarGridSpec(
        num_scalar_prefetch=1,
        grid=(bsz, nt, parts),
        in_specs=[pl.BlockSpec((tm, LANES), lambda b, i, k, w: (tile(b, i) * parts + k, 0))]
        + [y_spec(j) for j in range(n_blocks)]
        + [
            pl.BlockSpec((1, tm, d), lambda b, i, k, w: (b, i * parts + k, 0)),
            pl.BlockSpec((1, 1, d), lambda b, i, k, w: (b, 0, 0)),
            pl.BlockSpec((1, d), lambda b, i, k, w: (0, 0)),
            pl.BlockSpec((1, d), lambda b, i, k, w: (0, 0)),
        ],
        out_specs=pl.BlockSpec((1, tm, d), lambda b, i, k, w: (b, i * parts + k, 0)),
    )
    return pl.pallas_call(
        functools.partial(_moe_combine_kernel, alpha=alpha, n_blocks=n_blocks),
        grid_spec=grid_spec,
        out_shape=jax.ShapeDtypeStruct((bsz, s, d), F32),
        compiler_params=_cparams(("arbitrary", "arbitrary", "arbitrary")),
        name="moe_combine",
    )(where, tok, *([ys] * n_blocks), x, gate, ln_g, ln_b)


def _moe(x, shift, scale, gate, w_router_t, router_bias, w_gate, w_up, w_down, ln_g, ln_b, *, alpha):
    sort_tm = min(SORT_TM, x.shape[1])
    xs, cs, tok, seg = _moe_sort_call(x, shift, scale, w_router_t, router_bias, tm=sort_tm)
    table, where = _expert_work_table(seg, _sort_rows(sort_tm))
    by_group = lambda w: w.reshape((N_GROUPS, EXPERTS_PER_GROUP) + w.shape[1:]).astype(BF16)
    ys = _moe_expert_call(table, xs, cs, by_group(w_gate), by_group(w_up), by_group(w_down))
    return _moe_combine_call(where, tok, ys, x, gate, ln_g, ln_b, alpha=alpha, sort_tm=sort_tm,
                             tm=min(ROW_TM, sort_tm))


def _rope_tables(n):
    freqs = LANES // 8
    inv = jnp.power(ROPE_BASE, -jnp.arange(freqs, dtype=F32) / freqs)
    tok = jnp.arange(n)
    ang_r = (tok // GRID_W).astype(F32)[:, None] * inv
    ang_c = (tok % GRID_W).astype(F32)[:, None] * inv
    ang = jnp.concatenate([ang_r, ang_c], axis=1)
    cos = jnp.tile(jnp.cos(ang), (1, 4))
    sin = jnp.tile(jnp.sin(ang), (1, 4))
    sign = jnp.where(jnp.arange(LANES) < LANES // 2, -1.0, 1.0).astype(F32)
    return cos, sin * sign


def _rope_column_perm():
    perm = np.zeros((DA_HEADS, 2, 2, 2, 16), np.int32)
    for h in range(DA_HEADS):
        for p in range(2):
            for m in range(2):
                for ax in range(2):
                    for f in range(16):
                        perm[h, p, m, ax, f] = h * LANES + m * 64 + ax * 32 + p * 16 + f
    return perm.reshape(-1)


def kernel(x, c, ctx, c_ctx, w_mod, b_mod, ln_g, ln_b, w_in_e, conv_w, lambda_q1, lambda_k1, lambda_q2, lambda_k2,
           subln_g, w_out_e, w_in_o, v_ln_g, v_ln_b, w_spatial, b_spatial, w_out_o, w_router, router_bias,
           w_gate, w_up, w_down):
    bsz, s, d = x.shape
    depth = w_mod.shape[0]
    assert depth == 2 and w_in_e.shape[0] == 1 and w_in_o.shape[0] == 1, "two-layer (even, odd) stack only"
    conv_dim = conv_w.shape[-1]
    q_dim = DA_HEADS * LANES
    q_col, k_col, v_col = 3 * conv_dim, 3 * conv_dim + q_dim, 3 * conv_dim + 2 * q_dim
    assert w_in_e.shape[2] == v_col + q_dim and s % GRID_W == 0
    alpha = float((2 * depth) ** 0.25)

    rows = 8 * ((bsz + 1 + 7) // 8)
    cond = jnp.zeros((rows, d), F32).at[:bsz].set(c).at[bsz].set(c_ctx)
    mods = _mod_call(cond, w_mod, b_mod)

    def mod_vec(l, k, ctx_row=False):
        v = mods[l, bsz:bsz + 1, k * d:(k + 1) * d] if ctx_row else mods[l, :bsz, k * d:(k + 1) * d]
        return v.reshape(-1, 1, d)

    w_router_t = w_router.T.astype(BF16)
    rbias = router_bias.reshape(-1, 1).astype(F32)

    perm = _rope_column_perm()
    w_in = w_in_e[0]
    w_in = jnp.concatenate(
        [w_in[:, :q_col], w_in[:, q_col:k_col][:, perm], w_in[:, k_col:v_col][:, perm], w_in[:, v_col:]],
        axis=1).astype(BF16)
    cos_t, sin_t = _rope_tables(s)
    tn = q_dim
    assert q_col % tn == 0
    q_tile, k_tile = q_col // tn, k_col // tn
    conv_proj, q, k, v = _proj_call(
        x, mod_vec(0, 0), mod_vec(0, 1), w_in, tm=PROJ_TM, tn=tn, n_flat=q_tile,
        rope=(cos_t, sin_t, (q_tile, k_tile), q_tile, float(64 ** -0.5 * math.log2(math.e))))
    kc, vc = _proj_call(ctx, mod_vec(0, 0, True), mod_vec(0, 1, True), w_in[:, k_col:], tm=PROJ_TM, tn=tn, n_flat=0)

    lam_init = 0.8 - 0.6 * math.exp(-0.3 * 0)
    lam_params = jnp.zeros((8, LANES), F32)
    for r, p in enumerate((lambda_q1, lambda_k1, lambda_q2, lambda_k2)):
        lam_params = lam_params.at[r, :p.shape[-1]].set(p[0].astype(F32))
    attn = _attn_call(q, k, v, kc, vc, lam_params, subln_g[0].reshape(1, LANES), lam_init=lam_init, tq=ATTN_TQ)
    x = _out_ln_call("conv_attn", *_conv_attn_operands(conv_proj, conv_w[0], attn, tm=ROW_TM), w_out_e[0].astype(BF16),
                     x, mod_vec(0, 2), ln_g[0, 0:1], ln_b[0, 0:1], alpha=alpha, tm=ROW_TM)
    x = _moe(x, mod_vec(0, 3), mod_vec(0, 4), mod_vec(0, 5), w_router_t, rbias,
             w_gate[0], w_up[0], w_down[0], ln_g[0, 1:2], ln_b[0, 1:2], alpha=alpha)

    w_gmlp = w_in_o[0].astype(BF16)
    uv, = _proj_call(x, mod_vec(1, 0), mod_vec(1, 1), w_gmlp, tm=PROJ_TM, tn=tn, n_flat=w_gmlp.shape[1] // tn,
                     gelu=True)
    gmlp = _gmlp_operands(uv, v_ln_g[0:1], v_ln_b[0:1], w_spatial[0].astype(BF16), b_spatial[0].T, tm=ROW_TM)
    x = _out_ln_call("gmlp", *gmlp, w_out_o[0].astype(BF16), x, mod_vec(1, 2), ln_g[1, 0:1], ln_b[1, 0:1],
                     alpha=alpha, tm=ROW_TM)
    x = _moe(x, mod_vec(1, 3), mod_vec(1, 4), mod_vec(1, 5), w_router_t, rbias,
             w_gate[1], w_up[1], w_down[1], ln_g[1, 1:2], ln_b[1, 1:2], alpha=alpha)
    return x
```

```python
import functools
import math

import numpy as np
import jax
import jax.numpy as jnp
from jax import lax
from jax.experimental import pallas as pl
from jax.experimental.pallas import tpu as pltpu

F32 = jnp.float32
BF16 = jnp.bfloat16

GRID_W = 64
DA_HEADS = 8
N_EXPERTS = 16
N_GROUPS = 4
EXPERTS_PER_GROUP = N_EXPERTS // N_GROUPS
ROPE_BASE = 10000.0
LN_EPS = 1e-5
CHUNK = 128
GMLP_GROUPS = 16
LANES = 128
MXU_DEPTH = 256

VMEM_LIMIT = 56 * 1024 * 1024
PROJ_TM = 1024
ROW_TM = 512
ATTN_TQ = 1024


def _cparams(sem):
    return pltpu.CompilerParams(dimension_semantics=sem, vmem_limit_bytes=VMEM_LIMIT)


def _layer_norm(r, g, b):
    mu = jnp.mean(r, axis=-1, keepdims=True)
    d = r - mu
    var = jnp.mean(d * d, axis=-1, keepdims=True)
    return d * lax.rsqrt(var + LN_EPS) * g + b


def _mod_kernel(c_ref, w_ref, b_ref, o_ref):
    c = c_ref[...]
    s = (c * jax.nn.sigmoid(c)).astype(BF16)
    o_ref[0] = jnp.dot(s, w_ref[0].astype(BF16), preferred_element_type=F32) + b_ref[0]


def _mod_call(cond, w_mod, b_mod):
    depth, d, n = w_mod.shape
    rows = cond.shape[0]
    tn = 1024
    return pl.pallas_call(
        _mod_kernel,
        grid=(depth, n // tn),
        in_specs=[
            pl.BlockSpec((rows, d), lambda l, j: (0, 0)),
            pl.BlockSpec((1, d, tn), lambda l, j: (l, 0, j)),
            pl.BlockSpec((1, 1, tn), lambda l, j: (l, 0, j)),
        ],
        out_specs=pl.BlockSpec((1, rows, tn), lambda l, j: (l, 0, j)),
        out_shape=jax.ShapeDtypeStruct((depth, rows, n), F32),
        compiler_params=_cparams(("arbitrary", "arbitrary")),
        name="adaln_mod",
    )(cond, w_mod, b_mod.reshape(depth, 1, n))


def _proj_kernel(*refs, n_flat, n_head_tiles, rope, gelu):
    x_ref, sh_ref, sc_ref, w_ref = refs[:4]
    pos = 4
    if rope is not None:
        cos_ref, sin_ref = refs[pos:pos + 2]
        pos += 2
    flat_ref = refs[pos] if n_flat else None
    pos += 1 if n_flat else 0
    head_refs = refs[pos:pos + n_head_tiles]
    h_ref = refs[pos + n_head_tiles]
    j = pl.program_id(2)

    @pl.when(j == 0)
    def _():
        h_ref[...] = (x_ref[0] * (1.0 + sc_ref[0]) + sh_ref[0]).astype(BF16)

    tn = w_ref.shape[1]
    halves = [slice(c * (tn // 2), (c + 1) * (tn // 2)) for c in range(2)]

    def product(cols):
        return jnp.dot(h_ref[...], w_ref[:, cols], preferred_element_type=F32)

    if n_flat:
        @pl.when(j < n_flat)
        def _():
            for cols in halves:
                acc = product(cols)
                flat_ref[0, :, cols] = (jax.nn.gelu(acc, approximate=True) if gelu else acc).astype(flat_ref.dtype)

    for t in range(n_head_tiles):
        tile = n_flat + t
        o_ref = head_refs[t]

        @pl.when(j == tile)
        def _(tile=tile, o_ref=o_ref):
            rotary = rope is not None and tile in rope[0]
            if rotary:
                scale = rope[2] if tile == rope[1] else 1.0
                cs = cos_ref[...] * scale
                sn = sin_ref[...] * scale
            for cols in halves:
                acc = product(cols)
                for h in range(acc.shape[1] // LANES):
                    piece = acc[:, h * LANES:(h + 1) * LANES]
                    if rotary:
                        piece = piece * cs + pltpu.roll(piece, LANES // 2, axis=1) * sn
                    o_ref[0, cols.start // LANES + h] = piece.astype(o_ref.dtype)


def _proj_call(x, shift, scale, w, *, tm, tn, n_flat, rope=None, gelu=False):
    bsz, s, d = x.shape
    n_tiles = w.shape[1] // tn
    n_head_tiles = n_tiles - n_flat
    heads = tn // LANES
    tm = min(tm, s)
    per_batch = shift.shape[0] > 1
    mod_map = (lambda b, i, j: (b, 0, 0)) if per_batch else (lambda b, i, j: (0, 0, 0))
    in_specs = [
        pl.BlockSpec((1, tm, d), lambda b, i, j: (b, i, 0)),
        pl.BlockSpec((1, 1, d), mod_map),
        pl.BlockSpec((1, 1, d), mod_map),
        pl.BlockSpec((d, tn), lambda b, i, j: (0, j)),
    ]
    args = [x, shift, scale, w]
    rope_static = None
    if rope is not None:
        cos_t, sin_t, rope_tiles, q_tile, q_scale = rope
        in_specs += [pl.BlockSpec((tm, LANES), lambda b, i, j: (i, 0))] * 2
        args += [cos_t, sin_t]
        rope_static = (tuple(rope_tiles), q_tile, q_scale)
    out_specs, out_shape = [], []
    if n_flat:
        out_specs.append(pl.BlockSpec((1, tm, tn), lambda b, i, j: (b, i, jnp.minimum(j, n_flat - 1))))
        out_shape.append(jax.ShapeDtypeStruct((bsz, s, n_flat * tn), BF16))
    for _ in range(n_head_tiles):
        out_specs.append(pl.BlockSpec((1, heads, tm, LANES), lambda b, i, j: (b, 0, i, 0)))
        out_shape.append(jax.ShapeDtypeStruct((bsz, heads, s, LANES), BF16))
    return pl.pallas_call(
        functools.partial(_proj_kernel, n_flat=n_flat, n_head_tiles=n_head_tiles, rope=rope_static, gelu=gelu),
        grid=(bsz, s // tm, n_tiles),
        in_specs=in_specs,
        out_specs=out_specs,
        out_shape=out_shape,
        scratch_shapes=[pltpu.VMEM((tm, d), BF16)],
        compiler_params=_cparams(("arbitrary", "arbitrary", "arbitrary")),
        name="mod_proj",
    )(*args)


ONES_ROWS = 16


def _attn_kernel(lam_ref, g_ref, q_ref, kc_ref, vc_ref, k_ref, v_ref, o_ref, kf_ref, vt_ref, s_ref, p_ref, acc_ref,
                 *, tk, lam_init):
    tq = q_ref.shape[2]
    c_len = kc_ref.shape[2]
    s_len = k_ref.shape[2]
    n_chunks = (c_len + s_len) // tk
    vt_chunk = 512

    @pl.when(pl.program_id(2) == 0)
    def _():
        kf_ref[:c_len, :] = kc_ref[0, 0]
        kf_ref[c_len:, :] = k_ref[0, 0]
        vt_ref[LANES:, :] = jnp.ones((ONES_ROWS, vt_ref.shape[1]), BF16)
        vt_ref[:LANES, :c_len] = jnp.transpose(vc_ref[0, 0].astype(F32)).astype(BF16)
        for n in range(s_len // vt_chunk):
            rows = slice(n * vt_chunk, (n + 1) * vt_chunk)
            cols = slice(c_len + n * vt_chunk, c_len + (n + 1) * vt_chunk)
            vt_ref[:LANES, cols] = jnp.transpose(v_ref[0, 0, rows, :].astype(F32)).astype(BF16)

    q_t = jnp.transpose(q_ref[0, 0].astype(F32))
    dim = lax.broadcasted_iota(jnp.int32, q_t.shape, 0)
    first_map = (dim % (LANES // 2)) < (LANES // 4)
    qs_t = jnp.concatenate([jnp.where(first_map, q_t, 0.0), jnp.where(first_map, 0.0, q_t)], axis=1).astype(BF16)

    acc_ref[...] = jnp.zeros(acc_ref.shape, F32)

    def scores(t):
        off = t * tk if isinstance(t, int) else pl.multiple_of(t * tk, tk)
        return jnp.dot(kf_ref[pl.ds(off, tk), :], qs_t, preferred_element_type=F32)

    def softmax(slot, m_old):
        s_t = s_ref[slot]
        m_new = jnp.maximum(m_old, jnp.max(s_t, axis=0, keepdims=True))
        p_ref[slot] = jnp.exp2((s_t - m_new).astype(BF16))
        return m_new, jnp.exp2(m_old - m_new)

    def values(t, slot, alpha):
        off = t * tk if isinstance(t, int) else pl.multiple_of(t * tk, tk)
        pv = jnp.dot(vt_ref[:, pl.ds(off, tk)], p_ref[slot], preferred_element_type=F32)
        acc_ref[...] = alpha * acc_ref[...] + pv

    def tick(t, parity, m, alpha_prev):
        s_ref[parity] = scores(t)
        m, alpha = softmax(1 - parity, m)
        values(t - 2, parity, alpha_prev)
        return m, alpha

    s_ref[0] = scores(0)
    s_ref[1] = scores(1)
    m, alpha = softmax(0, jnp.full((1, 2 * tq), -jnp.inf, F32))

    def pair(jj, carry):
        m, alpha = tick(2 + 2 * jj, 0, *carry)
        return tick(3 + 2 * jj, 1, m, alpha)

    n_full = n_chunks - 2
    m, alpha = lax.fori_loop(0, n_full // 2, pair, (m, alpha))
    if n_full % 2:
        m, alpha = tick(n_chunks - 1, (n_chunks - 1) % 2, m, alpha)
    last = (n_chunks - 1) % 2
    m, alpha_last = softmax(last, m)
    values(n_chunks - 2, 1 - last, alpha)
    values(n_chunks - 1, last, alpha_last)

    lp = lam_ref[...]
    lam = (jnp.exp(jnp.sum(lp[0:1] * lp[1:2], axis=-1, keepdims=True))
           - jnp.exp(jnp.sum(lp[2:3] * lp[3:4], axis=-1, keepdims=True)) + lam_init)
    acc = acc_ref[...]
    o_t = acc[:LANES] / acc[LANES:LANES + 1]
    o_t = o_t[:, :tq] - lam * o_t[:, tq:]
    o_t = o_t * lax.rsqrt(jnp.mean(o_t * o_t, axis=0, keepdims=True) + LN_EPS)
    o_ref[0, 0] = (jnp.transpose(o_t) * g_ref[...] * (1.0 - lam_init)).astype(o_ref.dtype)


def _attn_call(q, k, v, kc, vc, lam_params, subln_g, *, lam_init, tq):
    bsz, h, s, _ = q.shape
    c_len = kc.shape[2]
    tq = min(tq, s)
    tk = max(t for t in (MXU_DEPTH, 2 * MXU_DEPTH, 3 * MXU_DEPTH) if (c_len + s) % t == 0)
    assert (c_len + s) // tk >= 3 and s % 512 == 0
    whole = lambda n: pl.BlockSpec((1, 1, n, LANES), lambda b, hh, i: (b, hh, 0, 0))
    return pl.pallas_call(
        functools.partial(_attn_kernel, tk=tk, lam_init=lam_init),
        grid=(bsz, h, s // tq),
        in_specs=[
            pl.BlockSpec((8, LANES), lambda b, hh, i: (0, 0)),
            pl.BlockSpec((1, LANES), lambda b, hh, i: (0, 0)),
            pl.BlockSpec((1, 1, tq, LANES), lambda b, hh, i: (b, hh, i, 0)),
            whole(c_len), whole(c_len), whole(s), whole(s),
        ],
        out_specs=pl.BlockSpec((1, 1, tq, LANES), lambda b, hh, i: (b, hh, i, 0)),
        out_shape=jax.ShapeDtypeStruct((bsz, h, s, LANES), BF16),
        scratch_shapes=[
            pltpu.VMEM((c_len + s, LANES), BF16),
            pltpu.VMEM((LANES + ONES_ROWS, c_len + s), BF16),
            pltpu.VMEM((2, tk, 2 * tq), F32),
            pltpu.VMEM((2, tk, 2 * tq), BF16),
            pltpu.VMEM((LANES + ONES_ROWS, 2 * tq), F32),
        ],
        compiler_params=_cparams(("arbitrary", "arbitrary", "arbitrary")),
        name="diff_attn",
    )(lam_params, subln_g, q, kc, vc, k, v)


def _gated_conv(xa_ref, bg_ref, cg_ref, xap_ref, cgp_ref, xan_ref, cgn_ref, w_ref):
    i = pl.program_id(1)
    tm = xa_ref.shape[1]
    halo = xap_ref.shape[1]
    z = xa_ref[0].astype(F32) * cg_ref[0].astype(F32)
    z_before = (xap_ref[0].astype(F32) * cgp_ref[0].astype(F32))[halo - 1:halo]
    z_after = (xan_ref[0].astype(F32) * cgn_ref[0].astype(F32))[0:1]
    z_before = jnp.where(i == 0, 0.0, z_before)
    z_after = jnp.where(i == pl.num_programs(1) - 1, 0.0, z_after)
    row = lax.broadcasted_iota(jnp.int32, z.shape, 0)
    z_prev = jnp.where(row == 0, z_before, pltpu.roll(z, 1, axis=0))
    z_next = jnp.where(row == tm - 1, z_after, pltpu.roll(z, tm - 1, axis=0))
    w = w_ref[...]
    conv = w[0:1] * z_prev + w[1:2] * z + w[2:3] * z_next
    return bg_ref[0].astype(F32) * conv


def _gated_conv_operands(proj, conv_w, *, width, tm):
    s = proj.shape[1]
    halo = 16
    nh = tm // halo
    last = s // halo - 1
    main = lambda c: pl.BlockSpec((1, tm, width), lambda b, i: (b, i, c))
    prev = lambda c: pl.BlockSpec((1, halo, width), lambda b, i: (b, jnp.maximum(i * nh - 1, 0), c))
    nxt = lambda c: pl.BlockSpec((1, halo, width), lambda b, i: (b, jnp.minimum((i + 1) * nh, last), c))
    w_pad = jnp.zeros((8, width), F32).at[:conv_w.shape[0]].set(conv_w)
    specs = [main(0), main(1), main(2), prev(0), prev(2), nxt(0), nxt(2), pl.BlockSpec((8, width), lambda b, i: (0, 0))]
    return specs, [proj] * 7 + [w_pad]


def _gmlp_gate(u_ref, v_ref, g_ref, b_ref, ws_ref, bs_ref, a_ref):
    tm = u_ref.shape[1]
    v = _layer_norm(v_ref[0].astype(F32), g_ref[...], b_ref[...]).astype(BF16)
    bs = bs_ref[...]
    for n in range(tm // CHUNK):
        rows = slice(n * CHUNK, (n + 1) * CHUNK)
        for g in range(GMLP_GROUPS):
            cols = slice(g * LANES, (g + 1) * LANES)
            sg = jnp.dot(ws_ref[g], v[rows, cols], preferred_element_type=F32) + bs[:, g:g + 1]
            a_ref[rows, cols] = (u_ref[0, rows, cols].astype(F32) * sg).astype(a_ref.dtype)


def _out_ln_kernel(*refs, alpha, mixer):
    if mixer == "conv_attn":
        conv_refs, (heads_ref, w_ref, x_ref, gate_ref, g_ref, b_ref, o_ref) = refs[:8], refs[8:]
        a = jnp.concatenate([_gated_conv(*conv_refs).astype(BF16)]
                            + [heads_ref[0, h] for h in range(heads_ref.shape[1])], axis=1)
    else:
        gmlp_refs, (w_ref, x_ref, gate_ref, g_ref, b_ref, o_ref, a_ref) = refs[:6], refs[6:]
        _gmlp_gate(*gmlp_refs, a_ref)
        a = a_ref[...]
    half = a.shape[0] // 2
    for rows in (slice(0, half), slice(half, 2 * half)):
        y = jnp.dot(a[rows], w_ref[...], preferred_element_type=F32)
        r = alpha * x_ref[0, rows, :] + gate_ref[0] * y
        o_ref[0, rows, :] = _layer_norm(r, g_ref[...], b_ref[...])


def _out_ln_call(mixer, mixer_specs, mixer_args, w, x, gate, ln_g, ln_b, *, alpha, tm):
    bsz, s, d = x.shape
    scratch = [pltpu.VMEM((tm, w.shape[0]), BF16)] if mixer == "gmlp" else []
    return pl.pallas_call(
        functools.partial(_out_ln_kernel, alpha=alpha, mixer=mixer),
        grid=(bsz, s // tm),
        in_specs=mixer_specs + [
            pl.BlockSpec(w.shape, lambda b, i: (0, 0)),
            pl.BlockSpec((1, tm, d), lambda b, i: (b, i, 0)),
            pl.BlockSpec((1, 1, d), lambda b, i: (b, 0, 0)),
            pl.BlockSpec((1, d), lambda b, i: (0, 0)),
            pl.BlockSpec((1, d), lambda b, i: (0, 0)),
        ],
        out_specs=pl.BlockSpec((1, tm, d), lambda b, i: (b, i, 0)),
        out_shape=jax.ShapeDtypeStruct((bsz, s, d), F32),
        scratch_shapes=scratch,
        compiler_params=_cparams(("arbitrary", "arbitrary")),
        name="out_proj_ln",
    )(*mixer_args, w, x, gate, ln_g, ln_b)


def _conv_attn_operands(proj, conv_w, attn, *, tm):
    specs, args = _gated_conv_operands(proj, conv_w, width=conv_w.shape[-1], tm=tm)
    specs.append(pl.BlockSpec((1, attn.shape[1], tm, LANES), lambda b, i: (b, 0, i, 0)))
    return specs, args + [attn]


def _gmlp_operands(uv, v_g, v_b, w_s, b_s_t, *, tm):
    d = uv.shape[2] // 2
    specs = [
        pl.BlockSpec((1, tm, d), lambda b, i: (b, i, 0)),
        pl.BlockSpec((1, tm, d), lambda b, i: (b, i, 1)),
        pl.BlockSpec((1, d), lambda b, i: (0, 0)),
        pl.BlockSpec((1, d), lambda b, i: (0, 0)),
        pl.BlockSpec(w_s.shape, lambda b, i: (0, 0, 0)),
        pl.BlockSpec(b_s_t.shape, lambda b, i: (0, 0)),
    ]
    return specs, [uv, uv, v_g, v_b, w_s, b_s_t]


def _route(scores, bias):
    sel = scores + bias
    rows = [sel[e:e + 1, :] for e in range(N_EXPERTS)]
    group_score = []
    for g in range(N_GROUPS):
        r = rows[g * EXPERTS_PER_GROUP:(g + 1) * EXPERTS_PER_GROUP]
        best = None
        for a in range(EXPERTS_PER_GROUP):
            for b in range(a + 1, EXPERTS_PER_GROUP):
                pair = r[a] + r[b]
                best = pair if best is None else jnp.maximum(best, pair)
        group_score.append(best)
    one = jnp.ones_like(rows[0])
    zero = jnp.zeros_like(rows[0])
    picked = []
    chosen = []
    for g in range(N_GROUPS):
        beaten = zero
        for o in range(N_GROUPS):
            if o < g:
                beaten = jnp.where(group_score[o] >= group_score[g], one, beaten)
            elif o > g:
                beaten = jnp.where(group_score[o] > group_score[g], one, beaten)
        chosen.append(1.0 - beaten)
        for a in range(EXPERTS_PER_GROUP):
            e = g * EXPERTS_PER_GROUP + a
            rank = zero
            for b in range(EXPERTS_PER_GROUP):
                o = g * EXPERTS_PER_GROUP + b
                if b < a:
                    rank = rank + jnp.where(rows[o] >= rows[e], one, zero)
                elif b > a:
                    rank = rank + jnp.where(rows[o] > rows[e], one, zero)
            picked.append(jnp.where(rank < 2.0, one, zero) * chosen[g])
    w = [picked[e] * scores[e:e + 1, :] for e in range(N_EXPERTS)]
    total = w[0]
    for e in range(1, N_EXPERTS):
        total = total + w[e]
    return [we / total for we in w], chosen


ROW_BLOCK = 32
STEP_BLOCKS = 8
DEST_LANE = EXPERTS_PER_GROUP
SORT_TM = 512


def _sort_rows(tm):
    return tm + N_GROUPS * ROW_BLOCK


def _moe_sort_kernel(x_ref, sh_ref, sc_ref, wr_ref, rb_ref, xs_ref, cs_ref, tok_ref, seg_ref):
    tm, d = x_ref.shape[1], x_ref.shape[2]
    rows = xs_ref.shape[0]
    col_chunk = 512
    h = (x_ref[0] * (1.0 + sc_ref[0]) + sh_ref[0]).astype(BF16)
    logits = lax.dot_general(wr_ref[...], h, (((1,), (1,)), ((), ())), preferred_element_type=F32)
    comb, chosen = _route(jax.nn.sigmoid(logits), rb_ref[...])
    src = lax.broadcasted_iota(jnp.int32, (tm, tm), 0)
    dst = lax.broadcasted_iota(jnp.int32, (tm, tm), 1)
    before = jnp.where(src < dst, 1.0, 0.0).astype(BF16)
    pad_rows = [jnp.zeros_like(chosen[0])] * (8 - N_GROUPS)
    rank = jnp.dot(jnp.concatenate(chosen + pad_rows, axis=0).astype(BF16), before, preferred_element_type=F32)
    dest = jnp.zeros_like(chosen[0])
    first = jnp.zeros((1, 1), F32)
    firsts, counts = [], []
    for g in range(N_GROUPS):
        count = jnp.sum(chosen[g], axis=1, keepdims=True)
        n_blocks = jnp.floor((count + (ROW_BLOCK - 1)) * (1.0 / ROW_BLOCK))
        firsts.append(first)
        counts.append(n_blocks)
        dest = dest + chosen[g] * (first * ROW_BLOCK + rank[g:g + 1])
        first = first + n_blocks
    seg = jnp.concatenate([jnp.broadcast_to(v, (1, LANES)) for v in firsts + counts], axis=0)
    seg_ref[0] = seg.astype(jnp.int32)
    dest_i = dest.astype(jnp.int32)
    row_id = lax.broadcasted_iota(jnp.int32, (rows, tm), 0)
    p = jnp.where(row_id == dest_i, 1.0, 0.0).astype(BF16)
    for c in range(d // col_chunk):
        cols = slice(c * col_chunk, (c + 1) * col_chunk)
        xs_ref[:, cols] = jnp.dot(p, h[:, cols], preferred_element_type=F32).astype(BF16)
    in_group = []
    for j in range(EXPERTS_PER_GROUP):
        cj = comb[j]
        for g in range(1, N_GROUPS):
            cj = cj + comb[g * EXPERTS_PER_GROUP + j]
        in_group.append(cj)
    c_rows = jnp.concatenate(in_group + [dest, jnp.zeros((LANES - DEST_LANE - 1, tm), F32)], axis=0)
    c_cols = jnp.transpose(c_rows)
    tok_ref[...] = c_cols
    hi = c_cols.astype(BF16)
    lo = (c_cols - hi.astype(F32)).astype(BF16)
    cs = jnp.dot(p, jnp.concatenate([hi, lo], axis=1), preferred_element_type=F32)
    cs_ref[...] = cs[:, :LANES] + cs[:, LANES:]


def _moe_sort_call(x, shift, scale, w_router_t, router_bias, *, tm):
    bsz, s, d = x.shape
    nt = s // tm
    n_tiles = bsz * nt
    rows = _sort_rows(tm)
    tile = lambda b, i: b * nt + i
    vec = lambda: pl.BlockSpec((1, 1, d), lambda b, i: (b, 0, 0))
    return pl.pallas_call(
        _moe_sort_kernel,
        grid=(bsz, nt),
        in_specs=[
            pl.BlockSpec((1, tm, d), lambda b, i: (b, i, 0)),
            vec(), vec(),
            pl.BlockSpec(w_router_t.shape, lambda b, i: (0, 0)),
            pl.BlockSpec(router_bias.shape, lambda b, i: (0, 0)),
        ],
        out_specs=[
            pl.BlockSpec((rows, d), lambda b, i: (tile(b, i), 0)),
            pl.BlockSpec((rows, LANES), lambda b, i: (tile(b, i), 0)),
            pl.BlockSpec((tm, LANES), lambda b, i: (tile(b, i), 0)),
            pl.BlockSpec((1, 2 * N_GROUPS, LANES), lambda b, i: (tile(b, i), 0, 0)),
        ],
        out_shape=[
            jax.ShapeDtypeStruct(((n_tiles + 1) * rows, d), BF16),
            jax.ShapeDtypeStruct(((n_tiles + 1) * rows, LANES), F32),
            jax.ShapeDtypeStruct((n_tiles * tm, LANES), F32),
            jax.ShapeDtypeStruct((n_tiles, 2 * N_GROUPS, LANES), jnp.int32),
        ],
        compiler_params=_cparams(("arbitrary", "arbitrary")),
        name="moe_sort",
    )(x, shift, scale, w_router_t, router_bias)


def _expert_work_table(seg, rows):
    n_tiles = seg.shape[0]
    per_tile = rows // ROW_BLOCK
    spare = n_tiles * per_tile
    max_items = spare // STEP_BLOCKS + N_GROUPS
    first = seg[:, :N_GROUPS, 0]
    count = seg[:, N_GROUPS:, 0]
    tri = jnp.arange(n_tiles)[:, None] >= jnp.arange(n_tiles)[None, :]
    cum = jnp.sum(jnp.where(tri[:, :, None], count[None, :, :], 0), axis=1)
    total = cum[-1]
    items = (total + (STEP_BLOCKS - 1)) // STEP_BLOCKS
    ends = jnp.sum(jnp.where(jnp.arange(N_GROUPS)[:, None] >= jnp.arange(N_GROUPS)[None, :], items[None, :], 0),
                   axis=1)
    n_items = ends[-1]
    step = jnp.arange(max_items + 1)
    live = step < n_items
    ref_step = jnp.where(live, step, jnp.maximum(n_items - 1, 0))
    grp = jnp.sum((ref_step[:, None] >= ends[None, :-1]).astype(jnp.int32), axis=1)
    pick = grp[:, None] == jnp.arange(N_GROUPS)[None, :]
    of_group = lambda v: jnp.sum(jnp.where(pick, v[None, :], 0), axis=1)
    start = of_group(ends - items)
    cum_s = jnp.sum(jnp.where(pick[:, None, :], cum[None, :, :], 0), axis=2)
    count_s = jnp.sum(jnp.where(pick[:, None, :], count[None, :, :], 0), axis=2)
    first_s = jnp.sum(jnp.where(pick[:, None, :], first[None, :, :], 0), axis=2)
    total_s = of_group(total)

    def block_id(k):
        done = cum_s <= k[:, None]
        t = jnp.sum(done.astype(jnp.int32), axis=1)
        skipped = jnp.sum(jnp.where(done, count_s, 0), axis=1)
        at_t = jnp.arange(n_tiles)[None, :] == t[:, None]
        return t * per_tile + jnp.sum(jnp.where(at_t, first_s, 0), axis=1) + (k - skipped)

    as_i32 = lambda v: v.astype(jnp.int32)
    block = jnp.arange(spare)[:, None]
    blocks = []
    where = jnp.zeros((spare,), jnp.int32)
    found = jnp.zeros((spare,), jnp.bool_)
    for u in range(STEP_BLOCKS):
        k_u = STEP_BLOCKS * (step - start) + u
        has_u = live & (k_u < total_s)
        blk_u = jnp.where(has_u, block_id(k_u), blocks[0] if u else spare)
        blocks.append(as_i32(blk_u))
        hit = has_u[None, :] & (blk_u[None, :] == block)
        where = where + jnp.sum(jnp.where(hit, STEP_BLOCKS * step[None, :] + u, 0), axis=1)
        found = found | jnp.any(hit, axis=1)
    where = jnp.where(found, where, STEP_BLOCKS * max_items)
    return (*blocks, as_i32(grp), as_i32(n_items).reshape(1)), as_i32(where)


def _moe_expert_kernel(*refs):
    n_items = refs[STEP_BLOCKS + 1]
    pos = STEP_BLOCKS + 2
    x_refs = refs[pos:pos + STEP_BLOCKS]
    c_refs = refs[pos + STEP_BLOCKS:pos + 2 * STEP_BLOCKS]
    wg_ref, wu_ref, wd_ref, y_ref = refs[pos + 2 * STEP_BLOCKS:]
    f, d = wd_ref.shape[2], wd_ref.shape[3]

    @pl.when(pl.program_id(0) < n_items[0])
    def _():
        x = jnp.concatenate([r[...] for r in x_refs], axis=0)
        c = jnp.concatenate([r[...] for r in c_refs], axis=0)
        pieces = []
        for j in range(EXPERTS_PER_GROUP):
            gate = jnp.dot(x, wg_ref[0, j].astype(BF16), preferred_element_type=F32)
            up = jnp.dot(x, wu_ref[0, j].astype(BF16), preferred_element_type=F32)
            pieces.append((jax.nn.silu(gate) * up * c[:, j:j + 1]).astype(BF16))
        w_down = wd_ref[0].reshape(EXPERTS_PER_GROUP * f, d).astype(BF16)
        y = jnp.dot(jnp.concatenate(pieces, axis=1), w_down, preferred_element_type=F32)
        y_ref[...] = y.astype(BF16)

    @pl.when(pl.program_id(0) >= n_items[0])
    def _():
        y_ref[...] = jnp.zeros(y_ref.shape, BF16)


def _moe_expert_call(table, xs, cs, w_gate, w_up, w_down, layer):
    d = xs.shape[1]
    steps = table[0].shape[0]
    n_pre = len(table)
    grp_at = STEP_BLOCKS
    rows_of = lambda u, width: pl.BlockSpec((ROW_BLOCK, width), lambda i, *pre: (pre[u][i], 0))
    weights = lambda w: pl.BlockSpec((1,) + w.shape[1:],
                                     lambda i, *pre: (layer * N_GROUPS + pre[grp_at][i], 0, 0, 0),
                                     pipeline_mode=pl.Buffered(1))
    grid_spec = pltpu.PrefetchScalarGridSpec(
        num_scalar_prefetch=n_pre,
        grid=(steps,),
        in_specs=[rows_of(u, d) for u in range(STEP_BLOCKS)] + [rows_of(u, LANES) for u in range(STEP_BLOCKS)]
        + [weights(w_gate), weights(w_up), weights(w_down)],
        out_specs=pl.BlockSpec((STEP_BLOCKS * ROW_BLOCK, d), lambda i, *pre: (i, 0)),
    )
    return pl.pallas_call(
        _moe_expert_kernel,
        grid_spec=grid_spec,
        out_shape=jax.ShapeDtypeStruct((steps * STEP_BLOCKS * ROW_BLOCK, d), BF16),
        compiler_params=_cparams(("arbitrary",)),
        name="moe_experts",
    )(*table, *([xs] * STEP_BLOCKS), *([cs] * STEP_BLOCKS), w_gate, w_up, w_down)


def _moe_combine_kernel(where_ref, tok_ref, *refs, alpha, n_blocks):
    y_refs = refs[:n_blocks]
    x_ref, gate_ref, g_ref, b_ref, o_ref = refs[n_blocks:]
    tm = x_ref.shape[1]
    ys = jnp.concatenate([r[...] for r in y_refs], axis=0)
    half = tm // 2
    row_id = lax.broadcasted_iota(jnp.int32, (half, ys.shape[0]), 1)
    for rows in (slice(0, half), slice(half, tm)):
        dest = tok_ref[rows, DEST_LANE:DEST_LANE + 1].astype(jnp.int32)
        p_t = jnp.where(row_id == dest, 1.0, 0.0).astype(BF16)
        y = jnp.dot(p_t, ys, preferred_element_type=F32)
        r = alpha * x_ref[0, rows, :] + gate_ref[0] * y
        o_ref[0, rows, :] = _layer_norm(r, g_ref[...], b_ref[...])


def _moe_combine_call(where, tok, ys, x, gate, ln_g, ln_b, *, alpha, sort_tm, tm):
    bsz, s, d = x.shape
    nt = s // sort_tm
    parts = sort_tm // tm
    n_blocks = _sort_rows(sort_tm) // ROW_BLOCK
    tile = lambda b, i: b * nt + i
    y_spec = lambda j: pl.BlockSpec((ROW_BLOCK, d), lambda b, i, k, w: (w[tile(b, i) * n_blocks + j], 0))
    grid_spec = pltpu.PrefetchScalarGridSpec(
        num_scalar_prefetch=1,
        grid=(bsz, nt, parts),
        in_specs=[pl.BlockSpec((tm, LANES), lambda b, i, k, w: (tile(b, i) * parts + k, 0))]
        + [y_spec(j) for j in range(n_blocks)]
        + [
            pl.BlockSpec((1, tm, d), lambda b, i, k, w: (b, i * parts + k, 0)),
            pl.BlockSpec((1, 1, d), lambda b, i, k, w: (b, 0, 0)),
            pl.BlockSpec((1, d), lambda b, i, k, w: (0, 0)),
            pl.BlockSpec((1, d), lambda b, i, k, w: (0, 0)),
        ],
        out_specs=pl.BlockSpec((1, tm, d), lambda b, i, k, w: (b, i * parts + k, 0)),
    )
    return pl.pallas_call(
        functools.partial(_moe_combine_kernel, alpha=alpha, n_blocks=n_blocks),
        grid_spec=grid_spec,
        out_shape=jax.ShapeDtypeStruct((bsz, s, d), F32),
        compiler_params=_cparams(("arbitrary", "arbitrary", "arbitrary")),
        name="moe_combine",
    )(where, tok, *([ys] * n_blocks), x, gate, ln_g, ln_b)


def _moe(x, shift, scale, gate, w_router_t, router_bias, w_gate, w_up, w_down, layer, ln_g, ln_b, *, alpha):
    sort_tm = min(SORT_TM, x.shape[1])
    xs, cs, tok, seg = _moe_sort_call(x, shift, scale, w_router_t, router_bias, tm=sort_tm)
    table, where = _expert_work_table(seg, _sort_rows(sort_tm))
    by_group = lambda w: w.reshape((w.shape[0] * N_GROUPS, EXPERTS_PER_GROUP) + w.shape[2:])
    ys = _moe_expert_call(table, xs, cs, by_group(w_gate), by_group(w_up), by_group(w_down), layer)
    return _moe_combine_call(where, tok, ys, x, gate, ln_g, ln_b, alpha=alpha, sort_tm=sort_tm,
                             tm=min(ROW_TM, sort_tm))


def _rope_tables(n):
    freqs = LANES // 8
    inv = jnp.power(ROPE_BASE, -jnp.arange(freqs, dtype=F32) / freqs)
    tok = jnp.arange(n)
    ang_r = (tok // GRID_W).astype(F32)[:, None] * inv
    ang_c = (tok % GRID_W).astype(F32)[:, None] * inv
    ang = jnp.concatenate([ang_r, ang_c], axis=1)
    cos = jnp.tile(jnp.cos(ang), (1, 4))
    sin = jnp.tile(jnp.sin(ang), (1, 4))
    sign = jnp.where(jnp.arange(LANES) < LANES // 2, -1.0, 1.0).astype(F32)
    return cos, sin * sign


def _rope_column_perm():
    perm = np.zeros((DA_HEADS, 2, 2, 2, 16), np.int32)
    for h in range(DA_HEADS):
        for p in range(2):
            for m in range(2):
                for ax in range(2):
                    for f in range(16):
                        perm[h, p, m, ax, f] = h * LANES + m * 64 + ax * 32 + p * 16 + f
    return perm.reshape(-1)


def kernel(x, c, ctx, c_ctx, w_mod, b_mod, ln_g, ln_b, w_in_e, conv_w, lambda_q1, lambda_k1, lambda_q2, lambda_k2,
           subln_g, w_out_e, w_in_o, v_ln_g, v_ln_b, w_spatial, b_spatial, w_out_o, w_router, router_bias,
           w_gate, w_up, w_down):
    bsz, s, d = x.shape
    depth = w_mod.shape[0]
    assert depth == 2 and w_in_e.shape[0] == 1 and w_in_o.shape[0] == 1, "two-layer (even, odd) stack only"
    conv_dim = conv_w.shape[-1]
    q_dim = DA_HEADS * LANES
    q_col, k_col, v_col = 3 * conv_dim, 3 * conv_dim + q_dim, 3 * conv_dim + 2 * q_dim
    assert w_in_e.shape[2] == v_col + q_dim and s % GRID_W == 0
    alpha = float((2 * depth) ** 0.25)

    rows = 8 * ((bsz + 1 + 7) // 8)
    cond = jnp.zeros((rows, d), F32).at[:bsz].set(c).at[bsz].set(c_ctx)
    mods = _mod_call(cond, w_mod, b_mod)

    def mod_vec(l, k, ctx_row=False):
        v = mods[l, bsz:bsz + 1, k * d:(k + 1) * d] if ctx_row else mods[l, :bsz, k * d:(k + 1) * d]
        return v.reshape(-1, 1, d)

    w_router_t = w_router.T.astype(BF16)
    rbias = router_bias.reshape(-1, 1).astype(F32)

    perm = _rope_column_perm()
    w_in = w_in_e[0]
    w_in = jnp.concatenate(
        [w_in[:, :q_col], w_in[:, q_col:k_col][:, perm], w_in[:, k_col:v_col][:, perm], w_in[:, v_col:]],
        axis=1).astype(BF16)
    cos_t, sin_t = _rope_tables(s)
    tn = q_dim
    assert q_col % tn == 0
    q_tile, k_tile = q_col // tn, k_col // tn
    conv_proj, q, k, v = _proj_call(
        x, mod_vec(0, 0), mod_vec(0, 1), w_in, tm=PROJ_TM, tn=tn, n_flat=q_tile,
        rope=(cos_t, sin_t, (q_tile, k_tile), q_tile, float(64 ** -0.5 * math.log2(math.e))))
    kc, vc = _proj_call(ctx, mod_vec(0, 0, True), mod_vec(0, 1, True), w_in[:, k_col:], tm=PROJ_TM, tn=tn, n_flat=0)

    lam_init = 0.8 - 0.6 * math.exp(-0.3 * 0)
    lam_params = jnp.zeros((8, LANES), F32)
    for r, p in enumerate((lambda_q1, lambda_k1, lambda_q2, lambda_k2)):
        lam_params = lam_params.at[r, :p.shape[-1]].set(p[0].astype(F32))
    attn = _attn_call(q, k, v, kc, vc, lam_params, subln_g[0].reshape(1, LANES), lam_init=lam_init, tq=ATTN_TQ)
    x = _out_ln_call("conv_attn", *_conv_attn_operands(conv_proj, conv_w[0], attn, tm=ROW_TM), w_out_e[0].astype(BF16),
                     x, mod_vec(0, 2), ln_g[0, 0:1], ln_b[0, 0:1], alpha=alpha, tm=ROW_TM)
    x = _moe(x, mod_vec(0, 3), mod_vec(0, 4), mod_vec(0, 5), w_router_t, rbias,
             w_gate, w_up, w_down, 0, ln_g[0, 1:2], ln_b[0, 1:2], alpha=alpha)

    w_gmlp = w_in_o[0].astype(BF16)
    uv, = _proj_call(x, mod_vec(1, 0), mod_vec(1, 1), w_gmlp, tm=PROJ_TM, tn=tn, n_flat=w_gmlp.shape[1] // tn,
                     gelu=True)
    gmlp = _gmlp_operands(uv, v_ln_g[0:1], v_ln_b[0:1], w_spatial[0].astype(BF16), b_spatial[0].T, tm=ROW_TM)
    x = _out_ln_call("gmlp", *gmlp, w_out_o[0].astype(BF16), x, mod_vec(1, 2), ln_g[1, 0:1], ln_b[1, 0:1],
                     alpha=alpha, tm=ROW_TM)
    x = _moe(x, mod_vec(1, 3), mod_vec(1, 4), mod_vec(1, 5), w_router_t, rbias,
             w_gate, w_up, w_down, 1, ln_g[1, 1:2], ln_b[1, 1:2], alpha=alpha)
    return x
```

```python
import functools
import math

import numpy as np
import jax
import jax.numpy as jnp
from jax import lax
from jax.experimental import pallas as pl
from jax.experimental.pallas import tpu as pltpu

F32 = jnp.float32
BF16 = jnp.bfloat16

GRID_W = 64
DA_HEADS = 8
N_EXPERTS = 16
N_GROUPS = 4
EXPERTS_PER_GROUP = N_EXPERTS // N_GROUPS
ROPE_BASE = 10000.0
LN_EPS = 1e-5
CHUNK = 128
GMLP_GROUPS = 16
LANES = 128
MXU_DEPTH = 256

VMEM_LIMIT = 56 * 1024 * 1024
PROJ_TM = 1024
ROW_TM = 512
ATTN_TQ = 1024


def _cparams(sem):
    return pltpu.CompilerParams(dimension_semantics=sem, vmem_limit_bytes=VMEM_LIMIT)


def _layer_norm(r, g, b):
    mu = jnp.mean(r, axis=-1, keepdims=True)
    d = r - mu
    var = jnp.mean(d * d, axis=-1, keepdims=True)
    return d * lax.rsqrt(var + LN_EPS) * g + b


def _mod_kernel(c_ref, w_ref, b_ref, o_ref):
    c = c_ref[...]
    s = (c * jax.nn.sigmoid(c)).astype(BF16)
    o_ref[0] = jnp.dot(s, w_ref[0].astype(BF16), preferred_element_type=F32) + b_ref[0]


def _mod_call(cond, w_mod, b_mod):
    depth, d, n = w_mod.shape
    rows = cond.shape[0]
    tn = 1024
    return pl.pallas_call(
        _mod_kernel,
        grid=(depth, n // tn),
        in_specs=[
            pl.BlockSpec((rows, d), lambda l, j: (0, 0)),
            pl.BlockSpec((1, d, tn), lambda l, j: (l, 0, j)),
            pl.BlockSpec((1, 1, tn), lambda l, j: (l, 0, j)),
        ],
        out_specs=pl.BlockSpec((1, rows, tn), lambda l, j: (l, 0, j)),
        out_shape=jax.ShapeDtypeStruct((depth, rows, n), F32),
        compiler_params=_cparams(("arbitrary", "arbitrary")),
        name="adaln_mod",
    )(cond, w_mod, b_mod.reshape(depth, 1, n))


def _proj_kernel(*refs, n_flat, n_head_tiles, rope, gelu):
    x_ref, sh_ref, sc_ref, w_ref = refs[:4]
    pos = 4
    if rope is not None:
        cos_ref, sin_ref = refs[pos:pos + 2]
        pos += 2
    flat_ref = refs[pos] if n_flat else None
    pos += 1 if n_flat else 0
    head_refs = refs[pos:pos + n_head_tiles]
    h_ref = refs[pos + n_head_tiles]
    j = pl.program_id(2)

    @pl.when(j == 0)
    def _():
        h_ref[...] = (x_ref[0] * (1.0 + sc_ref[0]) + sh_ref[0]).astype(BF16)

    tn = w_ref.shape[1]
    halves = [slice(c * (tn // 2), (c + 1) * (tn // 2)) for c in range(2)]

    def product(cols):
        return jnp.dot(h_ref[...], w_ref[:, cols], preferred_element_type=F32)

    if n_flat:
        @pl.when(j < n_flat)
        def _():
            for cols in halves:
                acc = product(cols)
                flat_ref[0, :, cols] = (jax.nn.gelu(acc, approximate=True) if gelu else acc).astype(flat_ref.dtype)

    for t in range(n_head_tiles):
        tile = n_flat + t
        o_ref = head_refs[t]

        @pl.when(j == tile)
        def _(tile=tile, o_ref=o_ref):
            rotary = rope is not None and tile in rope[0]
            if rotary:
                scale = rope[2] if tile == rope[1] else 1.0
                cs = cos_ref[...] * scale
                sn = sin_ref[...] * scale
            for cols in halves:
                acc = product(cols)
                for h in range(acc.shape[1] // LANES):
                    piece = acc[:, h * LANES:(h + 1) * LANES]
                    if rotary:
                        piece = piece * cs + pltpu.roll(piece, LANES // 2, axis=1) * sn
                    o_ref[0, cols.start // LANES + h] = piece.astype(o_ref.dtype)


def _proj_call(x, shift, scale, w, *, tm, tn, n_flat, rope=None, gelu=False):
    bsz, s, d = x.shape
    n_tiles = w.shape[1] // tn
    n_head_tiles = n_tiles - n_flat
    heads = tn // LANES
    tm = min(tm, s)
    per_batch = shift.shape[0] > 1
    mod_map = (lambda b, i, j: (b, 0, 0)) if per_batch else (lambda b, i, j: (0, 0, 0))
    in_specs = [
        pl.BlockSpec((1, tm, d), lambda b, i, j: (b, i, 0)),
        pl.BlockSpec((1, 1, d), mod_map),
        pl.BlockSpec((1, 1, d), mod_map),
        pl.BlockSpec((d, tn), lambda b, i, j: (0, j)),
    ]
    args = [x, shift, scale, w]
    rope_static = None
    if rope is not None:
        cos_t, sin_t, rope_tiles, q_tile, q_scale = rope
        in_specs += [pl.BlockSpec((tm, LANES), lambda b, i, j: (i, 0))] * 2
        args += [cos_t, sin_t]
        rope_static = (tuple(rope_tiles), q_tile, q_scale)
    out_specs, out_shape = [], []
    if n_flat:
        out_specs.append(pl.BlockSpec((1, tm, tn), lambda b, i, j: (b, i, jnp.minimum(j, n_flat - 1))))
        out_shape.append(jax.ShapeDtypeStruct((bsz, s, n_flat * tn), BF16))
    for _ in range(n_head_tiles):
        out_specs.append(pl.BlockSpec((1, heads, tm, LANES), lambda b, i, j: (b, 0, i, 0)))
        out_shape.append(jax.ShapeDtypeStruct((bsz, heads, s, LANES), BF16))
    return pl.pallas_call(
        functools.partial(_proj_kernel, n_flat=n_flat, n_head_tiles=n_head_tiles, rope=rope_static, gelu=gelu),
        grid=(bsz, s // tm, n_tiles),
        in_specs=in_specs,
        out_specs=out_specs,
        out_shape=out_shape,
        scratch_shapes=[pltpu.VMEM((tm, d), BF16)],
        compiler_params=_cparams(("arbitrary", "arbitrary", "arbitrary")),
        name="mod_proj",
    )(*args)


ONES_ROWS = 16


def _attn_kernel(lam_ref, g_ref, q_ref, kc_ref, vc_ref, k_ref, v_ref, o_ref, kf_ref, vt_ref, s_ref, p_ref, acc_ref,
                 *, tk, lam_init):
    tq = q_ref.shape[2]
    c_len = kc_ref.shape[2]
    s_len = k_ref.shape[2]
    n_chunks = (c_len + s_len) // tk
    vt_chunk = 512

    @pl.when(pl.program_id(2) == 0)
    def _():
        kf_ref[:c_len, :] = kc_ref[0, 0]
        kf_ref[c_len:, :] = k_ref[0, 0]
        vt_ref[LANES:, :] = jnp.ones((ONES_ROWS, vt_ref.shape[1]), BF16)
        vt_ref[:LANES, :c_len] = jnp.transpose(vc_ref[0, 0].astype(F32)).astype(BF16)
        for n in range(s_len // vt_chunk):
            rows = slice(n * vt_chunk, (n + 1) * vt_chunk)
            cols = slice(c_len + n * vt_chunk, c_len + (n + 1) * vt_chunk)
            vt_ref[:LANES, cols] = jnp.transpose(v_ref[0, 0, rows, :].astype(F32)).astype(BF16)

    q_t = jnp.transpose(q_ref[0, 0].astype(F32))
    dim = lax.broadcasted_iota(jnp.int32, q_t.shape, 0)
    first_map = (dim % (LANES // 2)) < (LANES // 4)
    qs_t = jnp.concatenate([jnp.where(first_map, q_t, 0.0), jnp.where(first_map, 0.0, q_t)], axis=1).astype(BF16)

    acc_ref[...] = jnp.zeros(acc_ref.shape, F32)

    def scores(t):
        off = t * tk if isinstance(t, int) else pl.multiple_of(t * tk, tk)
        return jnp.dot(kf_ref[pl.ds(off, tk), :], qs_t, preferred_element_type=F32)

    def softmax(slot, m_old):
        s_t = s_ref[slot]
        m_new = jnp.maximum(m_old, jnp.max(s_t, axis=0, keepdims=True))
        p_ref[slot] = jnp.exp2((s_t - m_new).astype(BF16))
        return m_new, jnp.exp2(m_old - m_new)

    def values(t, slot, alpha):
        off = t * tk if isinstance(t, int) else pl.multiple_of(t * tk, tk)
        pv = jnp.dot(vt_ref[:, pl.ds(off, tk)], p_ref[slot], preferred_element_type=F32)
        acc_ref[...] = alpha * acc_ref[...] + pv

    def tick(t, parity, m, alpha_prev):
        s_ref[parity] = scores(t)
        m, alpha = softmax(1 - parity, m)
        values(t - 2, parity, alpha_prev)
        return m, alpha

    s_ref[0] = scores(0)
    s_ref[1] = scores(1)
    m, alpha = softmax(0, jnp.full((1, 2 * tq), -jnp.inf, F32))

    def pair(jj, carry):
        m, alpha = tick(2 + 2 * jj, 0, *carry)
        return tick(3 + 2 * jj, 1, m, alpha)

    n_full = n_chunks - 2
    m, alpha = lax.fori_loop(0, n_full // 2, pair, (m, alpha))
    if n_full % 2:
        m, alpha = tick(n_chunks - 1, (n_chunks - 1) % 2, m, alpha)
    last = (n_chunks - 1) % 2
    m, alpha_last = softmax(last, m)
    values(n_chunks - 2, 1 - last, alpha)
    values(n_chunks - 1, last, alpha_last)

    lp = lam_ref[...]
    lam = (jnp.exp(jnp.sum(lp[0:1] * lp[1:2], axis=-1, keepdims=True))
           - jnp.exp(jnp.sum(lp[2:3] * lp[3:4], axis=-1, keepdims=True)) + lam_init)
    acc = acc_ref[...]
    o_t = acc[:LANES] / acc[LANES:LANES + 1]
    o_t = o_t[:, :tq] - lam * o_t[:, tq:]
    o_t = o_t * lax.rsqrt(jnp.mean(o_t * o_t, axis=0, keepdims=True) + LN_EPS)
    o_ref[0, 0] = (jnp.transpose(o_t) * g_ref[...] * (1.0 - lam_init)).astype(o_ref.dtype)


def _attn_call(q, k, v, kc, vc, lam_params, subln_g, *, lam_init, tq):
    bsz, h, s, _ = q.shape
    c_len = kc.shape[2]
    tq = min(tq, s)
    tk = max(t for t in (MXU_DEPTH, 2 * MXU_DEPTH, 3 * MXU_DEPTH) if (c_len + s) % t == 0)
    assert (c_len + s) // tk >= 3 and s % 512 == 0
    whole = lambda n: pl.BlockSpec((1, 1, n, LANES), lambda b, hh, i: (b, hh, 0, 0))
    return pl.pallas_call(
        functools.partial(_attn_kernel, tk=tk, lam_init=lam_init),
        grid=(bsz, h, s // tq),
        in_specs=[
            pl.BlockSpec((8, LANES), lambda b, hh, i: (0, 0)),
            pl.BlockSpec((1, LANES), lambda b, hh, i: (0, 0)),
            pl.BlockSpec((1, 1, tq, LANES), lambda b, hh, i: (b, hh, i, 0)),
            whole(c_len), whole(c_len), whole(s), whole(s),
        ],
        out_specs=pl.BlockSpec((1, 1, tq, LANES), lambda b, hh, i: (b, hh, i, 0)),
        out_shape=jax.ShapeDtypeStruct((bsz, h, s, LANES), BF16),
        scratch_shapes=[
            pltpu.VMEM((c_len + s, LANES), BF16),
            pltpu.VMEM((LANES + ONES_ROWS, c_len + s), BF16),
            pltpu.VMEM((2, tk, 2 * tq), F32),
            pltpu.VMEM((2, tk, 2 * tq), BF16),
            pltpu.VMEM((LANES + ONES_ROWS, 2 * tq), F32),
        ],
        compiler_params=_cparams(("arbitrary", "arbitrary", "arbitrary")),
        name="diff_attn",
    )(lam_params, subln_g, q, kc, vc, k, v)


def _gated_conv(xa_ref, bg_ref, cg_ref, xap_ref, cgp_ref, xan_ref, cgn_ref, w_ref):
    i = pl.program_id(1)
    tm = xa_ref.shape[1]
    halo = xap_ref.shape[1]
    z = xa_ref[0].astype(F32) * cg_ref[0].astype(F32)
    z_before = (xap_ref[0].astype(F32) * cgp_ref[0].astype(F32))[halo - 1:halo]
    z_after = (xan_ref[0].astype(F32) * cgn_ref[0].astype(F32))[0:1]
    z_before = jnp.where(i == 0, 0.0, z_before)
    z_after = jnp.where(i == pl.num_programs(1) - 1, 0.0, z_after)
    row = lax.broadcasted_iota(jnp.int32, z.shape, 0)
    z_prev = jnp.where(row == 0, z_before, pltpu.roll(z, 1, axis=0))
    z_next = jnp.where(row == tm - 1, z_after, pltpu.roll(z, tm - 1, axis=0))
    w = w_ref[...]
    conv = w[0:1] * z_prev + w[1:2] * z + w[2:3] * z_next
    return bg_ref[0].astype(F32) * conv


def _gated_conv_operands(proj, conv_w, *, width, tm):
    s = proj.shape[1]
    halo = 16
    nh = tm // halo
    last = s // halo - 1
    main = lambda c: pl.BlockSpec((1, tm, width), lambda b, i: (b, i, c))
    prev = lambda c: pl.BlockSpec((1, halo, width), lambda b, i: (b, jnp.maximum(i * nh - 1, 0), c))
    nxt = lambda c: pl.BlockSpec((1, halo, width), lambda b, i: (b, jnp.minimum((i + 1) * nh, last), c))
    w_pad = jnp.zeros((8, width), F32).at[:conv_w.shape[0]].set(conv_w)
    specs = [main(0), main(1), main(2), prev(0), prev(2), nxt(0), nxt(2), pl.BlockSpec((8, width), lambda b, i: (0, 0))]
    return specs, [proj] * 7 + [w_pad]


def _gmlp_gate(u_ref, v_ref, g_ref, b_ref, ws_ref, bs_ref, a_ref):
    tm = u_ref.shape[1]
    v = _layer_norm(v_ref[0].astype(F32), g_ref[...], b_ref[...]).astype(BF16)
    bs = bs_ref[...]
    for n in range(tm // CHUNK):
        rows = slice(n * CHUNK, (n + 1) * CHUNK)
        for g in range(GMLP_GROUPS):
            cols = slice(g * LANES, (g + 1) * LANES)
            sg = jnp.dot(ws_ref[g], v[rows, cols], preferred_element_type=F32) + bs[:, g:g + 1]
            a_ref[rows, cols] = (u_ref[0, rows, cols].astype(F32) * sg).astype(a_ref.dtype)


def _out_ln_kernel(*refs, alpha, mixer):
    if mixer == "conv_attn":
        conv_refs, (heads_ref, w_ref, x_ref, gate_ref, g_ref, b_ref, o_ref) = refs[:8], refs[8:]
        a = jnp.concatenate([_gated_conv(*conv_refs).astype(BF16)]
                            + [heads_ref[0, h] for h in range(heads_ref.shape[1])], axis=1)
    else:
        gmlp_refs, (w_ref, x_ref, gate_ref, g_ref, b_ref, o_ref, a_ref) = refs[:6], refs[6:]
        _gmlp_gate(*gmlp_refs, a_ref)
        a = a_ref[...]
    half = a.shape[0] // 2
    for rows in (slice(0, half), slice(half, 2 * half)):
        y = jnp.dot(a[rows], w_ref[...], preferred_element_type=F32)
        r = alpha * x_ref[0, rows, :] + gate_ref[0] * y
        o_ref[0, rows, :] = _layer_norm(r, g_ref[...], b_ref[...])


def _out_ln_call(mixer, mixer_specs, mixer_args, w, x, gate, ln_g, ln_b, *, alpha, tm):
    bsz, s, d = x.shape
    scratch = [pltpu.VMEM((tm, w.shape[0]), BF16)] if mixer == "gmlp" else []
    return pl.pallas_call(
        functools.partial(_out_ln_kernel, alpha=alpha, mixer=mixer),
        grid=(bsz, s // tm),
        in_specs=mixer_specs + [
            pl.BlockSpec(w.shape, lambda b, i: (0, 0)),
            pl.BlockSpec((1, tm, d), lambda b, i: (b, i, 0)),
            pl.BlockSpec((1, 1, d), lambda b, i: (b, 0, 0)),
            pl.BlockSpec((1, d), lambda b, i: (0, 0)),
            pl.BlockSpec((1, d), lambda b, i: (0, 0)),
        ],
        out_specs=pl.BlockSpec((1, tm, d), lambda b, i: (b, i, 0)),
        out_shape=jax.ShapeDtypeStruct((bsz, s, d), F32),
        scratch_shapes=scratch,
        compiler_params=_cparams(("arbitrary", "arbitrary")),
        name="out_proj_ln",
    )(*mixer_args, w, x, gate, ln_g, ln_b)


def _conv_attn_operands(proj, conv_w, attn, *, tm):
    specs, args = _gated_conv_operands(proj, conv_w, width=conv_w.shape[-1], tm=tm)
    specs.append(pl.BlockSpec((1, attn.shape[1], tm, LANES), lambda b, i: (b, 0, i, 0)))
    return specs, args + [attn]


def _gmlp_operands(uv, v_g, v_b, w_s, b_s_t, *, tm):
    d = uv.shape[2] // 2
    specs = [
        pl.BlockSpec((1, tm, d), lambda b, i: (b, i, 0)),
        pl.BlockSpec((1, tm, d), lambda b, i: (b, i, 1)),
        pl.BlockSpec((1, d), lambda b, i: (0, 0)),
        pl.BlockSpec((1, d), lambda b, i: (0, 0)),
        pl.BlockSpec(w_s.shape, lambda b, i: (0, 0, 0)),
        pl.BlockSpec(b_s_t.shape, lambda b, i: (0, 0)),
    ]
    return specs, [uv, uv, v_g, v_b, w_s, b_s_t]


def _route(scores, bias):
    sel = scores + bias
    rows = [sel[e:e + 1, :] for e in range(N_EXPERTS)]
    group_score = []
    for g in range(N_GROUPS):
        r = rows[g * EXPERTS_PER_GROUP:(g + 1) * EXPERTS_PER_GROUP]
        best = None
        for a in range(EXPERTS_PER_GROUP):
            for b in range(a + 1, EXPERTS_PER_GROUP):
                pair = r[a] + r[b]
                best = pair if best is None else jnp.maximum(best, pair)
        group_score.append(best)
    one = jnp.ones_like(rows[0])
    zero = jnp.zeros_like(rows[0])
    picked = []
    chosen = []
    for g in range(N_GROUPS):
        beaten = zero
        for o in range(N_GROUPS):
            if o < g:
                beaten = jnp.where(group_score[o] >= group_score[g], one, beaten)
            elif o > g:
                beaten = jnp.where(group_score[o] > group_score[g], one, beaten)
        chosen.append(1.0 - beaten)
        for a in range(EXPERTS_PER_GROUP):
            e = g * EXPERTS_PER_GROUP + a
            rank = zero
            for b in range(EXPERTS_PER_GROUP):
                o = g * EXPERTS_PER_GROUP + b
                if b < a:
                    rank = rank + jnp.where(rows[o] >= rows[e], one, zero)
                elif b > a:
                    rank = rank + jnp.where(rows[o] > rows[e], one, zero)
            picked.append(jnp.where(rank < 2.0, one, zero) * chosen[g])
    w = [picked[e] * scores[e:e + 1, :] for e in range(N_EXPERTS)]
    total = w[0]
    for e in range(1, N_EXPERTS):
        total = total + w[e]
    return [we / total for we in w], chosen


ROW_BLOCK = 32
STEP_BLOCKS = 8
DEST_LANE = EXPERTS_PER_GROUP
SORT_TM = 512


def _sort_rows(tm):
    return tm + N_GROUPS * ROW_BLOCK


def _moe_sort_kernel(x_ref, sh_ref, sc_ref, wr_ref, rb_ref, xs_ref, cs_ref, tok_ref, seg_ref):
    tm, d = x_ref.shape[1], x_ref.shape[2]
    rows = xs_ref.shape[0]
    col_chunk = 512
    h = (x_ref[0] * (1.0 + sc_ref[0]) + sh_ref[0]).astype(BF16)
    logits = lax.dot_general(wr_ref[...], h, (((1,), (1,)), ((), ())), preferred_element_type=F32)
    comb, chosen = _route(jax.nn.sigmoid(logits), rb_ref[...])
    src = lax.broadcasted_iota(jnp.int32, (tm, tm), 0)
    dst = lax.broadcasted_iota(jnp.int32, (tm, tm), 1)
    before = jnp.where(src < dst, 1.0, 0.0).astype(BF16)
    pad_rows = [jnp.zeros_like(chosen[0])] * (8 - N_GROUPS)
    rank = jnp.dot(jnp.concatenate(chosen + pad_rows, axis=0).astype(BF16), before, preferred_element_type=F32)
    dest = jnp.zeros_like(chosen[0])
    first = jnp.zeros((1, 1), F32)
    firsts, counts = [], []
    for g in range(N_GROUPS):
        count = jnp.sum(chosen[g], axis=1, keepdims=True)
        n_blocks = jnp.floor((count + (ROW_BLOCK - 1)) * (1.0 / ROW_BLOCK))
        firsts.append(first)
        counts.append(n_blocks)
        dest = dest + chosen[g] * (first * ROW_BLOCK + rank[g:g + 1])
        first = first + n_blocks
    seg = jnp.concatenate([jnp.broadcast_to(v, (1, LANES)) for v in firsts + counts], axis=0)
    seg_ref[0] = seg.astype(jnp.int32)
    dest_i = dest.astype(jnp.int32)
    row_id = lax.broadcasted_iota(jnp.int32, (rows, tm), 0)
    p = jnp.where(row_id == dest_i, 1.0, 0.0).astype(BF16)
    for c in range(d // col_chunk):
        cols = slice(c * col_chunk, (c + 1) * col_chunk)
        xs_ref[:, cols] = jnp.dot(p, h[:, cols], preferred_element_type=F32).astype(BF16)
    in_group = []
    for j in range(EXPERTS_PER_GROUP):
        cj = comb[j]
        for g in range(1, N_GROUPS):
            cj = cj + comb[g * EXPERTS_PER_GROUP + j]
        in_group.append(cj)
    c_rows = jnp.concatenate(in_group + [dest, jnp.zeros((LANES - DEST_LANE - 1, tm), F32)], axis=0)
    c_cols = jnp.transpose(c_rows)
    tok_ref[...] = c_cols
    hi = c_cols.astype(BF16)
    lo = (c_cols - hi.astype(F32)).astype(BF16)
    cs = jnp.dot(p, jnp.concatenate([hi, lo], axis=1), preferred_element_type=F32)
    cs_ref[...] = cs[:, :LANES] + cs[:, LANES:]


def _moe_sort_call(x, shift, scale, w_router_t, router_bias, *, tm):
    bsz, s, d = x.shape
    nt = s // tm
    n_tiles = bsz * nt
    rows = _sort_rows(tm)
    tile = lambda b, i: b * nt + i
    vec = lambda: pl.BlockSpec((1, 1, d), lambda b, i: (b, 0, 0))
    return pl.pallas_call(
        _moe_sort_kernel,
        grid=(bsz, nt),
        in_specs=[
            pl.BlockSpec((1, tm, d), lambda b, i: (b, i, 0)),
            vec(), vec(),
            pl.BlockSpec(w_router_t.shape, lambda b, i: (0, 0)),
            pl.BlockSpec(router_bias.shape, lambda b, i: (0, 0)),
        ],
        out_specs=[
            pl.BlockSpec((rows, d), lambda b, i: (tile(b, i), 0)),
            pl.BlockSpec((rows, LANES), lambda b, i: (tile(b, i), 0)),
            pl.BlockSpec((tm, LANES), lambda b, i: (tile(b, i), 0)),
            pl.BlockSpec((1, 2 * N_GROUPS, LANES), lambda b, i: (tile(b, i), 0, 0)),
        ],
        out_shape=[
            jax.ShapeDtypeStruct(((n_tiles + 1) * rows, d), BF16),
            jax.ShapeDtypeStruct(((n_tiles + 1) * rows, LANES), F32),
            jax.ShapeDtypeStruct((n_tiles * tm, LANES), F32),
            jax.ShapeDtypeStruct((n_tiles, 2 * N_GROUPS, LANES), jnp.int32),
        ],
        compiler_params=_cparams(("arbitrary", "arbitrary")),
        name="moe_sort",
    )(x, shift, scale, w_router_t, router_bias)


def _expert_work_table(seg, rows):
    n_tiles = seg.shape[0]
    per_tile = rows // ROW_BLOCK
    spare = n_tiles * per_tile
    max_items = spare // STEP_BLOCKS + N_GROUPS
    first = seg[:, :N_GROUPS, 0]
    count = seg[:, N_GROUPS:, 0]
    tri = jnp.arange(n_tiles)[:, None] >= jnp.arange(n_tiles)[None, :]
    cum = jnp.sum(jnp.where(tri[:, :, None], count[None, :, :], 0), axis=1)
    total = cum[-1]
    items = (total + (STEP_BLOCKS - 1)) // STEP_BLOCKS
    ends = jnp.sum(jnp.where(jnp.arange(N_GROUPS)[:, None] >= jnp.arange(N_GROUPS)[None, :], items[None, :], 0),
                   axis=1)
    n_items = ends[-1]
    step = jnp.arange(max_items + 1)
    live = step < n_items
    ref_step = jnp.where(live, step, jnp.maximum(n_items - 1, 0))
    grp = jnp.sum((ref_step[:, None] >= ends[None, :-1]).astype(jnp.int32), axis=1)
    pick = grp[:, None] == jnp.arange(N_GROUPS)[None, :]
    of_group = lambda v: jnp.sum(jnp.where(pick, v[None, :], 0), axis=1)
    start = of_group(ends - items)
    cum_s = jnp.sum(jnp.where(pick[:, None, :], cum[None, :, :], 0), axis=2)
    count_s = jnp.sum(jnp.where(pick[:, None, :], count[None, :, :], 0), axis=2)
    first_s = jnp.sum(jnp.where(pick[:, None, :], first[None, :, :], 0), axis=2)
    total_s = of_group(total)

    def block_id(k):
        done = cum_s <= k[:, None]
        t = jnp.sum(done.astype(jnp.int32), axis=1)
        skipped = jnp.sum(jnp.where(done, count_s, 0), axis=1)
        at_t = jnp.arange(n_tiles)[None, :] == t[:, None]
        return t * per_tile + jnp.sum(jnp.where(at_t, first_s, 0), axis=1) + (k - skipped)

    as_i32 = lambda v: v.astype(jnp.int32)
    blocks = []
    for u in range(STEP_BLOCKS):
        k_u = STEP_BLOCKS * (step - start) + u
        has_u = live & (k_u < total_s)
        blocks.append(as_i32(jnp.where(has_u, block_id(k_u), blocks[0] if u else spare)))
    local = (jnp.arange(spare) % per_tile)[:, None]
    per_block = lambda v: jnp.repeat(v, per_tile, axis=0)
    first_r, count_r = per_block(first), per_block(count)
    inside = (local >= first_r) & (local < first_r + count_r)
    k_r = per_block(cum - count) + (local - first_r)
    slot = STEP_BLOCKS * (ends - items)[None, :] + k_r
    where = jnp.where(jnp.any(inside, axis=1), jnp.sum(jnp.where(inside, slot, 0), axis=1), STEP_BLOCKS * max_items)
    return (*blocks, as_i32(grp), as_i32(n_items).reshape(1)), as_i32(where)


def _moe_expert_kernel(*refs):
    n_items = refs[STEP_BLOCKS + 1]
    pos = STEP_BLOCKS + 2
    x_refs = refs[pos:pos + STEP_BLOCKS]
    c_refs = refs[pos + STEP_BLOCKS:pos + 2 * STEP_BLOCKS]
    wg_ref, wu_ref, wd_ref, y_ref = refs[pos + 2 * STEP_BLOCKS:]
    f, d = wd_ref.shape[2], wd_ref.shape[3]

    @pl.when(pl.program_id(0) < n_items[0])
    def _():
        x = jnp.concatenate([r[...] for r in x_refs], axis=0)
        c = jnp.concatenate([r[...] for r in c_refs], axis=0)
        pieces = []
        for j in range(EXPERTS_PER_GROUP):
            gate = jnp.dot(x, wg_ref[0, j].astype(BF16), preferred_element_type=F32)
            up = jnp.dot(x, wu_ref[0, j].astype(BF16), preferred_element_type=F32)
            pieces.append((jax.nn.silu(gate) * up * c[:, j:j + 1]).astype(BF16))
        w_down = wd_ref[0].reshape(EXPERTS_PER_GROUP * f, d).astype(BF16)
        y = jnp.dot(jnp.concatenate(pieces, axis=1), w_down, preferred_element_type=F32)
        y_ref[...] = y.astype(BF16)

    @pl.when(pl.program_id(0) >= n_items[0])
    def _():
        y_ref[...] = jnp.zeros(y_ref.shape, BF16)


def _moe_expert_call(table, xs, cs, w_gate, w_up, w_down, layer):
    d = xs.shape[1]
    steps = table[0].shape[0]
    n_pre = len(table)
    grp_at = STEP_BLOCKS
    rows_of = lambda u, width: pl.BlockSpec((ROW_BLOCK, width), lambda i, *pre: (pre[u][i], 0))
    weights = lambda w: pl.BlockSpec((1,) + w.shape[1:],
                                     lambda i, *pre: (layer * N_GROUPS + pre[grp_at][i], 0, 0, 0),
                                     pipeline_mode=pl.Buffered(1))
    grid_spec = pltpu.PrefetchScalarGridSpec(
        num_scalar_prefetch=n_pre,
        grid=(steps,),
        in_specs=[rows_of(u, d) for u in range(STEP_BLOCKS)] + [rows_of(u, LANES) for u in range(STEP_BLOCKS)]
        + [weights(w_gate), weights(w_up), weights(w_down)],
        out_specs=pl.BlockSpec((STEP_BLOCKS * ROW_BLOCK, d), lambda i, *pre: (i, 0)),
    )
    return pl.pallas_call(
        _moe_expert_kernel,
        grid_spec=grid_spec,
        out_shape=jax.ShapeDtypeStruct((steps * STEP_BLOCKS * ROW_BLOCK, d), BF16),
        compiler_params=_cparams(("arbitrary",)),
        name="moe_experts",
    )(*table, *([xs] * STEP_BLOCKS), *([cs] * STEP_BLOCKS), w_gate, w_up, w_down)


def _moe_combine_kernel(where_ref, tok_ref, *refs, alpha, n_blocks):
    y_refs = refs[:n_blocks]
    x_ref, gate_ref, g_ref, b_ref, o_ref = refs[n_blocks:]
    tm = x_ref.shape[1]
    ys = jnp.concatenate([r[...] for r in y_refs], axis=0)
    half = tm // 2
    row_id = lax.broadcasted_iota(jnp.int32, (half, ys.shape[0]), 1)
    for rows in (slice(0, half), slice(half, tm)):
        dest = tok_ref[rows, DEST_LANE:DEST_LANE + 1].astype(jnp.int32)
        p_t = jnp.where(row_id == dest, 1.0, 0.0).astype(BF16)
        y = jnp.dot(p_t, ys, preferred_element_type=F32)
        r = alpha * x_ref[0, rows, :] + gate_ref[0] * y
        o_ref[0, rows, :] = _layer_norm(r, g_ref[...], b_ref[...])


def _moe_combine_call(where, tok, ys, x, gate, ln_g, ln_b, *, alpha, sort_tm, tm):
    bsz, s, d = x.shape
    nt = s // sort_tm
    parts = sort_tm // tm
    n_blocks = _sort_rows(sort_tm) // ROW_BLOCK
    tile = lambda b, i: b * nt + i
    y_spec = lambda j: pl.BlockSpec((ROW_BLOCK, d), lambda b, i, k, w: (w[tile(b, i) * n_blocks + j], 0))
    grid_spec = pltpu.PrefetchScalarGridSpec(
        num_scalar_prefetch=1,
        grid=(bsz, nt, parts),
        in_specs=[pl.BlockSpec((tm, LANES), lambda b, i, k, w: (tile(b, i) * parts + k, 0))]
        + [y_spec(j) for j in range(n_blocks)]
        + [
            pl.BlockSpec((1, tm, d), lambda b, i, k, w: (b, i * parts + k, 0)),
            pl.BlockSpec((1, 1, d), lambda b, i, k, w: (b, 0, 0)),
            pl.BlockSpec((1, d), lambda b, i, k, w: (0, 0)),
            pl.BlockSpec((1, d), lambda b, i, k, w: (0, 0)),
        ],
        out_specs=pl.BlockSpec((1, tm, d), lambda b, i, k, w: (b, i * parts + k, 0)),
    )
    return pl.pallas_call(
        functools.partial(_moe_combine_kernel, alpha=alpha, n_blocks=n_blocks),
        grid_spec=grid_spec,
        out_shape=jax.ShapeDtypeStruct((bsz, s, d), F32),
        compiler_params=_cparams(("arbitrary", "arbitrary", "arbitrary")),
        name="moe_combine",
    )(where, tok, *([ys] * n_blocks), x, gate, ln_g, ln_b)


def _moe(x, shift, scale, gate, w_router_t, router_bias, w_gate, w_up, w_down, layer, ln_g, ln_b, *, alpha):
    sort_tm = min(SORT_TM, x.shape[1])
    xs, cs, tok, seg = _moe_sort_call(x, shift, scale, w_router_t, router_bias, tm=sort_tm)
    table, where = _expert_work_table(seg, _sort_rows(sort_tm))
    by_group = lambda w: w.reshape((w.shape[0] * N_GROUPS, EXPERTS_PER_GROUP) + w.shape[2:])
    ys = _moe_expert_call(table, xs, cs, by_group(w_gate), by_group(w_up), by_group(w_down), layer)
    return _moe_combine_call(where, tok, ys, x, gate, ln_g, ln_b, alpha=alpha, sort_tm=sort_tm,
                             tm=min(ROW_TM, sort_tm))


def _rope_tables(n):
    freqs = LANES // 8
    inv = jnp.power(ROPE_BASE, -jnp.arange(freqs, dtype=F32) / freqs)
    tok = jnp.arange(n)
    ang_r = (tok // GRID_W).astype(F32)[:, None] * inv
    ang_c = (tok % GRID_W).astype(F32)[:, None] * inv
    ang = jnp.concatenate([ang_r, ang_c], axis=1)
    cos = jnp.tile(jnp.cos(ang), (1, 4))
    sin = jnp.tile(jnp.sin(ang), (1, 4))
    sign = jnp.where(jnp.arange(LANES) < LANES // 2, -1.0, 1.0).astype(F32)
    return cos, sin * sign


def _rope_column_perm():
    perm = np.zeros((DA_HEADS, 2, 2, 2, 16), np.int32)
    for h in range(DA_HEADS):
        for p in range(2):
            for m in range(2):
                for ax in range(2):
                    for f in range(16):
                        perm[h, p, m, ax, f] = h * LANES + m * 64 + ax * 32 + p * 16 + f
    return perm.reshape(-1)


def kernel(x, c, ctx, c_ctx, w_mod, b_mod, ln_g, ln_b, w_in_e, conv_w, lambda_q1, lambda_k1, lambda_q2, lambda_k2,
           subln_g, w_out_e, w_in_o, v_ln_g, v_ln_b, w_spatial, b_spatial, w_out_o, w_router, router_bias,
           w_gate, w_up, w_down):
    bsz, s, d = x.shape
    depth = w_mod.shape[0]
    assert depth == 2 and w_in_e.shape[0] == 1 and w_in_o.shape[0] == 1, "two-layer (even, odd) stack only"
    conv_dim = conv_w.shape[-1]
    q_dim = DA_HEADS * LANES
    q_col, k_col, v_col = 3 * conv_dim, 3 * conv_dim + q_dim, 3 * conv_dim + 2 * q_dim
    assert w_in_e.shape[2] == v_col + q_dim and s % GRID_W == 0
    alpha = float((2 * depth) ** 0.25)

    rows = 8 * ((bsz + 1 + 7) // 8)
    cond = jnp.zeros((rows, d), F32).at[:bsz].set(c).at[bsz].set(c_ctx)
    mods = _mod_call(cond, w_mod, b_mod)

    def mod_vec(l, k, ctx_row=False):
        v = mods[l, bsz:bsz + 1, k * d:(k + 1) * d] if ctx_row else mods[l, :bsz, k * d:(k + 1) * d]
        return v.reshape(-1, 1, d)

    w_router_t = w_router.T.astype(BF16)
    rbias = router_bias.reshape(-1, 1).astype(F32)

    perm = _rope_column_perm()
    w_in = w_in_e[0]
    w_in = jnp.concatenate(
        [w_in[:, :q_col], w_in[:, q_col:k_col][:, perm], w_in[:, k_col:v_col][:, perm], w_in[:, v_col:]],
        axis=1).astype(BF16)
    cos_t, sin_t = _rope_tables(s)
    tn = q_dim
    assert q_col % tn == 0
    q_tile, k_tile = q_col // tn, k_col // tn
    conv_proj, q, k, v = _proj_call(
        x, mod_vec(0, 0), mod_vec(0, 1), w_in, tm=PROJ_TM, tn=tn, n_flat=q_tile,
        rope=(cos_t, sin_t, (q_tile, k_tile), q_tile, float(64 ** -0.5 * math.log2(math.e))))
    kc, vc = _proj_call(ctx, mod_vec(0, 0, True), mod_vec(0, 1, True), w_in[:, k_col:], tm=PROJ_TM, tn=tn, n_flat=0)

    lam_init = 0.8 - 0.6 * math.exp(-0.3 * 0)
    lam_params = jnp.zeros((8, LANES), F32)
    for r, p in enumerate((lambda_q1, lambda_k1, lambda_q2, lambda_k2)):
        lam_params = lam_params.at[r, :p.shape[-1]].set(p[0].astype(F32))
    attn = _attn_call(q, k, v, kc, vc, lam_params, subln_g[0].reshape(1, LANES), lam_init=lam_init, tq=ATTN_TQ)
    x = _out_ln_call("conv_attn", *_conv_attn_operands(conv_proj, conv_w[0], attn, tm=ROW_TM), w_out_e[0].astype(BF16),
                     x, mod_vec(0, 2), ln_g[0, 0:1], ln_b[0, 0:1], alpha=alpha, tm=ROW_TM)
    x = _moe(x, mod_vec(0, 3), mod_vec(0, 4), mod_vec(0, 5), w_router_t, rbias,
             w_gate, w_up, w_down, 0, ln_g[0, 1:2], ln_b[0, 1:2], alpha=alpha)

    w_gmlp = w_in_o[0].astype(BF16)
    uv, = _proj_call(x, mod_vec(1, 0), mod_vec(1, 1), w_gmlp, tm=PROJ_TM, tn=tn, n_flat=w_gmlp.shape[1] // tn,
                     gelu=True)
    gmlp = _gmlp_operands(uv, v_ln_g[0:1], v_ln_b[0:1], w_spatial[0].astype(BF16), b_spatial[0].T, tm=ROW_TM)
    x = _out_ln_call("gmlp", *gmlp, w_out_o[0].astype(BF16), x, mod_vec(1, 2), ln_g[1, 0:1], ln_b[1, 0:1],
                     alpha=alpha, tm=ROW_TM)
    x = _moe(x, mod_vec(1, 3), mod_vec(1, 4), mod_vec(1, 5), w_router_t, rbias,
             w_gate, w_up, w_down, 1, ln_g[1, 1:2], ln_b[1, 1:2], alpha=alpha)
    return x
```

```python
import functools
import math

import numpy as np
import jax
import jax.numpy as jnp
from jax import lax
from jax.experimental import pallas as pl
from jax.experimental.pallas import tpu as pltpu

F32 = jnp.float32
BF16 = jnp.bfloat16

GRID_W = 64
DA_HEADS = 8
N_EXPERTS = 16
N_GROUPS = 4
EXPERTS_PER_GROUP = N_EXPERTS // N_GROUPS
ROPE_BASE = 10000.0
LN_EPS = 1e-5
CHUNK = 128
GMLP_GROUPS = 16
LANES = 128
MXU_DEPTH = 256

VMEM_LIMIT = 56 * 1024 * 1024
PROJ_TM = 1024
ROW_TM = 512
ATTN_TQ = 1024


def _cparams(sem):
    return pltpu.CompilerParams(dimension_semantics=sem, vmem_limit_bytes=VMEM_LIMIT)


def _layer_norm(r, g, b):
    mu = jnp.mean(r, axis=-1, keepdims=True)
    d = r - mu
    var = jnp.mean(d * d, axis=-1, keepdims=True)
    return d * lax.rsqrt(var + LN_EPS) * g + b


def _mod_kernel(c_ref, w_ref, b_ref, o_ref):
    c = c_ref[...]
    s = (c * jax.nn.sigmoid(c)).astype(BF16)
    o_ref[0] = jnp.dot(s, w_ref[0].astype(BF16), preferred_element_type=F32) + b_ref[0]


def _mod_call(cond, w_mod, b_mod):
    depth, d, n = w_mod.shape
    rows = cond.shape[0]
    tn = 1024
    return pl.pallas_call(
        _mod_kernel,
        grid=(depth, n // tn),
        in_specs=[
            pl.BlockSpec((rows, d), lambda l, j: (0, 0)),
            pl.BlockSpec((1, d, tn), lambda l, j: (l, 0, j)),
            pl.BlockSpec((1, 1, tn), lambda l, j: (l, 0, j)),
        ],
        out_specs=pl.BlockSpec((1, rows, tn), lambda l, j: (l, 0, j)),
        out_shape=jax.ShapeDtypeStruct((depth, rows, n), F32),
        compiler_params=_cparams(("arbitrary", "arbitrary")),
        name="adaln_mod",
    )(cond, w_mod, b_mod.reshape(depth, 1, n))


def _proj_kernel(*refs, n_flat, n_head_tiles, rope, gelu):
    x_ref, sh_ref, sc_ref, w_ref = refs[:4]
    pos = 4
    if rope is not None:
        cos_ref, sin_ref = refs[pos:pos + 2]
        pos += 2
    flat_ref = refs[pos] if n_flat else None
    pos += 1 if n_flat else 0
    head_refs = refs[pos:pos + n_head_tiles]
    h_ref = refs[pos + n_head_tiles]
    j = pl.program_id(2)

    tn = w_ref.shape[1]
    halves = [slice(c * (tn // 2), (c + 1) * (tn // 2)) for c in range(2)]
    activate = (lambda acc: jax.nn.gelu(acc, approximate=True)) if gelu else (lambda acc: acc)

    def modulated(rows):
        return (x_ref[0, rows, :] * (1.0 + sc_ref[0]) + sh_ref[0]).astype(BF16)

    def product(cols):
        return jnp.dot(h_ref[...], w_ref[:, cols], preferred_element_type=F32)

    if n_flat:
        @pl.when(j == 0)
        def _():
            tm = h_ref.shape[0]
            for rows in (slice(0, tm // 2), slice(tm // 2, tm)):
                h = modulated(rows)
                h_ref[rows, :] = h
                for cols in halves:
                    acc = jnp.dot(h, w_ref[:, cols], preferred_element_type=F32)
                    flat_ref[0, rows, cols] = activate(acc).astype(flat_ref.dtype)

        @pl.when(jnp.logical_and(j > 0, j < n_flat))
        def _():
            for cols in halves:
                flat_ref[0, :, cols] = activate(product(cols)).astype(flat_ref.dtype)
    else:
        @pl.when(j == 0)
        def _():
            h_ref[...] = modulated(slice(None))

    for t in range(n_head_tiles):
        tile = n_flat + t
        o_ref = head_refs[t]

        @pl.when(j == tile)
        def _(tile=tile, o_ref=o_ref):
            rotary = rope is not None and tile in rope[0]
            if rotary:
                scale = rope[2] if tile == rope[1] else 1.0
                cs = cos_ref[...] * scale
                sn = sin_ref[...] * scale
            for cols in halves:
                acc = product(cols)
                for h in range(acc.shape[1] // LANES):
                    piece = acc[:, h * LANES:(h + 1) * LANES]
                    if rotary:
                        piece = piece * cs + pltpu.roll(piece, LANES // 2, axis=1) * sn
                    o_ref[0, cols.start // LANES + h] = piece.astype(o_ref.dtype)


def _proj_call(x, shift, scale, w, *, tm, tn, n_flat, rope=None, gelu=False):
    bsz, s, d = x.shape
    n_tiles = w.shape[1] // tn
    n_head_tiles = n_tiles - n_flat
    heads = tn // LANES
    tm = min(tm, s)
    per_batch = shift.shape[0] > 1
    mod_map = (lambda b, i, j: (b, 0, 0)) if per_batch else (lambda b, i, j: (0, 0, 0))
    in_specs = [
        pl.BlockSpec((1, tm, d), lambda b, i, j: (b, i, 0)),
        pl.BlockSpec((1, 1, d), mod_map),
        pl.BlockSpec((1, 1, d), mod_map),
        pl.BlockSpec((d, tn), lambda b, i, j: (0, j)),
    ]
    args = [x, shift, scale, w]
    rope_static = None
    if rope is not None:
        cos_t, sin_t, rope_tiles, q_tile, q_scale = rope
        in_specs += [pl.BlockSpec((tm, LANES), lambda b, i, j: (i, 0))] * 2
        args += [cos_t, sin_t]
        rope_static = (tuple(rope_tiles), q_tile, q_scale)
    out_specs, out_shape = [], []
    if n_flat:
        out_specs.append(pl.BlockSpec((1, tm, tn), lambda b, i, j: (b, i, jnp.minimum(j, n_flat - 1))))
        out_shape.append(jax.ShapeDtypeStruct((bsz, s, n_flat * tn), BF16))
    for _ in range(n_head_tiles):
        out_specs.append(pl.BlockSpec((1, heads, tm, LANES), lambda b, i, j: (b, 0, i, 0)))
        out_shape.append(jax.ShapeDtypeStruct((bsz, heads, s, LANES), BF16))
    return pl.pallas_call(
        functools.partial(_proj_kernel, n_flat=n_flat, n_head_tiles=n_head_tiles, rope=rope_static, gelu=gelu),
        grid=(bsz, s // tm, n_tiles),
        in_specs=in_specs,
        out_specs=out_specs,
        out_shape=out_shape,
        scratch_shapes=[pltpu.VMEM((tm, d), BF16)],
        compiler_params=_cparams(("arbitrary", "arbitrary", "arbitrary")),
        name="mod_proj",
    )(*args)


ONES_ROWS = 16


def _attn_kernel(lam_ref, g_ref, q_ref, kc_ref, vc_ref, k_ref, v_ref, o_ref, kf_ref, vt_ref, s_ref, p_ref, acc_ref,
                 *, tk, lam_init):
    tq = q_ref.shape[2]
    c_len = kc_ref.shape[2]
    s_len = k_ref.shape[2]
    n_chunks = (c_len + s_len) // tk
    vt_chunk = 512

    @pl.when(pl.program_id(2) == 0)
    def _():
        kf_ref[:c_len, :] = kc_ref[0, 0]
        kf_ref[c_len:, :] = k_ref[0, 0]
        vt_ref[LANES:, :] = jnp.ones((ONES_ROWS, vt_ref.shape[1]), BF16)
        vt_ref[:LANES, :c_len] = jnp.transpose(vc_ref[0, 0].astype(F32)).astype(BF16)
        for n in range(s_len // vt_chunk):
            rows = slice(n * vt_chunk, (n + 1) * vt_chunk)
            cols = slice(c_len + n * vt_chunk, c_len + (n + 1) * vt_chunk)
            vt_ref[:LANES, cols] = jnp.transpose(v_ref[0, 0, rows, :].astype(F32)).astype(BF16)

    q_t = jnp.transpose(q_ref[0, 0].astype(F32))
    dim = lax.broadcasted_iota(jnp.int32, q_t.shape, 0)
    first_map = (dim % (LANES // 2)) < (LANES // 4)
    qs_t = jnp.concatenate([jnp.where(first_map, q_t, 0.0), jnp.where(first_map, 0.0, q_t)], axis=1).astype(BF16)

    acc_ref[...] = jnp.zeros(acc_ref.shape, F32)

    def scores(t):
        off = t * tk if isinstance(t, int) else pl.multiple_of(t * tk, tk)
        return jnp.dot(kf_ref[pl.ds(off, tk), :], qs_t, preferred_element_type=F32)

    def softmax(slot, m_old):
        s_t = s_ref[slot]
        m_new = jnp.maximum(m_old, jnp.max(s_t, axis=0, keepdims=True))
        p_ref[slot] = jnp.exp2((s_t - m_new).astype(BF16))
        return m_new, jnp.exp2(m_old - m_new)

    def values(t, slot, alpha):
        off = t * tk if isinstance(t, int) else pl.multiple_of(t * tk, tk)
        pv = jnp.dot(vt_ref[:, pl.ds(off, tk)], p_ref[slot], preferred_element_type=F32)
        acc_ref[...] = alpha * acc_ref[...] + pv

    def tick(t, parity, m, alpha_prev):
        s_ref[parity] = scores(t)
        m, alpha = softmax(1 - parity, m)
        values(t - 2, parity, alpha_prev)
        return m, alpha

    s_ref[0] = scores(0)
    s_ref[1] = scores(1)
    m, alpha = softmax(0, jnp.full((1, 2 * tq), -jnp.inf, F32))

    def pair(jj, carry):
        m, alpha = tick(2 + 2 * jj, 0, *carry)
        return tick(3 + 2 * jj, 1, m, alpha)

    n_full = n_chunks - 2
    m, alpha = lax.fori_loop(0, n_full // 2, pair, (m, alpha))
    if n_full % 2:
        m, alpha = tick(n_chunks - 1, (n_chunks - 1) % 2, m, alpha)
    last = (n_chunks - 1) % 2
    m, alpha_last = softmax(last, m)
    values(n_chunks - 2, 1 - last, alpha)
    values(n_chunks - 1, last, alpha_last)

    lp = lam_ref[...]
    lam = (jnp.exp(jnp.sum(lp[0:1] * lp[1:2], axis=-1, keepdims=True))
           - jnp.exp(jnp.sum(lp[2:3] * lp[3:4], axis=-1, keepdims=True)) + lam_init)
    acc = acc_ref[...]
    o_t = acc[:LANES] / acc[LANES:LANES + 1]
    o_t = o_t[:, :tq] - lam * o_t[:, tq:]
    o_t = o_t * lax.rsqrt(jnp.mean(o_t * o_t, axis=0, keepdims=True) + LN_EPS)
    o_ref[0, 0] = (jnp.transpose(o_t) * g_ref[...] * (1.0 - lam_init)).astype(o_ref.dtype)


def _attn_call(q, k, v, kc, vc, lam_params, subln_g, *, lam_init, tq):
    bsz, h, s, _ = q.shape
    c_len = kc.shape[2]
    tq = min(tq, s)
    tk = max(t for t in (MXU_DEPTH, 2 * MXU_DEPTH, 3 * MXU_DEPTH) if (c_len + s) % t == 0)
    assert (c_len + s) // tk >= 3 and s % 512 == 0
    whole = lambda n: pl.BlockSpec((1, 1, n, LANES), lambda b, hh, i: (b, hh, 0, 0))
    return pl.pallas_call(
        functools.partial(_attn_kernel, tk=tk, lam_init=lam_init),
        grid=(bsz, h, s // tq),
        in_specs=[
            pl.BlockSpec((8, LANES), lambda b, hh, i: (0, 0)),
            pl.BlockSpec((1, LANES), lambda b, hh, i: (0, 0)),
            pl.BlockSpec((1, 1, tq, LANES), lambda b, hh, i: (b, hh, i, 0)),
            whole(c_len), whole(c_len), whole(s), whole(s),
        ],
        out_specs=pl.BlockSpec((1, 1, tq, LANES), lambda b, hh, i: (b, hh, i, 0)),
        out_shape=jax.ShapeDtypeStruct((bsz, h, s, LANES), BF16),
        scratch_shapes=[
            pltpu.VMEM((c_len + s, LANES), BF16),
            pltpu.VMEM((LANES + ONES_ROWS, c_len + s), BF16),
            pltpu.VMEM((2, tk, 2 * tq), F32),
            pltpu.VMEM((2, tk, 2 * tq), BF16),
            pltpu.VMEM((LANES + ONES_ROWS, 2 * tq), F32),
        ],
        compiler_params=_cparams(("arbitrary", "arbitrary", "arbitrary")),
        name="diff_attn",
    )(lam_params, subln_g, q, kc, vc, k, v)


def _gated_conv(xa_ref, bg_ref, cg_ref, xap_ref, cgp_ref, xan_ref, cgn_ref, w_ref):
    i = pl.program_id(1)
    tm = xa_ref.shape[1]
    halo = xap_ref.shape[1]
    z = xa_ref[0].astype(F32) * cg_ref[0].astype(F32)
    z_before = (xap_ref[0].astype(F32) * cgp_ref[0].astype(F32))[halo - 1:halo]
    z_after = (xan_ref[0].astype(F32) * cgn_ref[0].astype(F32))[0:1]
    z_before = jnp.where(i == 0, 0.0, z_before)
    z_after = jnp.where(i == pl.num_programs(1) - 1, 0.0, z_after)
    row = lax.broadcasted_iota(jnp.int32, z.shape, 0)
    z_prev = jnp.where(row == 0, z_before, pltpu.roll(z, 1, axis=0))
    z_next = jnp.where(row == tm - 1, z_after, pltpu.roll(z, tm - 1, axis=0))
    w = w_ref[...]
    conv = w[0:1] * z_prev + w[1:2] * z + w[2:3] * z_next
    return bg_ref[0].astype(F32) * conv


def _gated_conv_operands(proj, conv_w, *, width, tm):
    s = proj.shape[1]
    halo = 16
    nh = tm // halo
    last = s // halo - 1
    main = lambda c: pl.BlockSpec((1, tm, width), lambda b, i: (b, i, c))
    prev = lambda c: pl.BlockSpec((1, halo, width), lambda b, i: (b, jnp.maximum(i * nh - 1, 0), c))
    nxt = lambda c: pl.BlockSpec((1, halo, width), lambda b, i: (b, jnp.minimum((i + 1) * nh, last), c))
    w_pad = jnp.zeros((8, width), F32).at[:conv_w.shape[0]].set(conv_w)
    specs = [main(0), main(1), main(2), prev(0), prev(2), nxt(0), nxt(2), pl.BlockSpec((8, width), lambda b, i: (0, 0))]
    return specs, [proj] * 7 + [w_pad]


def _gmlp_gate(u_ref, v_ref, g_ref, b_ref, ws_ref, bs_ref, a_ref):
    tm = u_ref.shape[1]
    v = _layer_norm(v_ref[0].astype(F32), g_ref[...], b_ref[...]).astype(BF16)
    bs = bs_ref[...]
    for n in range(tm // CHUNK):
        rows = slice(n * CHUNK, (n + 1) * CHUNK)
        for g in range(GMLP_GROUPS):
            cols = slice(g * LANES, (g + 1) * LANES)
            sg = jnp.dot(ws_ref[g], v[rows, cols], preferred_element_type=F32) + bs[:, g:g + 1]
            a_ref[rows, cols] = (u_ref[0, rows, cols].astype(F32) * sg).astype(a_ref.dtype)


def _out_ln_kernel(*refs, alpha, mixer):
    if mixer == "conv_attn":
        conv_refs, (heads_ref, w_ref, x_ref, gate_ref, g_ref, b_ref, o_ref) = refs[:8], refs[8:]
        a = jnp.concatenate([_gated_conv(*conv_refs).astype(BF16)]
                            + [heads_ref[0, h] for h in range(heads_ref.shape[1])], axis=1)
    else:
        gmlp_refs, (w_ref, x_ref, gate_ref, g_ref, b_ref, o_ref, a_ref) = refs[:6], refs[6:]
        _gmlp_gate(*gmlp_refs, a_ref)
        a = a_ref[...]
    half = a.shape[0] // 2
    for rows in (slice(0, half), slice(half, 2 * half)):
        y = jnp.dot(a[rows], w_ref[...], preferred_element_type=F32)
        r = alpha * x_ref[0, rows, :] + gate_ref[0] * y
        o_ref[0, rows, :] = _layer_norm(r, g_ref[...], b_ref[...])


def _out_ln_call(mixer, mixer_specs, mixer_args, w, x, gate, ln_g, ln_b, *, alpha, tm):
    bsz, s, d = x.shape
    scratch = [pltpu.VMEM((tm, w.shape[0]), BF16)] if mixer == "gmlp" else []
    return pl.pallas_call(
        functools.partial(_out_ln_kernel, alpha=alpha, mixer=mixer),
        grid=(bsz, s // tm),
        in_specs=mixer_specs + [
            pl.BlockSpec(w.shape, lambda b, i: (0, 0)),
            pl.BlockSpec((1, tm, d), lambda b, i: (b, i, 0)),
            pl.BlockSpec((1, 1, d), lambda b, i: (b, 0, 0)),
            pl.BlockSpec((1, d), lambda b, i: (0, 0)),
            pl.BlockSpec((1, d), lambda b, i: (0, 0)),
        ],
        out_specs=pl.BlockSpec((1, tm, d), lambda b, i: (b, i, 0)),
        out_shape=jax.ShapeDtypeStruct((bsz, s, d), F32),
        scratch_shapes=scratch,
        compiler_params=_cparams(("arbitrary", "arbitrary")),
        name="out_proj_ln",
    )(*mixer_args, w, x, gate, ln_g, ln_b)


def _conv_attn_operands(proj, conv_w, attn, *, tm):
    specs, args = _gated_conv_operands(proj, conv_w, width=conv_w.shape[-1], tm=tm)
    specs.append(pl.BlockSpec((1, attn.shape[1], tm, LANES), lambda b, i: (b, 0, i, 0)))
    return specs, args + [attn]


def _gmlp_operands(uv, v_g, v_b, w_s, b_s_t, *, tm):
    d = uv.shape[2] // 2
    specs = [
        pl.BlockSpec((1, tm, d), lambda b, i: (b, i, 0)),
        pl.BlockSpec((1, tm, d), lambda b, i: (b, i, 1)),
        pl.BlockSpec((1, d), lambda b, i: (0, 0)),
        pl.BlockSpec((1, d), lambda b, i: (0, 0)),
        pl.BlockSpec(w_s.shape, lambda b, i: (0, 0, 0)),
        pl.BlockSpec(b_s_t.shape, lambda b, i: (0, 0)),
    ]
    return specs, [uv, uv, v_g, v_b, w_s, b_s_t]


def _route(scores, bias):
    sel = scores + bias
    rows = [sel[e:e + 1, :] for e in range(N_EXPERTS)]
    group_score = []
    for g in range(N_GROUPS):
        r = rows[g * EXPERTS_PER_GROUP:(g + 1) * EXPERTS_PER_GROUP]
        best = None
        for a in range(EXPERTS_PER_GROUP):
            for b in range(a + 1, EXPERTS_PER_GROUP):
                pair = r[a] + r[b]
                best = pair if best is None else jnp.maximum(best, pair)
        group_score.append(best)
    one = jnp.ones_like(rows[0])
    zero = jnp.zeros_like(rows[0])
    picked = []
    chosen = []
    for g in range(N_GROUPS):
        beaten = zero
        for o in range(N_GROUPS):
            if o < g:
                beaten = jnp.where(group_score[o] >= group_score[g], one, beaten)
            elif o > g:
                beaten = jnp.where(group_score[o] > group_score[g], one, beaten)
        chosen.append(1.0 - beaten)
        for a in range(EXPERTS_PER_GROUP):
            e = g * EXPERTS_PER_GROUP + a
            rank = zero
            for b in range(EXPERTS_PER_GROUP):
                o = g * EXPERTS_PER_GROUP + b
                if b < a:
                    rank = rank + jnp.where(rows[o] >= rows[e], one, zero)
                elif b > a:
                    rank = rank + jnp.where(rows[o] > rows[e], one, zero)
            picked.append(jnp.where(rank < 2.0, one, zero) * chosen[g])
    w = [picked[e] * scores[e:e + 1, :] for e in range(N_EXPERTS)]
    total = w[0]
    for e in range(1, N_EXPERTS):
        total = total + w[e]
    return [we / total for we in w], chosen


ROW_BLOCK = 32
STEP_BLOCKS = 8
DEST_LANE = EXPERTS_PER_GROUP
SORT_TM = 512


def _sort_rows(tm):
    return tm + N_GROUPS * ROW_BLOCK


def _moe_sort_kernel(x_ref, sh_ref, sc_ref, wr_ref, rb_ref, xs_ref, cs_ref, tok_ref, seg_ref):
    tm, d = x_ref.shape[1], x_ref.shape[2]
    rows = xs_ref.shape[0]
    col_chunk = 512
    h = (x_ref[0] * (1.0 + sc_ref[0]) + sh_ref[0]).astype(BF16)
    logits = lax.dot_general(wr_ref[...], h, (((1,), (1,)), ((), ())), preferred_element_type=F32)
    comb, chosen = _route(jax.nn.sigmoid(logits), rb_ref[...])
    src = lax.broadcasted_iota(jnp.int32, (tm, tm), 0)
    dst = lax.broadcasted_iota(jnp.int32, (tm, tm), 1)
    before = jnp.where(src < dst, 1.0, 0.0).astype(BF16)
    pad_rows = [jnp.zeros_like(chosen[0])] * (8 - N_GROUPS)
    rank = jnp.dot(jnp.concatenate(chosen + pad_rows, axis=0).astype(BF16), before, preferred_element_type=F32)
    dest = jnp.zeros_like(chosen[0])
    first = jnp.zeros((1, 1), F32)
    firsts, counts = [], []
    for g in range(N_GROUPS):
        count = jnp.sum(chosen[g], axis=1, keepdims=True)
        n_blocks = jnp.floor((count + (ROW_BLOCK - 1)) * (1.0 / ROW_BLOCK))
        firsts.append(first)
        counts.append(n_blocks)
        dest = dest + chosen[g] * (first * ROW_BLOCK + rank[g:g + 1])
        first = first + n_blocks
    seg = jnp.concatenate([jnp.broadcast_to(v, (1, LANES)) for v in firsts + counts], axis=0)
    seg_ref[0] = seg.astype(jnp.int32)
    dest_i = dest.astype(jnp.int32)
    row_id = lax.broadcasted_iota(jnp.int32, (rows, tm), 0)
    p = jnp.where(row_id == dest_i, 1.0, 0.0).astype(BF16)
    for c in range(d // col_chunk):
        cols = slice(c * col_chunk, (c + 1) * col_chunk)
        xs_ref[:, cols] = jnp.dot(p, h[:, cols], preferred_element_type=F32).astype(BF16)
    in_group = []
    for j in range(EXPERTS_PER_GROUP):
        cj = comb[j]
        for g in range(1, N_GROUPS):
            cj = cj + comb[g * EXPERTS_PER_GROUP + j]
        in_group.append(cj)
    c_rows = jnp.concatenate(in_group + [dest, jnp.zeros((LANES - DEST_LANE - 1, tm), F32)], axis=0)
    c_cols = jnp.transpose(c_rows)
    tok_ref[...] = c_cols
    hi = c_cols.astype(BF16)
    lo = (c_cols - hi.astype(F32)).astype(BF16)
    cs = jnp.dot(p, jnp.concatenate([hi, lo], axis=1), preferred_element_type=F32)
    cs_ref[...] = cs[:, :LANES] + cs[:, LANES:]


def _moe_sort_call(x, shift, scale, w_router_t, router_bias, *, tm):
    bsz, s, d = x.shape
    nt = s // tm
    n_tiles = bsz * nt
    rows = _sort_rows(tm)
    tile = lambda b, i: b * nt + i
    vec = lambda: pl.BlockSpec((1, 1, d), lambda b, i: (b, 0, 0))
    return pl.pallas_call(
        _moe_sort_kernel,
        grid=(bsz, nt),
        in_specs=[
            pl.BlockSpec((1, tm, d), lambda b, i: (b, i, 0)),
            vec(), vec(),
            pl.BlockSpec(w_router_t.shape, lambda b, i: (0, 0)),
            pl.BlockSpec(router_bias.shape, lambda b, i: (0, 0)),
        ],
        out_specs=[
            pl.BlockSpec((rows, d), lambda b, i: (tile(b, i), 0)),
            pl.BlockSpec((rows, LANES), lambda b, i: (tile(b, i), 0)),
            pl.BlockSpec((tm, LANES), lambda b, i: (tile(b, i), 0)),
            pl.BlockSpec((1, 2 * N_GROUPS, LANES), lambda b, i: (tile(b, i), 0, 0)),
        ],
        out_shape=[
            jax.ShapeDtypeStruct(((n_tiles + 1) * rows, d), BF16),
            jax.ShapeDtypeStruct(((n_tiles + 1) * rows, LANES), F32),
            jax.ShapeDtypeStruct((n_tiles * tm, LANES), F32),
            jax.ShapeDtypeStruct((n_tiles, 2 * N_GROUPS, LANES), jnp.int32),
        ],
        compiler_params=_cparams(("arbitrary", "arbitrary")),
        name="moe_sort",
    )(x, shift, scale, w_router_t, router_bias)


def _expert_work_table(seg, rows):
    n_tiles = seg.shape[0]
    per_tile = rows // ROW_BLOCK
    spare = n_tiles * per_tile
    max_items = spare // STEP_BLOCKS + N_GROUPS
    first = seg[:, :N_GROUPS, 0]
    count = seg[:, N_GROUPS:, 0]
    tri = jnp.arange(n_tiles)[:, None] >= jnp.arange(n_tiles)[None, :]
    cum = jnp.sum(jnp.where(tri[:, :, None], count[None, :, :], 0), axis=1)
    total = cum[-1]
    items = (total + (STEP_BLOCKS - 1)) // STEP_BLOCKS
    ends = jnp.sum(jnp.where(jnp.arange(N_GROUPS)[:, None] >= jnp.arange(N_GROUPS)[None, :], items[None, :], 0),
                   axis=1)
    n_items = ends[-1]
    step = jnp.arange(max_items + 1)
    live = step < n_items
    ref_step = jnp.where(live, step, jnp.maximum(n_items - 1, 0))
    grp = jnp.sum((ref_step[:, None] >= ends[None, :-1]).astype(jnp.int32), axis=1)
    pick = grp[:, None] == jnp.arange(N_GROUPS)[None, :]
    of_group = lambda v: jnp.sum(jnp.where(pick, v[None, :], 0), axis=1)
    start = of_group(ends - items)
    cum_s = jnp.sum(jnp.where(pick[:, None, :], cum[None, :, :], 0), axis=2)
    count_s = jnp.sum(jnp.where(pick[:, None, :], count[None, :, :], 0), axis=2)
    first_s = jnp.sum(jnp.where(pick[:, None, :], first[None, :, :], 0), axis=2)
    total_s = of_group(total)

    def block_id(k):
        done = cum_s <= k[:, None]
        t = jnp.sum(done.astype(jnp.int32), axis=1)
        skipped = jnp.sum(jnp.where(done, count_s, 0), axis=1)
        at_t = jnp.arange(n_tiles)[None, :] == t[:, None]
        return t * per_tile + jnp.sum(jnp.where(at_t, first_s, 0), axis=1) + (k - skipped)

    as_i32 = lambda v: v.astype(jnp.int32)
    block = jnp.arange(spare)[:, None]
    blocks = []
    where = jnp.zeros((spare,), jnp.int32)
    found = jnp.zeros((spare,), jnp.bool_)
    for u in range(STEP_BLOCKS):
        k_u = STEP_BLOCKS * (step - start) + u
        has_u = live & (k_u < total_s)
        blk_u = jnp.where(has_u, block_id(k_u), blocks[0] if u else spare)
        blocks.append(as_i32(blk_u))
        hit = has_u[None, :] & (blk_u[None, :] == block)
        where = where + jnp.sum(jnp.where(hit, STEP_BLOCKS * step[None, :] + u, 0), axis=1)
        found = found | jnp.any(hit, axis=1)
    where = jnp.where(found, where, STEP_BLOCKS * max_items)
    return (*blocks, as_i32(grp), as_i32(n_items).reshape(1)), as_i32(where)


def _moe_expert_kernel(*refs):
    n_items = refs[STEP_BLOCKS + 1]
    pos = STEP_BLOCKS + 2
    x_refs = refs[pos:pos + STEP_BLOCKS]
    c_refs = refs[pos + STEP_BLOCKS:pos + 2 * STEP_BLOCKS]
    wg_ref, wu_ref, wd_ref, y_ref = refs[pos + 2 * STEP_BLOCKS:]
    f, d = wd_ref.shape[2], wd_ref.shape[3]

    @pl.when(pl.program_id(0) < n_items[0])
    def _():
        x = jnp.concatenate([r[...] for r in x_refs], axis=0)
        c = jnp.concatenate([r[...] for r in c_refs], axis=0)
        pieces = []
        for j in range(EXPERTS_PER_GROUP):
            gate = jnp.dot(x, wg_ref[0, j].astype(BF16), preferred_element_type=F32)
            up = jnp.dot(x, wu_ref[0, j].astype(BF16), preferred_element_type=F32)
            pieces.append((jax.nn.silu(gate) * up * c[:, j:j + 1]).astype(BF16))
        w_down = wd_ref[0].reshape(EXPERTS_PER_GROUP * f, d).astype(BF16)
        y = jnp.dot(jnp.concatenate(pieces, axis=1), w_down, preferred_element_type=F32)
        y_ref[...] = y.astype(BF16)

    @pl.when(pl.program_id(0) >= n_items[0])
    def _():
        y_ref[...] = jnp.zeros(y_ref.shape, BF16)


def _moe_expert_call(table, xs, cs, w_gate, w_up, w_down, layer):
    d = xs.shape[1]
    steps = table[0].shape[0]
    n_pre = len(table)
    grp_at = STEP_BLOCKS
    rows_of = lambda u, width: pl.BlockSpec((ROW_BLOCK, width), lambda i, *pre: (pre[u][i], 0))
    weights = lambda w: pl.BlockSpec((1,) + w.shape[1:],
                                     lambda i, *pre: (layer * N_GROUPS + pre[grp_at][i], 0, 0, 0),
                                     pipeline_mode=pl.Buffered(1))
    grid_spec = pltpu.PrefetchScalarGridSpec(
        num_scalar_prefetch=n_pre,
        grid=(steps,),
        in_specs=[rows_of(u, d) for u in range(STEP_BLOCKS)] + [rows_of(u, LANES) for u in range(STEP_BLOCKS)]
        + [weights(w_gate), weights(w_up), weights(w_down)],
        out_specs=pl.BlockSpec((STEP_BLOCKS * ROW_BLOCK, d), lambda i, *pre: (i, 0)),
    )
    return pl.pallas_call(
        _moe_expert_kernel,
        grid_spec=grid_spec,
        out_shape=jax.ShapeDtypeStruct((steps * STEP_BLOCKS * ROW_BLOCK, d), BF16),
        compiler_params=_cparams(("arbitrary",)),
        name="moe_experts",
    )(*table, *([xs] * STEP_BLOCKS), *([cs] * STEP_BLOCKS), w_gate, w_up, w_down)


def _moe_combine_kernel(where_ref, tok_ref, *refs, alpha, n_blocks):
    y_refs = refs[:n_blocks]
    x_ref, gate_ref, g_ref, b_ref, o_ref = refs[n_blocks:]
    tm = x_ref.shape[1]
    ys = jnp.concatenate([r[...] for r in y_refs], axis=0)
    half = tm // 2
    row_id = lax.broadcasted_iota(jnp.int32, (half, ys.shape[0]), 1)
    for rows in (slice(0, half), slice(half, tm)):
        dest = tok_ref[rows, DEST_LANE:DEST_LANE + 1].astype(jnp.int32)
        p_t = jnp.where(row_id == dest, 1.0, 0.0).astype(BF16)
        y = jnp.dot(p_t, ys, preferred_element_type=F32)
        r = alpha * x_ref[0, rows, :] + gate_ref[0] * y
        o_ref[0, rows, :] = _layer_norm(r, g_ref[...], b_ref[...])


def _moe_combine_call(where, tok, ys, x, gate, ln_g, ln_b, *, alpha, sort_tm, tm):
    bsz, s, d = x.shape
    nt = s // sort_tm
    parts = sort_tm // tm
    n_blocks = _sort_rows(sort_tm) // ROW_BLOCK
    tile = lambda b, i: b * nt + i
    y_spec = lambda j: pl.BlockSpec((ROW_BLOCK, d), lambda b, i, k, w: (w[tile(b, i) * n_blocks + j], 0))
    grid_spec = pltpu.PrefetchScalarGridSpec(
        num_scalar_prefetch=1,
        grid=(bsz, nt, parts),
        in_specs=[pl.BlockSpec((tm, LANES), lambda b, i, k, w: (tile(b, i) * parts + k, 0))]
        + [y_spec(j) for j in range(n_blocks)]
        + [
            pl.BlockSpec((1, tm, d), lambda b, i, k, w: (b, i * parts + k, 0)),
            pl.BlockSpec((1, 1, d), lambda b, i, k, w: (b, 0, 0)),
            pl.BlockSpec((1, d), lambda b, i, k, w: (0, 0)),
            pl.BlockSpec((1, d), lambda b, i, k, w: (0, 0)),
        ],
        out_specs=pl.BlockSpec((1, tm, d), lambda b, i, k, w: (b, i * parts + k, 0)),
    )
    return pl.pallas_call(
        functools.partial(_moe_combine_kernel, alpha=alpha, n_blocks=n_blocks),
        grid_spec=grid_spec,
        out_shape=jax.ShapeDtypeStruct((bsz, s, d), F32),
        compiler_params=_cparams(("arbitrary", "arbitrary", "arbitrary")),
        name="moe_combine",
    )(where, tok, *([ys] * n_blocks), x, gate, ln_g, ln_b)


def _moe(x, shift, scale, gate, w_router_t, router_bias, w_gate, w_up, w_down, layer, ln_g, ln_b, *, alpha):
    sort_tm = min(SORT_TM, x.shape[1])
    xs, cs, tok, seg = _moe_sort_call(x, shift, scale, w_router_t, router_bias, tm=sort_tm)
    table, where = _expert_work_table(seg, _sort_rows(sort_tm))
    by_group = lambda w: w.reshape((w.shape[0] * N_GROUPS, EXPERTS_PER_GROUP) + w.shape[2:])
    ys = _moe_expert_call(table, xs, cs, by_group(w_gate), by_group(w_up), by_group(w_down), layer)
    return _moe_combine_call(where, tok, ys, x, gate, ln_g, ln_b, alpha=alpha, sort_tm=sort_tm,
                             tm=min(ROW_TM, sort_tm))


def _rope_tables(n):
    freqs = LANES // 8
    inv = jnp.power(ROPE_BASE, -jnp.arange(freqs, dtype=F32) / freqs)
    tok = jnp.arange(n)
    ang_r = (tok // GRID_W).astype(F32)[:, None] * inv
    ang_c = (tok % GRID_W).astype(F32)[:, None] * inv
    ang = jnp.concatenate([ang_r, ang_c], axis=1)
    cos = jnp.tile(jnp.cos(ang), (1, 4))
    sin = jnp.tile(jnp.sin(ang), (1, 4))
    sign = jnp.where(jnp.arange(LANES) < LANES // 2, -1.0, 1.0).astype(F32)
    return cos, sin * sign


def _rope_column_perm():
    perm = np.zeros((DA_HEADS, 2, 2, 2, 16), np.int32)
    for h in range(DA_HEADS):
        for p in range(2):
            for m in range(2):
                for ax in range(2):
                    for f in range(16):
                        perm[h, p, m, ax, f] = h * LANES + m * 64 + ax * 32 + p * 16 + f
    return perm.reshape(-1)


def kernel(x, c, ctx, c_ctx, w_mod, b_mod, ln_g, ln_b, w_in_e, conv_w, lambda_q1, lambda_k1, lambda_q2, lambda_k2,
           subln_g, w_out_e, w_in_o, v_ln_g, v_ln_b, w_spatial, b_spatial, w_out_o, w_router, router_bias,
           w_gate, w_up, w_down):
    bsz, s, d = x.shape
    depth = w_mod.shape[0]
    assert depth == 2 and w_in_e.shape[0] == 1 and w_in_o.shape[0] == 1, "two-layer (even, odd) stack only"
    conv_dim = conv_w.shape[-1]
    q_dim = DA_HEADS * LANES
    q_col, k_col, v_col = 3 * conv_dim, 3 * conv_dim + q_dim, 3 * conv_dim + 2 * q_dim
    assert w_in_e.shape[2] == v_col + q_dim and s % GRID_W == 0
    alpha = float((2 * depth) ** 0.25)

    rows = 8 * ((bsz + 1 + 7) // 8)
    cond = jnp.zeros((rows, d), F32).at[:bsz].set(c).at[bsz].set(c_ctx)
    mods = _mod_call(cond, w_mod, b_mod)

    def mod_vec(l, k, ctx_row=False):
        v = mods[l, bsz:bsz + 1, k * d:(k + 1) * d] if ctx_row else mods[l, :bsz, k * d:(k + 1) * d]
        return v.reshape(-1, 1, d)

    w_router_t = w_router.T.astype(BF16)
    rbias = router_bias.reshape(-1, 1).astype(F32)

    perm = _rope_column_perm()
    w_in = w_in_e[0]
    w_in = jnp.concatenate(
        [w_in[:, :q_col], w_in[:, q_col:k_col][:, perm], w_in[:, k_col:v_col][:, perm], w_in[:, v_col:]],
        axis=1).astype(BF16)
    cos_t, sin_t = _rope_tables(s)
    tn = q_dim
    assert q_col % tn == 0
    q_tile, k_tile = q_col // tn, k_col // tn
    conv_proj, q, k, v = _proj_call(
        x, mod_vec(0, 0), mod_vec(0, 1), w_in, tm=PROJ_TM, tn=tn, n_flat=q_tile,
        rope=(cos_t, sin_t, (q_tile, k_tile), q_tile, float(64 ** -0.5 * math.log2(math.e))))
    kc, vc = _proj_call(ctx, mod_vec(0, 0, True), mod_vec(0, 1, True), w_in[:, k_col:], tm=PROJ_TM, tn=tn, n_flat=0)

    lam_init = 0.8 - 0.6 * math.exp(-0.3 * 0)
    lam_params = jnp.zeros((8, LANES), F32)
    for r, p in enumerate((lambda_q1, lambda_k1, lambda_q2, lambda_k2)):
        lam_params = lam_params.at[r, :p.shape[-1]].set(p[0].astype(F32))
    attn = _attn_call(q, k, v, kc, vc, lam_params, subln_g[0].reshape(1, LANES), lam_init=lam_init, tq=ATTN_TQ)
    x = _out_ln_call("conv_attn", *_conv_attn_operands(conv_proj, conv_w[0], attn, tm=ROW_TM), w_out_e[0].astype(BF16),
                     x, mod_vec(0, 2), ln_g[0, 0:1], ln_b[0, 0:1], alpha=alpha, tm=ROW_TM)
    x = _moe(x, mod_vec(0, 3), mod_vec(0, 4), mod_vec(0, 5), w_router_t, rbias,
             w_gate, w_up, w_down, 0, ln_g[0, 1:2], ln_b[0, 1:2], alpha=alpha)

    w_gmlp = w_in_o[0].astype(BF16)
    uv, = _proj_call(x, mod_vec(1, 0), mod_vec(1, 1), w_gmlp, tm=PROJ_TM, tn=tn, n_flat=w_gmlp.shape[1] // tn,
                     gelu=True)
    gmlp = _gmlp_operands(uv, v_ln_g[0:1], v_ln_b[0:1], w_spatial[0].astype(BF16), b_spatial[0].T, tm=ROW_TM)
    x = _out_ln_call("gmlp", *gmlp, w_out_o[0].astype(BF16), x, mod_vec(1, 2), ln_g[1, 0:1], ln_b[1, 0:1],
                     alpha=alpha, tm=ROW_TM)
    x = _moe(x, mod_vec(1, 3), mod_vec(1, 4), mod_vec(1, 5), w_router_t, rbias,
             w_gate, w_up, w_down, 1, ln_g[1, 1:2], ln_b[1, 1:2], alpha=alpha)
    return x
```

```python
import functools
import math

import numpy as np
import jax
import jax.numpy as jnp
from jax import lax
from jax.experimental import pallas as pl
from jax.experimental.pallas import tpu as pltpu

F32 = jnp.float32
BF16 = jnp.bfloat16

GRID_W = 64
DA_HEADS = 8
N_EXPERTS = 16
N_GROUPS = 4
EXPERTS_PER_GROUP = N_EXPERTS // N_GROUPS
ROPE_BASE = 10000.0
LN_EPS = 1e-5
CHUNK = 128
GMLP_GROUPS = 16
LANES = 128
MXU_DEPTH = 256

VMEM_LIMIT = 56 * 1024 * 1024
PROJ_TM = 1024
ROW_TM = 512
ATTN_TQ = 1024


def _cparams(sem):
    return pltpu.CompilerParams(dimension_semantics=sem, vmem_limit_bytes=VMEM_LIMIT)


def _layer_norm(r, g, b):
    mu = jnp.mean(r, axis=-1, keepdims=True)
    d = r - mu
    var = jnp.mean(d * d, axis=-1, keepdims=True)
    return d * lax.rsqrt(var + LN_EPS) * g + b


def _mod_kernel(c_ref, w_ref, b_ref, o_ref):
    c = c_ref[...]
    s = (c * jax.nn.sigmoid(c)).astype(BF16)
    o_ref[0] = jnp.dot(s, w_ref[0].astype(BF16), preferred_element_type=F32) + b_ref[0]


def _mod_call(cond, w_mod, b_mod):
    depth, d, n = w_mod.shape
    rows = cond.shape[0]
    tn = 1024
    return pl.pallas_call(
        _mod_kernel,
        grid=(depth, n // tn),
        in_specs=[
            pl.BlockSpec((rows, d), lambda l, j: (0, 0)),
            pl.BlockSpec((1, d, tn), lambda l, j: (l, 0, j)),
            pl.BlockSpec((1, 1, tn), lambda l, j: (l, 0, j)),
        ],
        out_specs=pl.BlockSpec((1, rows, tn), lambda l, j: (l, 0, j)),
        out_shape=jax.ShapeDtypeStruct((depth, rows, n), F32),
        compiler_params=_cparams(("arbitrary", "arbitrary")),
        name="adaln_mod",
    )(cond, w_mod, b_mod.reshape(depth, 1, n))


def _proj_kernel(*refs, n_flat, n_head_tiles, rope, gelu):
    x_ref, sh_ref, sc_ref, w_ref = refs[:4]
    pos = 4
    if rope is not None:
        cos_ref, sin_ref = refs[pos:pos + 2]
        pos += 2
    flat_ref = refs[pos] if n_flat else None
    pos += 1 if n_flat else 0
    head_refs = refs[pos:pos + n_head_tiles]
    h_ref = refs[pos + n_head_tiles]
    j = pl.program_id(2)

    tn = w_ref.shape[1]
    halves = [slice(c * (tn // 2), (c + 1) * (tn // 2)) for c in range(2)]
    activate = (lambda acc: jax.nn.gelu(acc, approximate=True)) if gelu else (lambda acc: acc)

    def modulated(rows):
        return (x_ref[0, rows, :] * (1.0 + sc_ref[0]) + sh_ref[0]).astype(BF16)

    def product(cols):
        return jnp.dot(h_ref[...], w_ref[:, cols], preferred_element_type=F32)

    if n_flat:
        @pl.when(j == 0)
        def _():
            tm = h_ref.shape[0]
            for rows in (slice(0, tm // 2), slice(tm // 2, tm)):
                h = modulated(rows)
                h_ref[rows, :] = h
                for cols in halves:
                    acc = jnp.dot(h, w_ref[:, cols], preferred_element_type=F32)
                    flat_ref[0, rows, cols] = activate(acc).astype(flat_ref.dtype)

        @pl.when(jnp.logical_and(j > 0, j < n_flat))
        def _():
            for cols in halves:
                flat_ref[0, :, cols] = activate(product(cols)).astype(flat_ref.dtype)
    else:
        @pl.when(j == 0)
        def _():
            h_ref[...] = modulated(slice(None))

    for t in range(n_head_tiles):
        tile = n_flat + t
        o_ref = head_refs[t]

        @pl.when(j == tile)
        def _(tile=tile, o_ref=o_ref):
            rotary = rope is not None and tile in rope[0]
            if rotary:
                scale = rope[2] if tile == rope[1] else 1.0
                cs = cos_ref[...] * scale
                sn = sin_ref[...] * scale
            for cols in halves:
                acc = product(cols)
                for h in range(acc.shape[1] // LANES):
                    piece = acc[:, h * LANES:(h + 1) * LANES]
                    if rotary:
                        piece = piece * cs + pltpu.roll(piece, LANES // 2, axis=1) * sn
                    o_ref[0, cols.start // LANES + h] = piece.astype(o_ref.dtype)


def _proj_call(x, shift, scale, w, *, tm, tn, n_flat, rope=None, gelu=False):
    bsz, s, d = x.shape
    n_tiles = w.shape[1] // tn
    n_head_tiles = n_tiles - n_flat
    heads = tn // LANES
    tm = min(tm, s)
    per_batch = shift.shape[0] > 1
    mod_map = (lambda b, i, j: (b, 0, 0)) if per_batch else (lambda b, i, j: (0, 0, 0))
    in_specs = [
        pl.BlockSpec((1, tm, d), lambda b, i, j: (b, i, 0)),
        pl.BlockSpec((1, 1, d), mod_map),
        pl.BlockSpec((1, 1, d), mod_map),
        pl.BlockSpec((d, tn), lambda b, i, j: (0, j)),
    ]
    args = [x, shift, scale, w]
    rope_static = None
    if rope is not None:
        cos_t, sin_t, rope_tiles, q_tile, q_scale = rope
        in_specs += [pl.BlockSpec((tm, LANES), lambda b, i, j: (i, 0))] * 2
        args += [cos_t, sin_t]
        rope_static = (tuple(rope_tiles), q_tile, q_scale)
    out_specs, out_shape = [], []
    if n_flat:
        out_specs.append(pl.BlockSpec((1, tm, tn), lambda b, i, j: (b, i, jnp.minimum(j, n_flat - 1))))
        out_shape.append(jax.ShapeDtypeStruct((bsz, s, n_flat * tn), BF16))
    for _ in range(n_head_tiles):
        out_specs.append(pl.BlockSpec((1, heads, tm, LANES), lambda b, i, j: (b, 0, i, 0)))
        out_shape.append(jax.ShapeDtypeStruct((bsz, heads, s, LANES), BF16))
    return pl.pallas_call(
        functools.partial(_proj_kernel, n_flat=n_flat, n_head_tiles=n_head_tiles, rope=rope_static, gelu=gelu),
        grid=(bsz, s // tm, n_tiles),
        in_specs=in_specs,
        out_specs=out_specs,
        out_shape=out_shape,
        scratch_shapes=[pltpu.VMEM((tm, d), BF16)],
        compiler_params=_cparams(("arbitrary", "arbitrary", "arbitrary")),
        name="mod_proj",
    )(*args)


ONES_ROWS = 16


def _attn_kernel(lam_ref, g_ref, q_ref, kc_ref, vc_ref, k_ref, v_ref, o_ref, kf_ref, vt_ref, s_ref, p_ref, acc_ref,
                 *, tk, lam_init):
    tq = q_ref.shape[2]
    c_len = kc_ref.shape[2]
    s_len = k_ref.shape[2]
    n_chunks = (c_len + s_len) // tk
    vt_chunk = 512

    @pl.when(pl.program_id(2) == 0)
    def _():
        kf_ref[:c_len, :] = kc_ref[0, 0]
        kf_ref[c_len:, :] = k_ref[0, 0]
        vt_ref[LANES:, :] = jnp.ones((ONES_ROWS, vt_ref.shape[1]), BF16)
        vt_ref[:LANES, :c_len] = jnp.transpose(vc_ref[0, 0].astype(F32)).astype(BF16)
        for n in range(s_len // vt_chunk):
            rows = slice(n * vt_chunk, (n + 1) * vt_chunk)
            cols = slice(c_len + n * vt_chunk, c_len + (n + 1) * vt_chunk)
            vt_ref[:LANES, cols] = jnp.transpose(v_ref[0, 0, rows, :].astype(F32)).astype(BF16)

    q_t = jnp.transpose(q_ref[0, 0].astype(F32))
    dim = lax.broadcasted_iota(jnp.int32, q_t.shape, 0)
    first_map = (dim % (LANES // 2)) < (LANES // 4)
    qs_t = jnp.concatenate([jnp.where(first_map, q_t, 0.0), jnp.where(first_map, 0.0, q_t)], axis=1).astype(BF16)

    acc_ref[...] = jnp.zeros(acc_ref.shape, F32)

    def scores(t):
        off = t * tk if isinstance(t, int) else pl.multiple_of(t * tk, tk)
        return jnp.dot(kf_ref[pl.ds(off, tk), :], qs_t, preferred_element_type=F32)

    def softmax(slot, m_old):
        s_t = s_ref[slot]
        m_new = jnp.maximum(m_old, jnp.max(s_t, axis=0, keepdims=True))
        p_ref[slot] = jnp.exp2((s_t - m_new).astype(BF16))
        return m_new, jnp.exp2(m_old - m_new)

    def values(t, slot, alpha):
        off = t * tk if isinstance(t, int) else pl.multiple_of(t * tk, tk)
        pv = jnp.dot(vt_ref[:, pl.ds(off, tk)], p_ref[slot], preferred_element_type=F32)
        acc_ref[...] = alpha * acc_ref[...] + pv

    def tick(t, parity, m, alpha_prev):
        s_ref[parity] = scores(t)
        m, alpha = softmax(1 - parity, m)
        values(t - 2, parity, alpha_prev)
        return m, alpha

    s_ref[0] = scores(0)
    s_ref[1] = scores(1)
    m, alpha = softmax(0, jnp.full((1, 2 * tq), -jnp.inf, F32))

    def pair(jj, carry):
        m, alpha = tick(2 + 2 * jj, 0, *carry)
        return tick(3 + 2 * jj, 1, m, alpha)

    n_full = n_chunks - 2
    m, alpha = lax.fori_loop(0, n_full // 2, pair, (m, alpha))
    if n_full % 2:
        m, alpha = tick(n_chunks - 1, (n_chunks - 1) % 2, m, alpha)
    last = (n_chunks - 1) % 2
    m, alpha_last = softmax(last, m)
    values(n_chunks - 2, 1 - last, alpha)
    values(n_chunks - 1, last, alpha_last)

    lp = lam_ref[...]
    lam = (jnp.exp(jnp.sum(lp[0:1] * lp[1:2], axis=-1, keepdims=True))
           - jnp.exp(jnp.sum(lp[2:3] * lp[3:4], axis=-1, keepdims=True)) + lam_init)
    acc = acc_ref[...]
    o_t = acc[:LANES] / acc[LANES:LANES + 1]
    o_t = o_t[:, :tq] - lam * o_t[:, tq:]
    o_t = o_t * lax.rsqrt(jnp.mean(o_t * o_t, axis=0, keepdims=True) + LN_EPS)
    o_ref[0, 0] = (jnp.transpose(o_t) * g_ref[...] * (1.0 - lam_init)).astype(o_ref.dtype)


def _attn_call(q, k, v, kc, vc, lam_params, subln_g, *, lam_init, tq):
    bsz, h, s, _ = q.shape
    c_len = kc.shape[2]
    tq = min(tq, s)
    tk = max(t for t in (MXU_DEPTH, 2 * MXU_DEPTH, 3 * MXU_DEPTH) if (c_len + s) % t == 0)
    assert (c_len + s) // tk >= 3 and s % 512 == 0
    whole = lambda n: pl.BlockSpec((1, 1, n, LANES), lambda b, hh, i: (b, hh, 0, 0))
    return pl.pallas_call(
        functools.partial(_attn_kernel, tk=tk, lam_init=lam_init),
        grid=(bsz, h, s // tq),
        in_specs=[
            pl.BlockSpec((8, LANES), lambda b, hh, i: (0, 0)),
            pl.BlockSpec((1, LANES), lambda b, hh, i: (0, 0)),
            pl.BlockSpec((1, 1, tq, LANES), lambda b, hh, i: (b, hh, i, 0)),
            whole(c_len), whole(c_len), whole(s), whole(s),
        ],
        out_specs=pl.BlockSpec((1, 1, tq, LANES), lambda b, hh, i: (b, hh, i, 0)),
        out_shape=jax.ShapeDtypeStruct((bsz, h, s, LANES), BF16),
        scratch_shapes=[
            pltpu.VMEM((c_len + s, LANES), BF16),
            pltpu.VMEM((LANES + ONES_ROWS, c_len + s), BF16),
            pltpu.VMEM((2, tk, 2 * tq), F32),
            pltpu.VMEM((2, tk, 2 * tq), BF16),
            pltpu.VMEM((LANES + ONES_ROWS, 2 * tq), F32),
        ],
        compiler_params=_cparams(("arbitrary", "arbitrary", "arbitrary")),
        name="diff_attn",
    )(lam_params, subln_g, q, kc, vc, k, v)


def _gated_conv(xa_ref, bg_ref, cg_ref, xap_ref, cgp_ref, xan_ref, cgn_ref, w_ref):
    i = pl.program_id(1)
    tm = xa_ref.shape[1]
    halo = xap_ref.shape[1]
    z = xa_ref[0].astype(F32) * cg_ref[0].astype(F32)
    z_before = (xap_ref[0].astype(F32) * cgp_ref[0].astype(F32))[halo - 1:halo]
    z_after = (xan_ref[0].astype(F32) * cgn_ref[0].astype(F32))[0:1]
    z_before = jnp.where(i == 0, 0.0, z_before)
    z_after = jnp.where(i == pl.num_programs(1) - 1, 0.0, z_after)
    row = lax.broadcasted_iota(jnp.int32, z.shape, 0)
    z_prev = jnp.where(row == 0, z_before, pltpu.roll(z, 1, axis=0))
    z_next = jnp.where(row == tm - 1, z_after, pltpu.roll(z, tm - 1, axis=0))
    w = w_ref[...]
    conv = w[0:1] * z_prev + w[1:2] * z + w[2:3] * z_next
    return bg_ref[0].astype(F32) * conv


def _gated_conv_operands(proj, conv_w, *, width, tm):
    s = proj.shape[1]
    halo = 16
    nh = tm // halo
    last = s // halo - 1
    main = lambda c: pl.BlockSpec((1, tm, width), lambda b, i: (b, i, c))
    prev = lambda c: pl.BlockSpec((1, halo, width), lambda b, i: (b, jnp.maximum(i * nh - 1, 0), c))
    nxt = lambda c: pl.BlockSpec((1, halo, width), lambda b, i: (b, jnp.minimum((i + 1) * nh, last), c))
    w_pad = jnp.zeros((8, width), F32).at[:conv_w.shape[0]].set(conv_w)
    specs = [main(0), main(1), main(2), prev(0), prev(2), nxt(0), nxt(2), pl.BlockSpec((8, width), lambda b, i: (0, 0))]
    return specs, [proj] * 7 + [w_pad]


def _gmlp_gate(u_ref, v_ref, g_ref, b_ref, ws_ref, bs_ref, a_ref):
    tm = u_ref.shape[1]
    v = _layer_norm(v_ref[0].astype(F32), g_ref[...], b_ref[...]).astype(BF16)
    bs = bs_ref[...]
    for n in range(tm // CHUNK):
        rows = slice(n * CHUNK, (n + 1) * CHUNK)
        for g in range(GMLP_GROUPS):
            cols = slice(g * LANES, (g + 1) * LANES)
            sg = jnp.dot(ws_ref[g], v[rows, cols], preferred_element_type=F32) + bs[:, g:g + 1]
            a_ref[rows, cols] = (u_ref[0, rows, cols].astype(F32) * sg).astype(a_ref.dtype)


def _out_ln_kernel(*refs, alpha, mixer):
    n_mixer = 9 if mixer == "conv_attn" else 6
    (w_ref, x_ref, gate_ref, g_ref, b_ref, sh_ref, sc_ref, wr_ref, rb_ref,
     o_ref, xs_ref, cs_ref, tok_ref, seg_ref) = refs[n_mixer:n_mixer + 14]
    if mixer == "conv_attn":
        heads_ref = refs[8]
        a = jnp.concatenate([_gated_conv(*refs[:8]).astype(BF16)]
                            + [heads_ref[0, h] for h in range(heads_ref.shape[1])], axis=1)
    else:
        a_ref = refs[n_mixer + 14]
        _gmlp_gate(*refs[:6], a_ref)
        a = a_ref[...]
    half = a.shape[0] // 2
    for rows in (slice(0, half), slice(half, 2 * half)):
        y = jnp.dot(a[rows], w_ref[...], preferred_element_type=F32)
        r = alpha * x_ref[0, rows, :] + gate_ref[0] * y
        o_ref[0, rows, :] = _layer_norm(r, g_ref[...], b_ref[...])
    _sort_tile(o_ref[0], sh_ref[0], sc_ref[0], wr_ref, rb_ref, xs_ref, cs_ref, tok_ref, seg_ref)


def _out_ln_call(mixer, mixer_specs, mixer_args, w, x, gate, ln_g, ln_b, shift, scale, w_router_t, router_bias,
                 *, alpha, tm):
    bsz, s, d = x.shape
    nt = s // tm
    n_tiles = bsz * nt
    rows = _sort_rows(tm)
    tile = lambda b, i: b * nt + i
    vec = lambda: pl.BlockSpec((1, 1, d), lambda b, i: (b, 0, 0))
    scratch = [pltpu.VMEM((tm, w.shape[0]), BF16)] if mixer == "gmlp" else []
    return pl.pallas_call(
        functools.partial(_out_ln_kernel, alpha=alpha, mixer=mixer),
        grid=(bsz, nt),
        in_specs=mixer_specs + [
            pl.BlockSpec(w.shape, lambda b, i: (0, 0), pipeline_mode=pl.Buffered(1)),
            pl.BlockSpec((1, tm, d), lambda b, i: (b, i, 0)),
            vec(),
            pl.BlockSpec((1, d), lambda b, i: (0, 0)),
            pl.BlockSpec((1, d), lambda b, i: (0, 0)),
            vec(), vec(),
            pl.BlockSpec(w_router_t.shape, lambda b, i: (0, 0)),
            pl.BlockSpec(router_bias.shape, lambda b, i: (0, 0)),
        ],
        out_specs=[
            pl.BlockSpec((1, tm, d), lambda b, i: (b, i, 0)),
            pl.BlockSpec((rows, d), lambda b, i: (tile(b, i), 0)),
            pl.BlockSpec((rows, LANES), lambda b, i: (tile(b, i), 0)),
            pl.BlockSpec((tm, LANES), lambda b, i: (tile(b, i), 0)),
            pl.BlockSpec((1, 2 * N_GROUPS, LANES), lambda b, i: (tile(b, i), 0, 0)),
        ],
        out_shape=[
            jax.ShapeDtypeStruct((bsz, s, d), F32),
            jax.ShapeDtypeStruct(((n_tiles + 1) * rows, d), BF16),
            jax.ShapeDtypeStruct(((n_tiles + 1) * rows, LANES), F32),
            jax.ShapeDtypeStruct((n_tiles * tm, LANES), F32),
            jax.ShapeDtypeStruct((n_tiles, 2 * N_GROUPS, LANES), jnp.int32),
        ],
        scratch_shapes=scratch,
        compiler_params=_cparams(("arbitrary", "arbitrary")),
        name="out_proj_ln_sort",
    )(*mixer_args, w, x, gate, ln_g, ln_b, shift, scale, w_router_t, router_bias)


def _conv_attn_operands(proj, conv_w, attn, *, tm):
    specs, args = _gated_conv_operands(proj, conv_w, width=conv_w.shape[-1], tm=tm)
    specs.append(pl.BlockSpec((1, attn.shape[1], tm, LANES), lambda b, i: (b, 0, i, 0)))
    return specs, args + [attn]


def _gmlp_operands(uv, v_g, v_b, w_s, b_s_t, *, tm):
    d = uv.shape[2] // 2
    specs = [
        pl.BlockSpec((1, tm, d), lambda b, i: (b, i, 0)),
        pl.BlockSpec((1, tm, d), lambda b, i: (b, i, 1)),
        pl.BlockSpec((1, d), lambda b, i: (0, 0)),
        pl.BlockSpec((1, d), lambda b, i: (0, 0)),
        pl.BlockSpec(w_s.shape, lambda b, i: (0, 0, 0)),
        pl.BlockSpec(b_s_t.shape, lambda b, i: (0, 0)),
    ]
    return specs, [uv, uv, v_g, v_b, w_s, b_s_t]


def _route(scores, bias):
    sel = scores + bias
    rows = [sel[e:e + 1, :] for e in range(N_EXPERTS)]
    group_score = []
    for g in range(N_GROUPS):
        r = rows[g * EXPERTS_PER_GROUP:(g + 1) * EXPERTS_PER_GROUP]
        best = None
        for a in range(EXPERTS_PER_GROUP):
            for b in range(a + 1, EXPERTS_PER_GROUP):
                pair = r[a] + r[b]
                best = pair if best is None else jnp.maximum(best, pair)
        group_score.append(best)
    one = jnp.ones_like(rows[0])
    zero = jnp.zeros_like(rows[0])
    picked = []
    chosen = []
    for g in range(N_GROUPS):
        beaten = zero
        for o in range(N_GROUPS):
            if o < g:
                beaten = jnp.where(group_score[o] >= group_score[g], one, beaten)
            elif o > g:
                beaten = jnp.where(group_score[o] > group_score[g], one, beaten)
        chosen.append(1.0 - beaten)
        for a in range(EXPERTS_PER_GROUP):
            e = g * EXPERTS_PER_GROUP + a
            rank = zero
            for b in range(EXPERTS_PER_GROUP):
                o = g * EXPERTS_PER_GROUP + b
                if b < a:
                    rank = rank + jnp.where(rows[o] >= rows[e], one, zero)
                elif b > a:
                    rank = rank + jnp.where(rows[o] > rows[e], one, zero)
            picked.append(jnp.where(rank < 2.0, one, zero) * chosen[g])
    w = [picked[e] * scores[e:e + 1, :] for e in range(N_EXPERTS)]
    total = w[0]
    for e in range(1, N_EXPERTS):
        total = total + w[e]
    return [we / total for we in w], chosen


ROW_BLOCK = 32
STEP_BLOCKS = 8
DEST_LANE = EXPERTS_PER_GROUP
SORT_TM = 512


def _sort_rows(tm):
    return tm + N_GROUPS * ROW_BLOCK


def _sort_tile(x, sh, sc, wr_ref, rb_ref, xs_ref, cs_ref, tok_ref, seg_ref):
    tm, d = x.shape
    rows = xs_ref.shape[0]
    col_chunk = 512
    h = (x * (1.0 + sc) + sh).astype(BF16)
    logits = lax.dot_general(wr_ref[...], h, (((1,), (1,)), ((), ())), preferred_element_type=F32)
    comb, chosen = _route(jax.nn.sigmoid(logits), rb_ref[...])
    src = lax.broadcasted_iota(jnp.int32, (tm, tm), 0)
    dst = lax.broadcasted_iota(jnp.int32, (tm, tm), 1)
    before = jnp.where(src < dst, 1.0, 0.0).astype(BF16)
    pad_rows = [jnp.zeros_like(chosen[0])] * (8 - N_GROUPS)
    rank = jnp.dot(jnp.concatenate(chosen + pad_rows, axis=0).astype(BF16), before, preferred_element_type=F32)
    dest = jnp.zeros_like(chosen[0])
    first = jnp.zeros((1, 1), F32)
    firsts, counts = [], []
    for g in range(N_GROUPS):
        count = jnp.sum(chosen[g], axis=1, keepdims=True)
        n_blocks = jnp.floor((count + (ROW_BLOCK - 1)) * (1.0 / ROW_BLOCK))
        firsts.append(first)
        counts.append(n_blocks)
        dest = dest + chosen[g] * (first * ROW_BLOCK + rank[g:g + 1])
        first = first + n_blocks
    seg = jnp.concatenate([jnp.broadcast_to(v, (1, LANES)) for v in firsts + counts], axis=0)
    seg_ref[0] = seg.astype(jnp.int32)
    dest_i = dest.astype(jnp.int32)
    row_id = lax.broadcasted_iota(jnp.int32, (rows, tm), 0)
    p = jnp.where(row_id == dest_i, 1.0, 0.0).astype(BF16)
    for c in range(d // col_chunk):
        cols = slice(c * col_chunk, (c + 1) * col_chunk)
        xs_ref[:, cols] = jnp.dot(p, h[:, cols], preferred_element_type=F32).astype(BF16)
    in_group = []
    for j in range(EXPERTS_PER_GROUP):
        cj = comb[j]
        for g in range(1, N_GROUPS):
            cj = cj + comb[g * EXPERTS_PER_GROUP + j]
        in_group.append(cj)
    c_rows = jnp.concatenate(in_group + [dest, jnp.zeros((LANES - DEST_LANE - 1, tm), F32)], axis=0)
    c_cols = jnp.transpose(c_rows)
    tok_ref[...] = c_cols
    hi = c_cols.astype(BF16)
    lo = (c_cols - hi.astype(F32)).astype(BF16)
    cs = jnp.dot(p, jnp.concatenate([hi, lo], axis=1), preferred_element_type=F32)
    cs_ref[...] = cs[:, :LANES] + cs[:, LANES:]


def _expert_work_table(seg, rows):
    n_tiles = seg.shape[0]
    per_tile = rows // ROW_BLOCK
    spare = n_tiles * per_tile
    max_items = spare // STEP_BLOCKS + N_GROUPS
    first = seg[:, :N_GROUPS, 0]
    count = seg[:, N_GROUPS:, 0]
    tri = jnp.arange(n_tiles)[:, None] >= jnp.arange(n_tiles)[None, :]
    cum = jnp.sum(jnp.where(tri[:, :, None], count[None, :, :], 0), axis=1)
    total = cum[-1]
    items = (total + (STEP_BLOCKS - 1)) // STEP_BLOCKS
    ends = jnp.sum(jnp.where(jnp.arange(N_GROUPS)[:, None] >= jnp.arange(N_GROUPS)[None, :], items[None, :], 0),
                   axis=1)
    n_items = ends[-1]
    step = jnp.arange(max_items + 1)
    live = step < n_items
    ref_step = jnp.where(live, step, jnp.maximum(n_items - 1, 0))
    grp = jnp.sum((ref_step[:, None] >= ends[None, :-1]).astype(jnp.int32), axis=1)
    pick = grp[:, None] == jnp.arange(N_GROUPS)[None, :]
    of_group = lambda v: jnp.sum(jnp.where(pick, v[None, :], 0), axis=1)
    start = of_group(ends - items)
    cum_s = jnp.sum(jnp.where(pick[:, None, :], cum[None, :, :], 0), axis=2)
    count_s = jnp.sum(jnp.where(pick[:, None, :], count[None, :, :], 0), axis=2)
    first_s = jnp.sum(jnp.where(pick[:, None, :], first[None, :, :], 0), axis=2)
    total_s = of_group(total)

    def block_id(k):
        done = cum_s <= k[:, None]
        t = jnp.sum(done.astype(jnp.int32), axis=1)
        skipped = jnp.sum(jnp.where(done, count_s, 0), axis=1)
        at_t = jnp.arange(n_tiles)[None, :] == t[:, None]
        return t * per_tile + jnp.sum(jnp.where(at_t, first_s, 0), axis=1) + (k - skipped)

    as_i32 = lambda v: v.astype(jnp.int32)
    block = jnp.arange(spare)[:, None]
    blocks = []
    where = jnp.zeros((spare,), jnp.int32)
    found = jnp.zeros((spare,), jnp.bool_)
    for u in range(STEP_BLOCKS):
        k_u = STEP_BLOCKS * (step - start) + u
        has_u = live & (k_u < total_s)
        blk_u = jnp.where(has_u, block_id(k_u), blocks[0] if u else spare)
        blocks.append(as_i32(blk_u))
        hit = has_u[None, :] & (blk_u[None, :] == block)
        where = where + jnp.sum(jnp.where(hit, STEP_BLOCKS * step[None, :] + u, 0), axis=1)
        found = found | jnp.any(hit, axis=1)
    where = jnp.where(found, where, STEP_BLOCKS * max_items)
    return (*blocks, as_i32(grp), as_i32(n_items).reshape(1)), as_i32(where)


def _moe_expert_kernel(*refs):
    n_items = refs[STEP_BLOCKS + 1]
    pos = STEP_BLOCKS + 2
    x_refs = refs[pos:pos + STEP_BLOCKS]
    c_refs = refs[pos + STEP_BLOCKS:pos + 2 * STEP_BLOCKS]
    wg_ref, wu_ref, wd_ref, y_ref = refs[pos + 2 * STEP_BLOCKS:]
    f, d = wd_ref.shape[2], wd_ref.shape[3]

    @pl.when(pl.program_id(0) < n_items[0])
    def _():
        x = jnp.concatenate([r[...] for r in x_refs], axis=0)
        c = jnp.concatenate([r[...] for r in c_refs], axis=0)
        pieces = []
        for j in range(EXPERTS_PER_GROUP):
            gate = jnp.dot(x, wg_ref[0, j].astype(BF16), preferred_element_type=F32)
            up = jnp.dot(x, wu_ref[0, j].astype(BF16), preferred_element_type=F32)
            pieces.append((jax.nn.silu(gate) * up * c[:, j:j + 1]).astype(BF16))
        w_down = wd_ref[0].reshape(EXPERTS_PER_GROUP * f, d).astype(BF16)
        y = jnp.dot(jnp.concatenate(pieces, axis=1), w_down, preferred_element_type=F32)
        y_ref[...] = y.astype(BF16)

    @pl.when(pl.program_id(0) >= n_items[0])
    def _():
        y_ref[...] = jnp.zeros(y_ref.shape, BF16)


def _moe_expert_call(table, xs, cs, w_gate, w_up, w_down, layer):
    d = xs.shape[1]
    steps = table[0].shape[0]
    n_pre = len(table)
    grp_at = STEP_BLOCKS
    rows_of = lambda u, width: pl.BlockSpec((ROW_BLOCK, width), lambda i, *pre: (pre[u][i], 0))
    weights = lambda w: pl.BlockSpec((1,) + w.shape[1:],
                                     lambda i, *pre: (layer * N_GROUPS + pre[grp_at][i], 0, 0, 0),
                                     pipeline_mode=pl.Buffered(1))
    grid_spec = pltpu.PrefetchScalarGridSpec(
        num_scalar_prefetch=n_pre,
        grid=(steps,),
        in_specs=[rows_of(u, d) for u in range(STEP_BLOCKS)] + [rows_of(u, LANES) for u in range(STEP_BLOCKS)]
        + [weights(w_gate), weights(w_up), weights(w_down)],
        out_specs=pl.BlockSpec((STEP_BLOCKS * ROW_BLOCK, d), lambda i, *pre: (i, 0)),
    )
    return pl.pallas_call(
        _moe_expert_kernel,
        grid_spec=grid_spec,
        out_shape=jax.ShapeDtypeStruct((steps * STEP_BLOCKS * ROW_BLOCK, d), BF16),
        compiler_params=_cparams(("arbitrary",)),
        name="moe_experts",
    )(*table, *([xs] * STEP_BLOCKS), *([cs] * STEP_BLOCKS), w_gate, w_up, w_down)


def _moe_combine_kernel(where_ref, tok_ref, *refs, alpha, n_blocks):
    y_refs = refs[:n_blocks]
    x_ref, gate_ref, g_ref, b_ref, o_ref = refs[n_blocks:]
    tm = x_ref.shape[1]
    ys = jnp.concatenate([r[...] for r in y_refs], axis=0)
    half = tm // 2
    row_id = lax.broadcasted_iota(jnp.int32, (half, ys.shape[0]), 1)
    for rows in (slice(0, half), slice(half, tm)):
        dest = tok_ref[rows, DEST_LANE:DEST_LANE + 1].astype(jnp.int32)
        p_t = jnp.where(row_id == dest, 1.0, 0.0).astype(BF16)
        y = jnp.dot(p_t, ys, preferred_element_type=F32)
        r = alpha * x_ref[0, rows, :] + gate_ref[0] * y
        o_ref[0, rows, :] = _layer_norm(r, g_ref[...], b_ref[...])


def _moe_combine_call(where, tok, ys, x, gate, ln_g, ln_b, *, alpha, sort_tm, tm):
    bsz, s, d = x.shape
    nt = s // sort_tm
    parts = sort_tm // tm
    n_blocks = _sort_rows(sort_tm) // ROW_BLOCK
    tile = lambda b, i: b * nt + i
    y_spec = lambda j: pl.BlockSpec((ROW_BLOCK, d), lambda b, i, k, w: (w[tile(b, i) * n_blocks + j], 0))
    grid_spec = pltpu.PrefetchScalarGridSpec(
        num_scalar_prefetch=1,
        grid=(bsz, nt, parts),
        in_specs=[pl.BlockSpec((tm, LANES), lambda b, i, k, w: (tile(b, i) * parts + k, 0))]
        + [y_spec(j) for j in range(n_blocks)]
        + [
            pl.BlockSpec((1, tm, d), lambda b, i, k, w: (b, i * parts + k, 0)),
            pl.BlockSpec((1, 1, d), lambda b, i, k, w: (b, 0, 0)),
            pl.BlockSpec((1, d), lambda b, i, k, w: (0, 0)),
            pl.BlockSpec((1, d), lambda b, i, k, w: (0, 0)),
        ],
        out_specs=pl.BlockSpec((1, tm, d), lambda b, i, k, w: (b, i * parts + k, 0)),
    )
    return pl.pallas_call(
        functools.partial(_moe_combine_kernel, alpha=alpha, n_blocks=n_blocks),
        grid_spec=grid_spec,
        out_shape=jax.ShapeDtypeStruct((bsz, s, d), F32),
        compiler_params=_cparams(("arbitrary", "arbitrary", "arbitrary")),
        name="moe_combine",
    )(where, tok, *([ys] * n_blocks), x, gate, ln_g, ln_b)


def _moe(x, sorted_tokens, gate, w_gate, w_up, w_down, layer, ln_g, ln_b, *, alpha):
    sort_tm = min(SORT_TM, x.shape[1])
    xs, cs, tok, seg = sorted_tokens
    table, where = _expert_work_table(seg, _sort_rows(sort_tm))
    by_group = lambda w: w.reshape((w.shape[0] * N_GROUPS, EXPERTS_PER_GROUP) + w.shape[2:])
    ys = _moe_expert_call(table, xs, cs, by_group(w_gate), by_group(w_up), by_group(w_down), layer)
    return _moe_combine_call(where, tok, ys, x, gate, ln_g, ln_b, alpha=alpha, sort_tm=sort_tm,
                             tm=min(ROW_TM, sort_tm))


def _rope_tables(n):
    freqs = LANES // 8
    inv = jnp.power(ROPE_BASE, -jnp.arange(freqs, dtype=F32) / freqs)
    tok = jnp.arange(n)
    ang_r = (tok // GRID_W).astype(F32)[:, None] * inv
    ang_c = (tok % GRID_W).astype(F32)[:, None] * inv
    ang = jnp.concatenate([ang_r, ang_c], axis=1)
    cos = jnp.tile(jnp.cos(ang), (1, 4))
    sin = jnp.tile(jnp.sin(ang), (1, 4))
    sign = jnp.where(jnp.arange(LANES) < LANES // 2, -1.0, 1.0).astype(F32)
    return cos, sin * sign


def _rope_column_perm():
    perm = np.zeros((DA_HEADS, 2, 2, 2, 16), np.int32)
    for h in range(DA_HEADS):
        for p in range(2):
            for m in range(2):
                for ax in range(2):
                    for f in range(16):
                        perm[h, p, m, ax, f] = h * LANES + m * 64 + ax * 32 + p * 16 + f
    return perm.reshape(-1)


def kernel(x, c, ctx, c_ctx, w_mod, b_mod, ln_g, ln_b, w_in_e, conv_w, lambda_q1, lambda_k1, lambda_q2, lambda_k2,
           subln_g, w_out_e, w_in_o, v_ln_g, v_ln_b, w_spatial, b_spatial, w_out_o, w_router, router_bias,
           w_gate, w_up, w_down):
    bsz, s, d = x.shape
    depth = w_mod.shape[0]
    assert depth == 2 and w_in_e.shape[0] == 1 and w_in_o.shape[0] == 1, "two-layer (even, odd) stack only"
    conv_dim = conv_w.shape[-1]
    q_dim = DA_HEADS * LANES
    q_col, k_col, v_col = 3 * conv_dim, 3 * conv_dim + q_dim, 3 * conv_dim + 2 * q_dim
    assert w_in_e.shape[2] == v_col + q_dim and s % GRID_W == 0
    alpha = float((2 * depth) ** 0.25)

    rows = 8 * ((bsz + 1 + 7) // 8)
    cond = jnp.zeros((rows, d), F32).at[:bsz].set(c).at[bsz].set(c_ctx)
    mods = _mod_call(cond, w_mod, b_mod)

    def mod_vec(l, k, ctx_row=False):
        v = mods[l, bsz:bsz + 1, k * d:(k + 1) * d] if ctx_row else mods[l, :bsz, k * d:(k + 1) * d]
        return v.reshape(-1, 1, d)

    w_router_t = w_router.T.astype(BF16)
    rbias = router_bias.reshape(-1, 1).astype(F32)

    perm = _rope_column_perm()
    w_in = w_in_e[0]
    w_in = jnp.concatenate(
        [w_in[:, :q_col], w_in[:, q_col:k_col][:, perm], w_in[:, k_col:v_col][:, perm], w_in[:, v_col:]],
        axis=1).astype(BF16)
    cos_t, sin_t = _rope_tables(s)
    tn = q_dim
    assert q_col % tn == 0
    q_tile, k_tile = q_col // tn, k_col // tn
    conv_proj, q, k, v = _proj_call(
        x, mod_vec(0, 0), mod_vec(0, 1), w_in, tm=PROJ_TM, tn=tn, n_flat=q_tile,
        rope=(cos_t, sin_t, (q_tile, k_tile), q_tile, float(64 ** -0.5 * math.log2(math.e))))
    kc, vc = _proj_call(ctx, mod_vec(0, 0, True), mod_vec(0, 1, True), w_in[:, k_col:], tm=PROJ_TM, tn=tn, n_flat=0)

    lam_init = 0.8 - 0.6 * math.exp(-0.3 * 0)
    lam_params = jnp.zeros((8, LANES), F32)
    for r, p in enumerate((lambda_q1, lambda_k1, lambda_q2, lambda_k2)):
        lam_params = lam_params.at[r, :p.shape[-1]].set(p[0].astype(F32))
    attn = _attn_call(q, k, v, kc, vc, lam_params, subln_g[0].reshape(1, LANES), lam_init=lam_init, tq=ATTN_TQ)
    assert ROW_TM == SORT_TM
    x, *sorted_tokens = _out_ln_call(
        "conv_attn", *_conv_attn_operands(conv_proj, conv_w[0], attn, tm=ROW_TM), w_out_e[0].astype(BF16),
        x, mod_vec(0, 2), ln_g[0, 0:1], ln_b[0, 0:1], mod_vec(0, 3), mod_vec(0, 4), w_router_t, rbias,
        alpha=alpha, tm=min(ROW_TM, s))
    x = _moe(x, sorted_tokens, mod_vec(0, 5), w_gate, w_up, w_down, 0, ln_g[0, 1:2], ln_b[0, 1:2], alpha=alpha)

    w_gmlp = w_in_o[0].astype(BF16)
    uv, = _proj_call(x, mod_vec(1, 0), mod_vec(1, 1), w_gmlp, tm=PROJ_TM, tn=tn, n_flat=w_gmlp.shape[1] // tn,
                     gelu=True)
    gmlp = _gmlp_operands(uv, v_ln_g[0:1], v_ln_b[0:1], w_spatial[0].astype(BF16), b_spatial[0].T, tm=ROW_TM)
    x, *sorted_tokens = _out_ln_call(
        "gmlp", *gmlp, w_out_o[0].astype(BF16), x, mod_vec(1, 2), ln_g[1, 0:1], ln_b[1, 0:1],
        mod_vec(1, 3), mod_vec(1, 4), w_router_t, rbias, alpha=alpha, tm=min(ROW_TM, s))
    x = _moe(x, sorted_tokens, mod_vec(1, 5), w_gate, w_up, w_down, 1, ln_g[1, 1:2], ln_b[1, 1:2], alpha=alpha)
    return x
```

```python
import functools
import math

import numpy as np
import jax
import jax.numpy as jnp
from jax import lax
from jax.experimental import pallas as pl
from jax.experimental.pallas import tpu as pltpu

F32 = jnp.float32
BF16 = jnp.bfloat16

GRID_W = 64
DA_HEADS = 8
N_EXPERTS = 16
N_GROUPS = 4
EXPERTS_PER_GROUP = N_EXPERTS // N_GROUPS
ROPE_BASE = 10000.0
LN_EPS = 1e-5
CHUNK = 128
GMLP_GROUPS = 16
LANES = 128
MXU_DEPTH = 256

VMEM_LIMIT = 56 * 1024 * 1024
PROJ_TM = 1024
ROW_TM = 512
ATTN_TQ = 1024


def _cparams(sem):
    return pltpu.CompilerParams(dimension_semantics=sem, vmem_limit_bytes=VMEM_LIMIT)


def _layer_norm(r, g, b):
    mu = jnp.mean(r, axis=-1, keepdims=True)
    d = r - mu
    var = jnp.mean(d * d, axis=-1, keepdims=True)
    return d * lax.rsqrt(var + LN_EPS) * g + b


def _mod_kernel(c_ref, w_ref, b_ref, o_ref):
    c = c_ref[...]
    s = (c * jax.nn.sigmoid(c)).astype(BF16)
    o_ref[0] = jnp.dot(s, w_ref[0].astype(BF16), preferred_element_type=F32) + b_ref[0]


def _mod_call(cond, w_mod, b_mod):
    depth, d, n = w_mod.shape
    rows = cond.shape[0]
    tn = 1024
    return pl.pallas_call(
        _mod_kernel,
        grid=(depth, n // tn),
        in_specs=[
            pl.BlockSpec((rows, d), lambda l, j: (0, 0)),
            pl.BlockSpec((1, d, tn), lambda l, j: (l, 0, j)),
            pl.BlockSpec((1, 1, tn), lambda l, j: (l, 0, j)),
        ],
        out_specs=pl.BlockSpec((1, rows, tn), lambda l, j: (l, 0, j)),
        out_shape=jax.ShapeDtypeStruct((depth, rows, n), F32),
        compiler_params=_cparams(("arbitrary", "arbitrary")),
        name="adaln_mod",
    )(cond, w_mod, b_mod.reshape(depth, 1, n))


def _proj_kernel(*refs, n_flat, n_head_tiles, rope, gelu):
    x_ref, sh_ref, sc_ref, w_ref = refs[:4]
    pos = 4
    if rope is not None:
        cos_ref, sin_ref = refs[pos:pos + 2]
        pos += 2
    flat_ref = refs[pos] if n_flat else None
    pos += 1 if n_flat else 0
    head_refs = refs[pos:pos + n_head_tiles]
    h_ref = refs[pos + n_head_tiles]
    j = pl.program_id(2)

    tn = w_ref.shape[1]
    halves = [slice(c * (tn // 2), (c + 1) * (tn // 2)) for c in range(2)]
    activate = (lambda acc: jax.nn.gelu(acc, approximate=True)) if gelu else (lambda acc: acc)

    def modulated(rows):
        return (x_ref[0, rows, :] * (1.0 + sc_ref[0]) + sh_ref[0]).astype(BF16)

    def product(cols):
        return jnp.dot(h_ref[...], w_ref[:, cols], preferred_element_type=F32)

    if n_flat:
        @pl.when(j == 0)
        def _():
            tm = h_ref.shape[0]
            for rows in (slice(0, tm // 2), slice(tm // 2, tm)):
                h = modulated(rows)
                h_ref[rows, :] = h
                for cols in halves:
                    acc = jnp.dot(h, w_ref[:, cols], preferred_element_type=F32)
                    flat_ref[0, rows, cols] = activate(acc).astype(flat_ref.dtype)

        @pl.when(jnp.logical_and(j > 0, j < n_flat))
        def _():
            for cols in halves:
                flat_ref[0, :, cols] = activate(product(cols)).astype(flat_ref.dtype)
    else:
        @pl.when(j == 0)
        def _():
            h_ref[...] = modulated(slice(None))

    for t in range(n_head_tiles):
        tile = n_flat + t
        o_ref = head_refs[t]

        @pl.when(j == tile)
        def _(tile=tile, o_ref=o_ref):
            rotary = rope is not None and tile in rope[0]
            if rotary:
                scale = rope[2] if tile == rope[1] else 1.0
                cs = cos_ref[...] * scale
                sn = sin_ref[...] * scale
            for cols in halves:
                acc = product(cols)
                for h in range(acc.shape[1] // LANES):
                    piece = acc[:, h * LANES:(h + 1) * LANES]
                    if rotary:
                        piece = piece * cs + pltpu.roll(piece, LANES // 2, axis=1) * sn
                    o_ref[0, cols.start // LANES + h] = piece.astype(o_ref.dtype)


def _proj_call(x, shift, scale, w, *, tm, tn, n_flat, rope=None, gelu=False):
    bsz, s, d = x.shape
    n_tiles = w.shape[1] // tn
    n_head_tiles = n_tiles - n_flat
    heads = tn // LANES
    tm = min(tm, s)
    per_batch = shift.shape[0] > 1
    mod_map = (lambda b, i, j: (b, 0, 0)) if per_batch else (lambda b, i, j: (0, 0, 0))
    in_specs = [
        pl.BlockSpec((1, tm, d), lambda b, i, j: (b, i, 0)),
        pl.BlockSpec((1, 1, d), mod_map),
        pl.BlockSpec((1, 1, d), mod_map),
        pl.BlockSpec((d, tn), lambda b, i, j: (0, j)),
    ]
    args = [x, shift, scale, w]
    rope_static = None
    if rope is not None:
        cos_t, sin_t, rope_tiles, q_tile, q_scale = rope
        in_specs += [pl.BlockSpec((tm, LANES), lambda b, i, j: (i, 0))] * 2
        args += [cos_t, sin_t]
        rope_static = (tuple(rope_tiles), q_tile, q_scale)
    out_specs, out_shape = [], []
    if n_flat:
        out_specs.append(pl.BlockSpec((1, tm, tn), lambda b, i, j: (b, i, jnp.minimum(j, n_flat - 1))))
        out_shape.append(jax.ShapeDtypeStruct((bsz, s, n_flat * tn), BF16))
    for _ in range(n_head_tiles):
        out_specs.append(pl.BlockSpec((1, heads, tm, LANES), lambda b, i, j: (b, 0, i, 0)))
        out_shape.append(jax.ShapeDtypeStruct((bsz, heads, s, LANES), BF16))
    return pl.pallas_call(
        functools.partial(_proj_kernel, n_flat=n_flat, n_head_tiles=n_head_tiles, rope=rope_static, gelu=gelu),
        grid=(bsz, s // tm, n_tiles),
        in_specs=in_specs,
        out_specs=out_specs,
        out_shape=out_shape,
        scratch_shapes=[pltpu.VMEM((tm, d), BF16)],
        compiler_params=_cparams(("arbitrary", "arbitrary", "arbitrary")),
        name="mod_proj",
    )(*args)


ONES_ROWS = 16


def _attn_kernel(lam_ref, g_ref, q_ref, kc_ref, vc_ref, k_ref, v_ref, o_ref, kf_ref, vt_ref, s_ref, p_ref, acc_ref,
                 *, tk, lam_init):
    tq = q_ref.shape[2]
    c_len = kc_ref.shape[2]
    s_len = k_ref.shape[2]
    n_chunks = (c_len + s_len) // tk
    vt_chunk = 512

    @pl.when(pl.program_id(2) == 0)
    def _():
        kf_ref[:c_len, :] = kc_ref[0, 0]
        kf_ref[c_len:, :] = k_ref[0, 0]
        vt_ref[LANES:, :] = jnp.ones((ONES_ROWS, vt_ref.shape[1]), BF16)
        vt_ref[:LANES, :c_len] = jnp.transpose(vc_ref[0, 0].astype(F32)).astype(BF16)
        for n in range(s_len // vt_chunk):
            rows = slice(n * vt_chunk, (n + 1) * vt_chunk)
            cols = slice(c_len + n * vt_chunk, c_len + (n + 1) * vt_chunk)
            vt_ref[:LANES, cols] = jnp.transpose(v_ref[0, 0, rows, :].astype(F32)).astype(BF16)

    q_t = jnp.transpose(q_ref[0, 0].astype(F32))
    dim = lax.broadcasted_iota(jnp.int32, q_t.shape, 0)
    first_map = (dim % (LANES // 2)) < (LANES // 4)
    qs_t = jnp.concatenate([jnp.where(first_map, q_t, 0.0), jnp.where(first_map, 0.0, q_t)], axis=1).astype(BF16)

    acc_ref[...] = jnp.zeros(acc_ref.shape, F32)

    def scores(t):
        off = t * tk if isinstance(t, int) else pl.multiple_of(t * tk, tk)
        return jnp.dot(kf_ref[pl.ds(off, tk), :], qs_t, preferred_element_type=F32)

    def softmax(slot, m_old):
        s_t = s_ref[slot]
        m_new = jnp.maximum(m_old, jnp.max(s_t, axis=0, keepdims=True))
        p_ref[slot] = jnp.exp2((s_t - m_new).astype(BF16))
        return m_new, jnp.exp2(m_old - m_new)

    def values(t, slot, alpha):
        off = t * tk if isinstance(t, int) else pl.multiple_of(t * tk, tk)
        pv = jnp.dot(vt_ref[:, pl.ds(off, tk)], p_ref[slot], preferred_element_type=F32)
        acc_ref[...] = alpha * acc_ref[...] + pv

    def tick(t, parity, m, alpha_prev):
        s_ref[parity] = scores(t)
        m, alpha = softmax(1 - parity, m)
        values(t - 2, parity, alpha_prev)
        return m, alpha

    s_ref[0] = scores(0)
    s_ref[1] = scores(1)
    m, alpha = softmax(0, jnp.full((1, 2 * tq), -jnp.inf, F32))

    def pair(jj, carry):
        m, alpha = tick(2 + 2 * jj, 0, *carry)
        return tick(3 + 2 * jj, 1, m, alpha)

    n_full = n_chunks - 2
    m, alpha = lax.fori_loop(0, n_full // 2, pair, (m, alpha))
    if n_full % 2:
        m, alpha = tick(n_chunks - 1, (n_chunks - 1) % 2, m, alpha)
    last = (n_chunks - 1) % 2
    m, alpha_last = softmax(last, m)
    values(n_chunks - 2, 1 - last, alpha)
    values(n_chunks - 1, last, alpha_last)

    lp = lam_ref[...]
    lam = (jnp.exp(jnp.sum(lp[0:1] * lp[1:2], axis=-1, keepdims=True))
           - jnp.exp(jnp.sum(lp[2:3] * lp[3:4], axis=-1, keepdims=True)) + lam_init)
    acc = acc_ref[...]
    o_t = acc[:LANES] / acc[LANES:LANES + 1]
    o_t = o_t[:, :tq] - lam * o_t[:, tq:]
    o_t = o_t * lax.rsqrt(jnp.mean(o_t * o_t, axis=0, keepdims=True) + LN_EPS)
    o_ref[0, 0] = (jnp.transpose(o_t) * g_ref[...] * (1.0 - lam_init)).astype(o_ref.dtype)


def _attn_call(q, k, v, kc, vc, lam_params, subln_g, *, lam_init, tq):
    bsz, h, s, _ = q.shape
    c_len = kc.shape[2]
    tq = min(tq, s)
    tk = max(t for t in (MXU_DEPTH, 2 * MXU_DEPTH, 3 * MXU_DEPTH) if (c_len + s) % t == 0)
    assert (c_len + s) // tk >= 3 and s % 512 == 0
    whole = lambda n: pl.BlockSpec((1, 1, n, LANES), lambda b, hh, i: (b, hh, 0, 0))
    return pl.pallas_call(
        functools.partial(_attn_kernel, tk=tk, lam_init=lam_init),
        grid=(bsz, h, s // tq),
        in_specs=[
            pl.BlockSpec((8, LANES), lambda b, hh, i: (0, 0)),
            pl.BlockSpec((1, LANES), lambda b, hh, i: (0, 0)),
            pl.BlockSpec((1, 1, tq, LANES), lambda b, hh, i: (b, hh, i, 0)),
            whole(c_len), whole(c_len), whole(s), whole(s),
        ],
        out_specs=pl.BlockSpec((1, 1, tq, LANES), lambda b, hh, i: (b, hh, i, 0)),
        out_shape=jax.ShapeDtypeStruct((bsz, h, s, LANES), BF16),
        scratch_shapes=[
            pltpu.VMEM((c_len + s, LANES), BF16),
            pltpu.VMEM((LANES + ONES_ROWS, c_len + s), BF16),
            pltpu.VMEM((2, tk, 2 * tq), F32),
            pltpu.VMEM((2, tk, 2 * tq), BF16),
            pltpu.VMEM((LANES + ONES_ROWS, 2 * tq), F32),
        ],
        compiler_params=_cparams(("arbitrary", "arbitrary", "arbitrary")),
        name="diff_attn",
    )(lam_params, subln_g, q, kc, vc, k, v)


def _gated_conv(xa_ref, bg_ref, cg_ref, xap_ref, cgp_ref, xan_ref, cgn_ref, w_ref):
    i = pl.program_id(1)
    tm = xa_ref.shape[1]
    halo = xap_ref.shape[1]
    z = xa_ref[0].astype(F32) * cg_ref[0].astype(F32)
    z_before = (xap_ref[0].astype(F32) * cgp_ref[0].astype(F32))[halo - 1:halo]
    z_after = (xan_ref[0].astype(F32) * cgn_ref[0].astype(F32))[0:1]
    z_before = jnp.where(i == 0, 0.0, z_before)
    z_after = jnp.where(i == pl.num_programs(1) - 1, 0.0, z_after)
    row = lax.broadcasted_iota(jnp.int32, z.shape, 0)
    z_prev = jnp.where(row == 0, z_before, pltpu.roll(z, 1, axis=0))
    z_next = jnp.where(row == tm - 1, z_after, pltpu.roll(z, tm - 1, axis=0))
    w = w_ref[...]
    conv = w[0:1] * z_prev + w[1:2] * z + w[2:3] * z_next
    return bg_ref[0].astype(F32) * conv


def _gated_conv_operands(proj, conv_w, *, width, tm):
    s = proj.shape[1]
    halo = 16
    nh = tm // halo
    last = s // halo - 1
    main = lambda c: pl.BlockSpec((1, tm, width), lambda b, i: (b, i, c))
    prev = lambda c: pl.BlockSpec((1, halo, width), lambda b, i: (b, jnp.maximum(i * nh - 1, 0), c))
    nxt = lambda c: pl.BlockSpec((1, halo, width), lambda b, i: (b, jnp.minimum((i + 1) * nh, last), c))
    w_pad = jnp.zeros((8, width), F32).at[:conv_w.shape[0]].set(conv_w)
    specs = [main(0), main(1), main(2), prev(0), prev(2), nxt(0), nxt(2), pl.BlockSpec((8, width), lambda b, i: (0, 0))]
    return specs, [proj] * 7 + [w_pad]


def _gmlp_gate(u_ref, v_ref, g_ref, b_ref, ws_ref, bs_ref, a_ref):
    tm = u_ref.shape[1]
    v = _layer_norm(v_ref[0].astype(F32), g_ref[...], b_ref[...]).astype(BF16)
    bs = bs_ref[...]
    for n in range(tm // CHUNK):
        rows = slice(n * CHUNK, (n + 1) * CHUNK)
        for g in range(GMLP_GROUPS):
            cols = slice(g * LANES, (g + 1) * LANES)
            sg = jnp.dot(ws_ref[g], v[rows, cols], preferred_element_type=F32) + bs[:, g:g + 1]
            a_ref[rows, cols] = (u_ref[0, rows, cols].astype(F32) * sg).astype(a_ref.dtype)


def _out_ln_kernel(*refs, alpha, mixer):
    n_mixer = 9 if mixer == "conv_attn" else 6
    (w_ref, x_ref, gate_ref, g_ref, b_ref, sh_ref, sc_ref, wr_ref, rb_ref,
     o_ref, xs_ref, cs_ref, tok_ref, seg_ref) = refs[n_mixer:n_mixer + 14]
    if mixer == "conv_attn":
        heads_ref = refs[8]
        a = jnp.concatenate([_gated_conv(*refs[:8]).astype(BF16)]
                            + [heads_ref[0, h] for h in range(heads_ref.shape[1])], axis=1)
    else:
        a_ref = refs[n_mixer + 14]
        _gmlp_gate(*refs[:6], a_ref)
        a = a_ref[...]
    half = a.shape[0] // 2
    for rows in (slice(0, half), slice(half, 2 * half)):
        y = jnp.dot(a[rows], w_ref[...], preferred_element_type=F32)
        r = alpha * x_ref[0, rows, :] + gate_ref[0] * y
        o_ref[0, rows, :] = _layer_norm(r, g_ref[...], b_ref[...])
    _sort_tile(o_ref[0], sh_ref[0], sc_ref[0], wr_ref, rb_ref, xs_ref, cs_ref, tok_ref, seg_ref)


def _out_ln_call(mixer, mixer_specs, mixer_args, w, x, gate, ln_g, ln_b, shift, scale, w_router_t, router_bias,
                 *, alpha, tm):
    bsz, s, d = x.shape
    nt = s // tm
    n_tiles = bsz * nt
    rows = _sort_rows(tm)
    tile = lambda b, i: b * nt + i
    vec = lambda: pl.BlockSpec((1, 1, d), lambda b, i: (b, 0, 0))
    scratch = [pltpu.VMEM((tm, w.shape[0]), BF16)] if mixer == "gmlp" else []
    return pl.pallas_call(
        functools.partial(_out_ln_kernel, alpha=alpha, mixer=mixer),
        grid=(bsz, nt),
        in_specs=mixer_specs + [
            pl.BlockSpec(w.shape, lambda b, i: (0, 0), pipeline_mode=pl.Buffered(1)),
            pl.BlockSpec((1, tm, d), lambda b, i: (b, i, 0)),
            vec(),
            pl.BlockSpec((1, d), lambda b, i: (0, 0)),
            pl.BlockSpec((1, d), lambda b, i: (0, 0)),
            vec(), vec(),
            pl.BlockSpec(w_router_t.shape, lambda b, i: (0, 0)),
            pl.BlockSpec(router_bias.shape, lambda b, i: (0, 0)),
        ],
        out_specs=[
            pl.BlockSpec((1, tm, d), lambda b, i: (b, i, 0)),
            pl.BlockSpec((rows, d), lambda b, i: (tile(b, i), 0)),
            pl.BlockSpec((rows, LANES), lambda b, i: (tile(b, i), 0)),
            pl.BlockSpec((tm, LANES), lambda b, i: (tile(b, i), 0)),
            pl.BlockSpec((1, 2 * N_GROUPS, LANES), lambda b, i: (tile(b, i), 0, 0)),
        ],
        out_shape=[
            jax.ShapeDtypeStruct((bsz, s, d), F32),
            jax.ShapeDtypeStruct(((n_tiles + 1) * rows, d), BF16),
            jax.ShapeDtypeStruct(((n_tiles + 1) * rows, LANES), F32),
            jax.ShapeDtypeStruct((n_tiles * tm, LANES), F32),
            jax.ShapeDtypeStruct((n_tiles, 2 * N_GROUPS, LANES), jnp.int32),
        ],
        scratch_shapes=scratch,
        compiler_params=_cparams(("arbitrary", "arbitrary")),
        name="out_proj_ln_sort",
    )(*mixer_args, w, x, gate, ln_g, ln_b, shift, scale, w_router_t, router_bias)


def _conv_attn_operands(proj, conv_w, attn, *, tm):
    specs, args = _gated_conv_operands(proj, conv_w, width=conv_w.shape[-1], tm=tm)
    specs.append(pl.BlockSpec((1, attn.shape[1], tm, LANES), lambda b, i: (b, 0, i, 0)))
    return specs, args + [attn]


def _gmlp_operands(uv, v_g, v_b, w_s, b_s_t, *, tm):
    d = uv.shape[2] // 2
    specs = [
        pl.BlockSpec((1, tm, d), lambda b, i: (b, i, 0)),
        pl.BlockSpec((1, tm, d), lambda b, i: (b, i, 1)),
        pl.BlockSpec((1, d), lambda b, i: (0, 0)),
        pl.BlockSpec((1, d), lambda b, i: (0, 0)),
        pl.BlockSpec(w_s.shape, lambda b, i: (0, 0, 0)),
        pl.BlockSpec(b_s_t.shape, lambda b, i: (0, 0)),
    ]
    return specs, [uv, uv, v_g, v_b, w_s, b_s_t]


def _route(scores, bias):
    sel = scores + bias
    rows = [sel[e:e + 1, :] for e in range(N_EXPERTS)]
    group_score = []
    for g in range(N_GROUPS):
        r = rows[g * EXPERTS_PER_GROUP:(g + 1) * EXPERTS_PER_GROUP]
        best = None
        for a in range(EXPERTS_PER_GROUP):
            for b in range(a + 1, EXPERTS_PER_GROUP):
                pair = r[a] + r[b]
                best = pair if best is None else jnp.maximum(best, pair)
        group_score.append(best)
    one = jnp.ones_like(rows[0])
    zero = jnp.zeros_like(rows[0])
    picked = []
    chosen = []
    for g in range(N_GROUPS):
        beaten = zero
        for o in range(N_GROUPS):
            if o < g:
                beaten = jnp.where(group_score[o] >= group_score[g], one, beaten)
            elif o > g:
                beaten = jnp.where(group_score[o] > group_score[g], one, beaten)
        chosen.append(1.0 - beaten)
        for a in range(EXPERTS_PER_GROUP):
            e = g * EXPERTS_PER_GROUP + a
            rank = zero
            for b in range(EXPERTS_PER_GROUP):
                o = g * EXPERTS_PER_GROUP + b
                if b < a:
                    rank = rank + jnp.where(rows[o] >= rows[e], one, zero)
                elif b > a:
                    rank = rank + jnp.where(rows[o] > rows[e], one, zero)
            picked.append(jnp.where(rank < 2.0, one, zero) * chosen[g])
    w = [picked[e] * scores[e:e + 1, :] for e in range(N_EXPERTS)]
    total = w[0]
    for e in range(1, N_EXPERTS):
        total = total + w[e]
    return [we / total for we in w], chosen


ROW_BLOCK = 32
STEP_BLOCKS = 8
DEST_LANE = EXPERTS_PER_GROUP
SORT_TM = 512


def _sort_rows(tm):
    return tm + N_GROUPS * ROW_BLOCK


def _sort_tile(x, sh, sc, wr_ref, rb_ref, xs_ref, cs_ref, tok_ref, seg_ref):
    tm, d = x.shape
    rows = xs_ref.shape[0]
    col_chunk = 512
    h = (x * (1.0 + sc) + sh).astype(BF16)
    logits = lax.dot_general(wr_ref[...], h, (((1,), (1,)), ((), ())), preferred_element_type=F32)
    comb, chosen = _route(jax.nn.sigmoid(logits), rb_ref[...])
    src = lax.broadcasted_iota(jnp.int32, (tm, tm), 0)
    dst = lax.broadcasted_iota(jnp.int32, (tm, tm), 1)
    before = jnp.where(src < dst, 1.0, 0.0).astype(BF16)
    pad_rows = [jnp.zeros_like(chosen[0])] * (8 - N_GROUPS)
    rank = jnp.dot(jnp.concatenate(chosen + pad_rows, axis=0).astype(BF16), before, preferred_element_type=F32)
    dest = jnp.zeros_like(chosen[0])
    first = jnp.zeros((1, 1), F32)
    firsts, counts = [], []
    for g in range(N_GROUPS):
        count = jnp.sum(chosen[g], axis=1, keepdims=True)
        n_blocks = jnp.floor((count + (ROW_BLOCK - 1)) * (1.0 / ROW_BLOCK))
        firsts.append(first)
        counts.append(n_blocks)
        dest = dest + chosen[g] * (first * ROW_BLOCK + rank[g:g + 1])
        first = first + n_blocks
    seg = jnp.concatenate([jnp.broadcast_to(v, (1, LANES)) for v in firsts + counts], axis=0)
    seg_ref[0] = seg.astype(jnp.int32)
    dest_i = dest.astype(jnp.int32)
    row_id = lax.broadcasted_iota(jnp.int32, (rows, tm), 0)
    p = jnp.where(row_id == dest_i, 1.0, 0.0).astype(BF16)
    for c in range(d // col_chunk):
        cols = slice(c * col_chunk, (c + 1) * col_chunk)
        xs_ref[:, cols] = jnp.dot(p, h[:, cols], preferred_element_type=F32).astype(BF16)
    in_group = []
    for j in range(EXPERTS_PER_GROUP):
        cj = comb[j]
        for g in range(1, N_GROUPS):
            cj = cj + comb[g * EXPERTS_PER_GROUP + j]
        in_group.append(cj)
    c_rows = jnp.concatenate(in_group + [dest, jnp.zeros((LANES - DEST_LANE - 1, tm), F32)], axis=0)
    c_cols = jnp.transpose(c_rows)
    tok_ref[...] = c_cols
    hi = c_cols.astype(BF16)
    lo = (c_cols - hi.astype(F32)).astype(BF16)
    cs = jnp.dot(p, jnp.concatenate([hi, lo], axis=1), preferred_element_type=F32)
    cs_ref[...] = cs[:, :LANES] + cs[:, LANES:]


def _expert_work_table(seg, rows):
    n_tiles = seg.shape[0]
    per_tile = rows // ROW_BLOCK
    spare = n_tiles * per_tile
    max_items = spare // STEP_BLOCKS + N_GROUPS
    first = seg[:, :N_GROUPS, 0]
    count = seg[:, N_GROUPS:, 0]
    tri = jnp.arange(n_tiles)[:, None] >= jnp.arange(n_tiles)[None, :]
    cum = jnp.sum(jnp.where(tri[:, :, None], count[None, :, :], 0), axis=1)
    total = cum[-1]
    items = (total + (STEP_BLOCKS - 1)) // STEP_BLOCKS
    ends = jnp.sum(jnp.where(jnp.arange(N_GROUPS)[:, None] >= jnp.arange(N_GROUPS)[None, :], items[None, :], 0),
                   axis=1)
    n_items = ends[-1]
    step = jnp.arange(max_items + 1)
    live = step < n_items
    ref_step = jnp.where(live, step, jnp.maximum(n_items - 1, 0))
    grp = jnp.sum((ref_step[:, None] >= ends[None, :-1]).astype(jnp.int32), axis=1)
    pick = grp[:, None] == jnp.arange(N_GROUPS)[None, :]
    of_group = lambda v: jnp.sum(jnp.where(pick, v[None, :], 0), axis=1)
    start = of_group(ends - items)
    cum_s = jnp.sum(jnp.where(pick[:, None, :], cum[None, :, :], 0), axis=2)
    count_s = jnp.sum(jnp.where(pick[:, None, :], count[None, :, :], 0), axis=2)
    first_s = jnp.sum(jnp.where(pick[:, None, :], first[None, :, :], 0), axis=2)
    total_s = of_group(total)

    def block_id(k):
        done = cum_s <= k[:, None]
        t = jnp.sum(done.astype(jnp.int32), axis=1)
        skipped = jnp.sum(jnp.where(done, count_s, 0), axis=1)
        at_t = jnp.arange(n_tiles)[None, :] == t[:, None]
        return t * per_tile + jnp.sum(jnp.where(at_t, first_s, 0), axis=1) + (k - skipped)

    as_i32 = lambda v: v.astype(jnp.int32)
    blocks = []
    for u in range(STEP_BLOCKS):
        k_u = STEP_BLOCKS * (step - start) + u
        has_u = live & (k_u < total_s)
        blocks.append(as_i32(jnp.where(has_u, block_id(k_u), blocks[0] if u else spare)))
    local = (jnp.arange(spare) % per_tile)[:, None]
    per_block = lambda v: jnp.repeat(v, per_tile, axis=0)
    first_r, count_r = per_block(first), per_block(count)
    inside = (local >= first_r) & (local < first_r + count_r)
    k_r = per_block(cum - count) + (local - first_r)
    slot = STEP_BLOCKS * (ends - items)[None, :] + k_r
    where = jnp.where(jnp.any(inside, axis=1), jnp.sum(jnp.where(inside, slot, 0), axis=1), STEP_BLOCKS * max_items)
    return (*blocks, as_i32(grp), as_i32(n_items).reshape(1)), as_i32(where)


def _moe_expert_kernel(*refs):
    n_items = refs[STEP_BLOCKS + 1]
    pos = STEP_BLOCKS + 2
    x_refs = refs[pos:pos + STEP_BLOCKS]
    c_refs = refs[pos + STEP_BLOCKS:pos + 2 * STEP_BLOCKS]
    wg_ref, wu_ref, wd_ref, y_ref = refs[pos + 2 * STEP_BLOCKS:]
    f, d = wd_ref.shape[2], wd_ref.shape[3]

    @pl.when(pl.program_id(0) < n_items[0])
    def _():
        x = jnp.concatenate([r[...] for r in x_refs], axis=0)
        c = jnp.concatenate([r[...] for r in c_refs], axis=0)
        pieces = []
        for j in range(EXPERTS_PER_GROUP):
            gate = jnp.dot(x, wg_ref[0, j].astype(BF16), preferred_element_type=F32)
            up = jnp.dot(x, wu_ref[0, j].astype(BF16), preferred_element_type=F32)
            pieces.append((jax.nn.silu(gate) * up * c[:, j:j + 1]).astype(BF16))
        w_down = wd_ref[0].reshape(EXPERTS_PER_GROUP * f, d).astype(BF16)
        y = jnp.dot(jnp.concatenate(pieces, axis=1), w_down, preferred_element_type=F32)
        y_ref[...] = y.astype(BF16)

    @pl.when(pl.program_id(0) >= n_items[0])
    def _():
        y_ref[...] = jnp.zeros(y_ref.shape, BF16)


def _moe_expert_call(table, xs, cs, w_gate, w_up, w_down, layer):
    d = xs.shape[1]
    steps = table[0].shape[0]
    n_pre = len(table)
    grp_at = STEP_BLOCKS
    rows_of = lambda u, width: pl.BlockSpec((ROW_BLOCK, width), lambda i, *pre: (pre[u][i], 0))
    weights = lambda w: pl.BlockSpec((1,) + w.shape[1:],
                                     lambda i, *pre: (layer * N_GROUPS + pre[grp_at][i], 0, 0, 0),
                                     pipeline_mode=pl.Buffered(1))
    grid_spec = pltpu.PrefetchScalarGridSpec(
        num_scalar_prefetch=n_pre,
        grid=(steps,),
        in_specs=[rows_of(u, d) for u in range(STEP_BLOCKS)] + [rows_of(u, LANES) for u in range(STEP_BLOCKS)]
        + [weights(w_gate), weights(w_up), weights(w_down)],
        out_specs=pl.BlockSpec((STEP_BLOCKS * ROW_BLOCK, d), lambda i, *pre: (i, 0)),
    )
    return pl.pallas_call(
        _moe_expert_kernel,
        grid_spec=grid_spec,
        out_shape=jax.ShapeDtypeStruct((steps * STEP_BLOCKS * ROW_BLOCK, d), BF16),
        compiler_params=_cparams(("arbitrary",)),
        name="moe_experts",
    )(*table, *([xs] * STEP_BLOCKS), *([cs] * STEP_BLOCKS), w_gate, w_up, w_down)


def _moe_combine_kernel(where_ref, tok_ref, *refs, alpha, n_blocks):
    y_refs = refs[:n_blocks]
    x_ref, gate_ref, g_ref, b_ref, o_ref = refs[n_blocks:]
    tm = x_ref.shape[1]
    ys = jnp.concatenate([r[...] for r in y_refs], axis=0)
    half = tm // 2
    row_id = lax.broadcasted_iota(jnp.int32, (half, ys.shape[0]), 1)
    for rows in (slice(0, half), slice(half, tm)):
        dest = tok_ref[rows, DEST_LANE:DEST_LANE + 1].astype(jnp.int32)
        p_t = jnp.where(row_id == dest, 1.0, 0.0).astype(BF16)
        y = jnp.dot(p_t, ys, preferred_element_type=F32)
        r = alpha * x_ref[0, rows, :] + gate_ref[0] * y
        o_ref[0, rows, :] = _layer_norm(r, g_ref[...], b_ref[...])


def _moe_combine_call(where, tok, ys, x, gate, ln_g, ln_b, *, alpha, sort_tm, tm):
    bsz, s, d = x.shape
    nt = s // sort_tm
    parts = sort_tm // tm
    n_blocks = _sort_rows(sort_tm) // ROW_BLOCK
    tile = lambda b, i: b * nt + i
    y_spec = lambda j: pl.BlockSpec((ROW_BLOCK, d), lambda b, i, k, w: (w[tile(b, i) * n_blocks + j], 0))
    grid_spec = pltpu.PrefetchScalarGridSpec(
        num_scalar_prefetch=1,
        grid=(bsz, nt, parts),
        in_specs=[pl.BlockSpec((tm, LANES), lambda b, i, k, w: (tile(b, i) * parts + k, 0))]
        + [y_spec(j) for j in range(n_blocks)]
        + [
            pl.BlockSpec((1, tm, d), lambda b, i, k, w: (b, i * parts + k, 0)),
            pl.BlockSpec((1, 1, d), lambda b, i, k, w: (b, 0, 0)),
            pl.BlockSpec((1, d), lambda b, i, k, w: (0, 0)),
            pl.BlockSpec((1, d), lambda b, i, k, w: (0, 0)),
        ],
        out_specs=pl.BlockSpec((1, tm, d), lambda b, i, k, w: (b, i * parts + k, 0)),
    )
    return pl.pallas_call(
        functools.partial(_moe_combine_kernel, alpha=alpha, n_blocks=n_blocks),
        grid_spec=grid_spec,
        out_shape=jax.ShapeDtypeStruct((bsz, s, d), F32),
        compiler_params=_cparams(("arbitrary", "arbitrary", "arbitrary")),
        name="moe_combine",
    )(where, tok, *([ys] * n_blocks), x, gate, ln_g, ln_b)


def _moe(x, sorted_tokens, gate, w_gate, w_up, w_down, layer, ln_g, ln_b, *, alpha):
    sort_tm = min(SORT_TM, x.shape[1])
    xs, cs, tok, seg = sorted_tokens
    table, where = _expert_work_table(seg, _sort_rows(sort_tm))
    by_group = lambda w: w.reshape((w.shape[0] * N_GROUPS, EXPERTS_PER_GROUP) + w.shape[2:])
    ys = _moe_expert_call(table, xs, cs, by_group(w_gate), by_group(w_up), by_group(w_down), layer)
    return _moe_combine_call(where, tok, ys, x, gate, ln_g, ln_b, alpha=alpha, sort_tm=sort_tm,
                             tm=min(ROW_TM, sort_tm))


def _rope_tables(n):
    freqs = LANES // 8
    inv = jnp.power(ROPE_BASE, -jnp.arange(freqs, dtype=F32) / freqs)
    tok = jnp.arange(n)
    ang_r = (tok // GRID_W).astype(F32)[:, None] * inv
    ang_c = (tok % GRID_W).astype(F32)[:, None] * inv
    ang = jnp.concatenate([ang_r, ang_c], axis=1)
    cos = jnp.tile(jnp.cos(ang), (1, 4))
    sin = jnp.tile(jnp.sin(ang), (1, 4))
    sign = jnp.where(jnp.arange(LANES) < LANES // 2, -1.0, 1.0).astype(F32)
    return cos, sin * sign


def _rope_column_perm():
    perm = np.zeros((DA_HEADS, 2, 2, 2, 16), np.int32)
    for h in range(DA_HEADS):
        for p in range(2):
            for m in range(2):
                for ax in range(2):
                    for f in range(16):
                        perm[h, p, m, ax, f] = h * LANES + m * 64 + ax * 32 + p * 16 + f
    return perm.reshape(-1)


def kernel(x, c, ctx, c_ctx, w_mod, b_mod, ln_g, ln_b, w_in_e, conv_w, lambda_q1, lambda_k1, lambda_q2, lambda_k2,
           subln_g, w_out_e, w_in_o, v_ln_g, v_ln_b, w_spatial, b_spatial, w_out_o, w_router, router_bias,
           w_gate, w_up, w_down):
    bsz, s, d = x.shape
    depth = w_mod.shape[0]
    assert depth == 2 and w_in_e.shape[0] == 1 and w_in_o.shape[0] == 1, "two-layer (even, odd) stack only"
    conv_dim = conv_w.shape[-1]
    q_dim = DA_HEADS * LANES
    q_col, k_col, v_col = 3 * conv_dim, 3 * conv_dim + q_dim, 3 * conv_dim + 2 * q_dim
    assert w_in_e.shape[2] == v_col + q_dim and s % GRID_W == 0
    alpha = float((2 * depth) ** 0.25)

    rows = 8 * ((bsz + 1 + 7) // 8)
    cond = jnp.zeros((rows, d), F32).at[:bsz].set(c).at[bsz].set(c_ctx)
    mods = _mod_call(cond, w_mod, b_mod)

    def mod_vec(l, k, ctx_row=False):
        v = mods[l, bsz:bsz + 1, k * d:(k + 1) * d] if ctx_row else mods[l, :bsz, k * d:(k + 1) * d]
        return v.reshape(-1, 1, d)

    w_router_t = w_router.T.astype(BF16)
    rbias = router_bias.reshape(-1, 1).astype(F32)

    perm = _rope_column_perm()
    w_in = w_in_e[0]
    w_in = jnp.concatenate(
        [w_in[:, :q_col], w_in[:, q_col:k_col][:, perm], w_in[:, k_col:v_col][:, perm], w_in[:, v_col:]],
        axis=1).astype(BF16)
    cos_t, sin_t = _rope_tables(s)
    tn = q_dim
    assert q_col % tn == 0
    q_tile, k_tile = q_col // tn, k_col // tn
    conv_proj, q, k, v = _proj_call(
        x, mod_vec(0, 0), mod_vec(0, 1), w_in, tm=PROJ_TM, tn=tn, n_flat=q_tile,
        rope=(cos_t, sin_t, (q_tile, k_tile), q_tile, float(64 ** -0.5 * math.log2(math.e))))
    kc, vc = _proj_call(ctx, mod_vec(0, 0, True), mod_vec(0, 1, True), w_in[:, k_col:], tm=PROJ_TM, tn=tn, n_flat=0)

    lam_init = 0.8 - 0.6 * math.exp(-0.3 * 0)
    lam_params = jnp.zeros((8, LANES), F32)
    for r, p in enumerate((lambda_q1, lambda_k1, lambda_q2, lambda_k2)):
        lam_params = lam_params.at[r, :p.shape[-1]].set(p[0].astype(F32))
    attn = _attn_call(q, k, v, kc, vc, lam_params, subln_g[0].reshape(1, LANES), lam_init=lam_init, tq=ATTN_TQ)
    assert ROW_TM == SORT_TM
    x, *sorted_tokens = _out_ln_call(
        "conv_attn", *_conv_attn_operands(conv_proj, conv_w[0], attn, tm=ROW_TM), w_out_e[0].astype(BF16),
        x, mod_vec(0, 2), ln_g[0, 0:1], ln_b[0, 0:1], mod_vec(0, 3), mod_vec(0, 4), w_router_t, rbias,
        alpha=alpha, tm=min(ROW_TM, s))
    x = _moe(x, sorted_tokens, mod_vec(0, 5), w_gate, w_up, w_down, 0, ln_g[0, 1:2], ln_b[0, 1:2], alpha=alpha)

    w_gmlp = w_in_o[0].astype(BF16)
    uv, = _proj_call(x, mod_vec(1, 0), mod_vec(1, 1), w_gmlp, tm=PROJ_TM, tn=tn, n_flat=w_gmlp.shape[1] // tn,
                     gelu=True)
    gmlp = _gmlp_operands(uv, v_ln_g[0:1], v_ln_b[0:1], w_spatial[0].astype(BF16), b_spatial[0].T, tm=ROW_TM)
    x, *sorted_tokens = _out_ln_call(
        "gmlp", *gmlp, w_out_o[0].astype(BF16), x, mod_vec(1, 2), ln_g[1, 0:1], ln_b[1, 0:1],
        mod_vec(1, 3), mod_vec(1, 4), w_router_t, rbias, alpha=alpha, tm=min(ROW_TM, s))
    x = _moe(x, sorted_tokens, mod_vec(1, 5), w_gate, w_up, w_down, 1, ln_g[1, 1:2], ln_b[1, 1:2], alpha=alpha)
    return x
```
